```python
import jax, jax.numpy as jnp
from jax import lax
import numpy as np

D_MODEL = 2048
BATCH = 4
SEQ = 8192
DEPTH = 2

EPS = 1e-6
N_BRANCH = 3
BRANCH_WIDTH = D_MODEL // 4
SGU_GROUPS = 8
SGU_GROUP_DIM = BRANCH_WIDTH // SGU_GROUPS
SGU_CHUNK = 128
MOBA_HEADS = 8
MOBA_HEAD_DIM = BRANCH_WIDTH // MOBA_HEADS
MOBA_BLOCK = 256
MOBA_TOPK = 3
MOBA_QBLOCK = 64
POOL_WINDOWS = (2, 4, 8, 16)
POOL_GROUPS = 4
POOL_GROUP_DIM = BRANCH_WIDTH // POOL_GROUPS
IN_SPLIT_POINTS = tuple(BRANCH_WIDTH * i for i in range(1, 7))
IN_WIDTH = 6 * BRANCH_WIDTH + N_BRANCH * D_MODEL
D_FF = 11 * D_MODEL // 4
CONV_WIDTH = 3
N_MOD = 6

kernel_name = "hybrid_sgu_moba_pool_block"


def rms_norm(x, g):
    xf = x.astype(jnp.float32)
    y = xf * lax.rsqrt(jnp.mean(xf * xf, axis=-1, keepdims=True) + EPS)
    return (y * g.astype(jnp.float32)).astype(x.dtype)


def alibi_slopes(n_heads):
    return jnp.exp2(-(8.0 / n_heads) * jnp.arange(1, n_heads + 1, dtype=jnp.float32))


def spatial_gating(u, v, norm_g, w_s, b_s):
    Bsz, S, _ = u.shape
    vf = v.astype(jnp.float32)
    mu = jnp.mean(vf, axis=-1, keepdims=True)
    var = jnp.mean(jnp.square(vf - mu), axis=-1, keepdims=True)
    vn = (vf - mu) * lax.rsqrt(var + EPS) * norm_g.astype(jnp.float32)
    vc = vn.reshape(Bsz, S // SGU_CHUNK, SGU_CHUNK, SGU_GROUPS, SGU_GROUP_DIM)
    mask = jnp.tril(jnp.ones((SGU_CHUNK, SGU_CHUNK), jnp.float32))
    w = w_s.astype(jnp.float32) * mask
    mixed = jnp.einsum('gts,bcsgd->bctgd', w, vc) + b_s.astype(jnp.float32).T[None, None, :, :, None]
    return u * mixed.reshape(Bsz, S, BRANCH_WIDTH).astype(u.dtype)


def moba_attention(q, k, v):
    Bsz, S, _ = q.shape
    H, Dh, BLK, QB = MOBA_HEADS, MOBA_HEAD_DIM, MOBA_BLOCK, MOBA_QBLOCK
    nb = -(-S // BLK)
    pad = nb * BLK - S

    def heads(t):
        return t.reshape(Bsz, S, H, Dh).transpose(0, 2, 1, 3)

    qh = heads(q)
    kb = jnp.pad(heads(k), ((0, 0), (0, 0), (0, pad), (0, 0))).reshape(Bsz, H, nb, BLK, Dh)
    vb = jnp.pad(heads(v), ((0, 0), (0, 0), (0, pad), (0, 0))).reshape(Bsz, H, nb, BLK, Dh)
    kmean = jnp.mean(kb.astype(jnp.float32), axis=3)
    slopes = alibi_slopes(H)[None, :, None, None]
    k_sel = min(MOBA_TOPK, nb)
    scale = Dh ** -0.5
    bi = jnp.arange(Bsz)[:, None, None, None]
    hi = jnp.arange(H)[None, :, None, None]
    key_off = jnp.arange(BLK)
    n_q = S // QB
    q_chunks = qh.reshape(Bsz, H, n_q, QB, Dh).transpose(2, 0, 1, 3, 4)

    def attend(args):
        qc, ci = args
        t = ci * QB + jnp.arange(QB)
        own = (ci * QB) // BLK
        blk_scores = jnp.einsum('bhqd,bhnd->bhqn', qc.astype(jnp.float32), kmean)
        blk_scores = jnp.where(jnp.arange(nb) < own, blk_scores, -jnp.inf)
        _, idx = lax.top_k(blk_scores, k_sel)
        ksel = kb[bi, hi, idx]
        vsel = vb[bi, hi, idx]
        s_past = jnp.einsum('bhqd,bhqjsd->bhqjs', qc, ksel,
                            preferred_element_type=jnp.float32) * scale
        pos_past = idx[..., None] * BLK + key_off
        dist_past = (t[None, None, :, None, None] - pos_past).astype(jnp.float32)
        s_past = s_past - slopes[..., None] * dist_past
        valid = (jnp.arange(k_sel) < own)[:, None]
        s_past = jnp.where(valid, s_past, -jnp.inf).reshape(Bsz, H, QB, k_sel * BLK)
        k_own = lax.dynamic_index_in_dim(kb, own, axis=2, keepdims=False)
        v_own = lax.dynamic_index_in_dim(vb, own, axis=2, keepdims=False)
        dist_own = t[:, None] - (own * BLK + key_off)[None, :]
        s_own = jnp.einsum('bhqd,bhsd->bhqs', qc, k_own,
                           preferred_element_type=jnp.float32) * scale
        s_own = s_own - slopes * dist_own.astype(jnp.float32)
        s_own = jnp.where(dist_own >= 0, s_own, -jnp.inf)
        p = jax.nn.softmax(jnp.concatenate([s_past, s_own], axis=-1), axis=-1)
        p_past = p[..., :k_sel * BLK].reshape(Bsz, H, QB, k_sel, BLK).astype(v.dtype)
        p_own = p[..., k_sel * BLK:].astype(v.dtype)
        o = (jnp.einsum('bhqjs,bhqjsd->bhqd', p_past, vsel, preferred_element_type=jnp.float32)
             + jnp.einsum('bhqs,bhsd->bhqd', p_own, v_own, preferred_element_type=jnp.float32))
        return o.astype(v.dtype)

    out = lax.map(attend, (q_chunks, jnp.arange(n_q)))
    return out.transpose(1, 0, 3, 2, 4).reshape(Bsz, S, H * Dh)


def multiscale_pool(z, pool_w, pool_scale):
    Bsz, S, _ = z.shape
    zf = z.astype(jnp.float32).reshape(Bsz, S, POOL_GROUPS, POOL_GROUP_DIM)
    cs = jnp.pad(jnp.cumsum(zf, axis=1), ((0, 0), (1, 0), (0, 0), (0, 0)))
    t = jnp.arange(S)
    means = []
    for gi, w in enumerate(POOL_WINDOWS):
        lo = jnp.maximum(t + 1 - w, 0)
        win_sum = cs[:, 1:, gi] - jnp.take(cs[:, :, gi], lo, axis=1)
        count = jnp.minimum(t + 1, w).astype(jnp.float32)
        means.append(win_sum / count[None, :, None])
    pooled = jnp.stack(means, axis=2) - zf
    y = jnp.einsum('bsgc,gcd->bsgd', pooled, pool_w.astype(jnp.float32))
    return (y.reshape(Bsz, S, BRANCH_WIDTH) * pool_scale.astype(jnp.float32)).astype(z.dtype)


def token_mixing(h, w_in, b_in, sgu_norm_g, sgu_w, sgu_b, pool_w, pool_scale,
                 w_sgu_out, w_moba_out, w_pool_out, w_out):
    p = h @ w_in + b_in
    u, v, q, k, v_att, z_pool, gates = jnp.split(p, IN_SPLIT_POINTS, axis=-1)
    y_sgu = spatial_gating(jax.nn.gelu(u), jax.nn.gelu(v), sgu_norm_g, sgu_w, sgu_b)
    y_moba = moba_attention(q, k, v_att)
    y_pool = multiscale_pool(z_pool, pool_w, pool_scale)
    g_sgu, g_moba, g_pool = jnp.split(jax.nn.sigmoid(gates), N_BRANCH, axis=-1)
    merged = (g_sgu * (y_sgu @ w_sgu_out) + g_moba * (y_moba @ w_moba_out)
              + g_pool * (y_pool @ w_pool_out))
    return merged @ w_out


def conv_ffn(h, w_up, w_conv, b_conv, w_down):
    z = h @ w_up
    z1 = jnp.pad(z[:, :-1], ((0, 0), (1, 0), (0, 0)))
    z2 = jnp.pad(z[:, :-2], ((0, 0), (2, 0), (0, 0)))
    z = w_conv[0] * z2 + w_conv[1] * z1 + w_conv[2] * z + b_conv
    gate, val = jnp.split(z, 2, axis=-1)
    return (jax.nn.gelu(gate) * val) @ w_down


def setup_inputs(seed: int = 0) -> dict:
    key = jax.random.key(seed)
    ks = jax.random.split(key, 24)
    f32 = jnp.float32
    L, D = DEPTH, D_MODEL

    def nrm(k, shape, s):
        return jax.random.normal(k, shape, f32) * s

    return {
        "x": nrm(ks[0], (BATCH, SEQ, D), 1.0),
        "c": nrm(ks[1], (BATCH, D), 1.0),
        "g_pre_mix": 1.0 + nrm(ks[2], (L, D), 0.02),
        "g_post_mix": 1.0 + nrm(ks[3], (L, D), 0.02),
        "g_pre_ffn": 1.0 + nrm(ks[4], (L, D), 0.02),
        "g_post_ffn": 1.0 + nrm(ks[5], (L, D), 0.02),
        "w_ada": nrm(ks[6], (L, D, N_MOD * D), D ** -0.5),
        "b_ada": nrm(ks[7], (L, N_MOD * D), 0.01),
        "w_in": nrm(ks[8], (L, D, IN_WIDTH), D ** -0.5),
        "b_in": nrm(ks[9], (L, IN_WIDTH), 0.01),
        "sgu_norm_g": 1.0 + nrm(ks[10], (L, BRANCH_WIDTH), 0.02),
        "sgu_w": nrm(ks[11], (L, SGU_GROUPS, SGU_CHUNK, SGU_CHUNK), SGU_CHUNK ** -0.5),
        "sgu_b": 1.0 + nrm(ks[12], (L, SGU_GROUPS, SGU_CHUNK), 0.02),
        "pool_w": nrm(ks[13], (L, POOL_GROUPS, POOL_GROUP_DIM, POOL_GROUP_DIM), POOL_GROUP_DIM ** -0.5),
        "pool_scale": 1.0 + nrm(ks[14], (L, BRANCH_WIDTH), 0.02),
        "w_sgu_out": nrm(ks[15], (L, BRANCH_WIDTH, D), BRANCH_WIDTH ** -0.5),
        "w_moba_out": nrm(ks[16], (L, BRANCH_WIDTH, D), BRANCH_WIDTH ** -0.5),
        "w_pool_out": nrm(ks[17], (L, BRANCH_WIDTH, D), BRANCH_WIDTH ** -0.5),
        "w_out": nrm(ks[18], (L, D, D), D ** -0.5),
        "w_up": nrm(ks[19], (L, D, 2 * D_FF), D ** -0.5),
        "w_conv": nrm(ks[20], (L, CONV_WIDTH, 2 * D_FF), CONV_WIDTH ** -0.5),
        "b_conv": nrm(ks[21], (L, 2 * D_FF), 0.01),
        "w_down": nrm(ks[22], (L, D_FF, D), D_FF ** -0.5),
    }


def reference(x, c, g_pre_mix, g_post_mix, g_pre_ffn, g_post_ffn, w_ada, b_ada,
              w_in, b_in, sgu_norm_g, sgu_w, sgu_b, pool_w, pool_scale,
              w_sgu_out, w_moba_out, w_pool_out, w_out, w_up, w_conv, b_conv, w_down):
    cond = jax.nn.silu(c)
    for l in range(DEPTH):
        mod = cond @ w_ada[l] + b_ada[l]
        sh1, sc1, gt1, sh2, sc2, gt2 = [m[:, None, :] for m in jnp.split(mod, N_MOD, axis=-1)]
        h = rms_norm(x, g_pre_mix[l]) * (1.0 + sc1) + sh1
        y = token_mixing(h, w_in[l], b_in[l], sgu_norm_g[l], sgu_w[l], sgu_b[l],
                         pool_w[l], pool_scale[l], w_sgu_out[l], w_moba_out[l],
                         w_pool_out[l], w_out[l])
        x = x + gt1 * rms_norm(y, g_post_mix[l])
        h = rms_norm(x, g_pre_ffn[l]) * (1.0 + sc2) + sh2
        y = conv_ffn(h, w_up[l], w_conv[l], b_conv[l], w_down[l])
        x = x + gt2 * rms_norm(y, g_post_ffn[l])
    return x
```

```python
import functools

import jax
import jax.numpy as jnp
from jax import lax
from jax.experimental import pallas as pl
from jax.experimental.pallas import tpu as pltpu

F32 = jnp.float32
BF16 = jnp.bfloat16

D_MODEL = 2048
BATCH = 4
SEQ = 8192
DEPTH = 2
EPS = 1e-6
BRANCH_WIDTH = 512
SGU_GROUPS = 8
SGU_GROUP_DIM = 64
SGU_CHUNK = 128
MOBA_HEADS = 8
MOBA_HEAD_DIM = 64
MOBA_BLOCK = 256
MOBA_TOPK = 3
POOL_WINDOWS = (2, 4, 8, 16)
POOL_GROUPS = 4
POOL_GROUP_DIM = 128
IN_WIDTH = 6 * BRANCH_WIDTH + 3 * D_MODEL
D_FF = 5632
N_MOD = 6
M_ROWS = BATCH * SEQ
N_KV_BLOCKS = SEQ // MOBA_BLOCK

VMEM_LIMIT_BYTES = 56 * 1024 * 1024
LANES = 128
NEG = -1e30

MOD_TN = 1024
INPROJ_TM, INPROJ_TN = 1024, 1024
BRANCH_TR = 512
ATTN_TQ = MOBA_BLOCK
MERGE_TM = 512
UP_TM, UP_TN = 1024, 512
CONV_HALO = 16
DOWN_TM, DOWN_TK = 512, 1408
POOL_HALO = 16


def _gelu_tanh(x):
    c = 0.7978845608028654
    return 0.5 * x * (1.0 + jnp.tanh(c * (x + 0.044715 * (x * x * x))))


def _sigmoid(x):
    return 1.0 / (1.0 + jnp.exp(-x))


def _rms_norm(x, g):
    ms = jnp.mean(x * x, axis=-1, keepdims=True)
    return x * lax.rsqrt(ms + EPS) * g


def _dot(a, b):
    return jnp.dot(a, b, preferred_element_type=F32)


def _dot_nt(a, b):
    return lax.dot_general(a, b, (((1,), (1,)), ((), ())), preferred_element_type=F32)


def _params(*sem):
    return pltpu.CompilerParams(dimension_semantics=sem, vmem_limit_bytes=VMEM_LIMIT_BYTES)


def _mod_kernel(c_ref, w_ref, b_ref, o_ref):
    c = c_ref[...]
    cond = (c * _sigmoid(c)).astype(BF16)
    o_ref[0] = _dot(cond, w_ref[0].astype(BF16)) + b_ref[0]


def _modulation(c_pad, w_ada, b_ada):
    rows = c_pad.shape[0]
    n = N_MOD * D_MODEL
    return pl.pallas_call(
        _mod_kernel,
        grid=(DEPTH, n // MOD_TN),
        in_specs=[
            pl.BlockSpec((rows, D_MODEL), lambda l, j: (0, 0)),
            pl.BlockSpec((1, D_MODEL, MOD_TN), lambda l, j: (l, 0, j)),
            pl.BlockSpec((1, 1, MOD_TN), lambda l, j: (l, 0, j)),
        ],
        out_specs=pl.BlockSpec((1, rows, MOD_TN), lambda l, j: (l, 0, j)),
        out_shape=jax.ShapeDtypeStruct((DEPTH, rows, n), F32),
        compiler_params=_params("arbitrary", "arbitrary"),
        name="adaln_mod",
    )(c_pad, w_ada, b_ada.reshape(DEPTH, 1, n))


def _inproj_kernel(x_ref, g_ref, sc_ref, sh_ref, w_ref, b_ref, p_ref, kmean_ref, h_ref):
    j = pl.program_id(1)

    @pl.when(j == 0)
    def _():
        h = _rms_norm(x_ref[...], g_ref[...]) * (1.0 + sc_ref[0]) + sh_ref[0]
        h_ref[...] = h.astype(BF16)

    acc = _dot(h_ref[...], w_ref[...]) + b_ref[...]

    @pl.when(j == 0)
    def _():
        p_ref[...] = _gelu_tanh(acc).astype(BF16)

    @pl.when(j == 1)
    def _():
        p_ref[...] = acc.astype(BF16)
        k = acc[:, BRANCH_WIDTH:]
        for r in range(INPROJ_TM // MOBA_BLOCK):
            kmean_ref[0, r:r + 1, :] = jnp.mean(
                k[r * MOBA_BLOCK:(r + 1) * MOBA_BLOCK], axis=0, keepdims=True)

    @pl.when(j == 2)
    def _():
        p_ref[...] = acc.astype(BF16)

    @pl.when(j >= 3)
    def _():
        p_ref[...] = _sigmoid(acc).astype(BF16)


def _inproj(x, g, scale, shift, w, b):
    tiles_per_seq = SEQ // INPROJ_TM
    blocks_per_tile = INPROJ_TM // MOBA_BLOCK
    return pl.pallas_call(
        _inproj_kernel,
        grid=(M_ROWS // INPROJ_TM, IN_WIDTH // INPROJ_TN),
        in_specs=[
            pl.BlockSpec((INPROJ_TM, D_MODEL), lambda i, j: (i, 0)),
            pl.BlockSpec((1, D_MODEL), lambda i, j: (0, 0)),
            pl.BlockSpec((1, 1, D_MODEL), lambda i, j: (i // tiles_per_seq, 0, 0)),
            pl.BlockSpec((1, 1, D_MODEL), lambda i, j: (i // tiles_per_seq, 0, 0)),
            pl.BlockSpec((D_MODEL, INPROJ_TN), lambda i, j: (0, j)),
            pl.BlockSpec((1, INPROJ_TN), lambda i, j: (0, j)),
        ],
        out_specs=[
            pl.BlockSpec((INPROJ_TM, INPROJ_TN), lambda i, j: (i, j)),
            pl.BlockSpec((1, blocks_per_tile, BRANCH_WIDTH), lambda i, j: (i, 0, 0)),
        ],
        out_shape=[
            jax.ShapeDtypeStruct((M_ROWS, IN_WIDTH), BF16),
            jax.ShapeDtypeStruct((M_ROWS // INPROJ_TM, blocks_per_tile, BRANCH_WIDTH), F32),
        ],
        scratch_shapes=[pltpu.VMEM((INPROJ_TM, D_MODEL), BF16)],
        compiler_params=_params("arbitrary", "arbitrary"),
        name="inproj",
    )(x, g, scale, shift, w, b)


def _branch_kernel(u_ref, v_ref, z_ref, zprev_ref, ng_ref, sw_ref, sb_ref, pw_ref, ps_ref,
                   ysgu_ref, ypool_ref):
    i = pl.program_id(0)
    tr = BRANCH_TR
    first = (i % (SEQ // tr)) == 0

    v = v_ref[...].astype(F32)
    mu = jnp.mean(v, axis=-1, keepdims=True)
    vc = v - mu
    var = jnp.mean(vc * vc, axis=-1, keepdims=True)
    vn = (vc * lax.rsqrt(var + EPS) * ng_ref[...]).astype(BF16)

    row = lax.broadcasted_iota(jnp.int32, (SGU_CHUNK, SGU_CHUNK), 0)
    col = lax.broadcasted_iota(jnp.int32, (SGU_CHUNK, SGU_CHUNK), 1)
    causal = row >= col
    wm = [jnp.where(causal, sw_ref[g], 0.0).astype(BF16) for g in range(SGU_GROUPS)]
    lane = lax.broadcasted_iota(jnp.int32, (SGU_CHUNK, LANES), 1)
    low_half = lane < SGU_GROUP_DIM
    zero = jnp.zeros((SGU_CHUNK, LANES), BF16)

    for c in range(tr // SGU_CHUNK):
        rows = slice(c * SGU_CHUNK, (c + 1) * SGU_CHUNK)
        for pr in range(SGU_GROUPS // 2):
            cols = slice(pr * LANES, (pr + 1) * LANES)
            blk = vn[rows, cols]
            mixed = (_dot(wm[2 * pr], jnp.where(low_half, blk, zero))
                     + _dot(wm[2 * pr + 1], jnp.where(low_half, zero, blk))
                     + sb_ref[:, cols])
            ysgu_ref[rows, cols] = (u_ref[rows, cols].astype(F32) * mixed).astype(BF16)

    z = z_ref[...].astype(F32)
    zprev = jnp.where(first, 0.0, zprev_ref[...].astype(F32))
    pos = (i % (SEQ // tr)) * tr + lax.broadcasted_iota(jnp.int32, (tr, POOL_GROUP_DIM), 0)
    for gi, w in enumerate(POOL_WINDOWS):
        cols = slice(gi * POOL_GROUP_DIM, (gi + 1) * POOL_GROUP_DIM)
        zg = z[:, cols]
        cur = jnp.concatenate([zprev[:, cols], zg], axis=0)
        d = 1
        while d < w:
            cur = cur[d:] + cur[:-d]
            d *= 2
        win_sum = cur[cur.shape[0] - tr:]
        count = jnp.minimum(pos + 1, w).astype(F32)
        pooled = win_sum / count - zg
        y = _dot(pooled.astype(BF16), pw_ref[gi].astype(BF16)) * ps_ref[:, cols]
        ypool_ref[:, cols] = y.astype(BF16)


def _branches(p, sgu_norm_g, sgu_w, sgu_bias_full, pool_w, pool_scale):
    tr = BRANCH_TR
    halo_blocks = tr // POOL_HALO
    return pl.pallas_call(
        _branch_kernel,
        grid=(M_ROWS // tr,),
        in_specs=[
            pl.BlockSpec((tr, BRANCH_WIDTH), lambda i: (i, 0)),
            pl.BlockSpec((tr, BRANCH_WIDTH), lambda i: (i, 1)),
            pl.BlockSpec((tr, BRANCH_WIDTH), lambda i: (i, 5)),
            pl.BlockSpec((POOL_HALO, BRANCH_WIDTH),
                         lambda i: (jnp.maximum(i * halo_blocks - 1, 0), 5)),
            pl.BlockSpec((1, BRANCH_WIDTH), lambda i: (0, 0)),
            pl.BlockSpec((SGU_GROUPS, SGU_CHUNK, SGU_CHUNK), lambda i: (0, 0, 0)),
            pl.BlockSpec((SGU_CHUNK, BRANCH_WIDTH), lambda i: (0, 0)),
            pl.BlockSpec((POOL_GROUPS, POOL_GROUP_DIM, POOL_GROUP_DIM), lambda i: (0, 0, 0)),
            pl.BlockSpec((1, BRANCH_WIDTH), lambda i: (0, 0)),
        ],
        out_specs=[
            pl.BlockSpec((tr, BRANCH_WIDTH), lambda i: (i, 0)),
            pl.BlockSpec((tr, BRANCH_WIDTH), lambda i: (i, 0)),
        ],
        out_shape=[
            jax.ShapeDtypeStruct((M_ROWS, BRANCH_WIDTH), BF16),
            jax.ShapeDtypeStruct((M_ROWS, BRANCH_WIDTH), BF16),
        ],
        compiler_params=_params("arbitrary"),
        name="sgu_pool",
    )(p, p, p, p, sgu_norm_g, sgu_w, sgu_bias_full, pool_w, pool_scale)


def _attn_kernel(q_ref, k_ref, v_ref, km_ref, o_ref):
    own = pl.program_id(1)
    tq, blk = ATTN_TQ, MOBA_BLOCK
    own_f = own.astype(F32)

    lane = lax.broadcasted_iota(jnp.int32, (tq, LANES), 1)
    low_half = lane < MOBA_HEAD_DIM
    lane_f = lane.astype(F32)
    past = lane < own
    rel = (lax.broadcasted_iota(jnp.int32, (tq, blk), 1)
           - lax.broadcasted_iota(jnp.int32, (tq, blk), 0))
    rel_f = rel.astype(F32)
    own_start = pl.multiple_of(own * blk, blk)

    for pr in range(MOBA_HEADS // 2):
        cols = slice(pr * LANES, (pr + 1) * LANES)
        qp = q_ref[:, cols]
        km = km_ref[0, :, cols]
        km_hi = km.astype(BF16)
        km_lo = (km - km_hi.astype(F32)).astype(BF16)
        outs = []
        for hh in range(2):
            head = 2 * pr + hh
            slope = 2.0 ** -(head + 1)
            qh = jnp.where(low_half if hh == 0 else ~low_half, qp, jnp.zeros_like(qp))

            sc = _dot_nt(qh, km_hi) + _dot_nt(qh, km_lo)
            sc = jnp.where(past, sc, -jnp.inf)
            picks = []
            for _ in range(MOBA_TOPK):
                best = jnp.max(sc, axis=1, keepdims=True)
                idx = jnp.min(jnp.where(sc == best, lane_f, 1e9), axis=1, keepdims=True)
                picks.append(jnp.where(best > -jnp.inf, idx, -1.0))
                sc = jnp.where(lane_f == idx, -jnp.inf, sc)

            qs = qh * (MOBA_HEAD_DIM ** -0.5)
            bias = slope * rel_f

            s = _dot_nt(qs, k_ref[pl.ds(own_start, blk), cols]) + bias
            s = jnp.where(rel > 0, NEG, s)
            m0 = jnp.max(s, axis=1, keepdims=True)
            p0 = jnp.exp(s - m0)
            l0 = jnp.sum(p0, axis=1, keepdims=True)
            acc0 = _dot(p0.astype(BF16), v_ref[pl.ds(own_start, blk), cols])

            def body(j, carry, qs=qs, bias=bias, picks=picks, slope=slope, cols=cols):
                m, l, acc = carry
                jf = j.astype(F32)
                start = pl.multiple_of(j * blk, blk)
                s = _dot_nt(qs, k_ref[pl.ds(start, blk), cols]) + bias
                chosen = (picks[0] == jf) | (picks[1] == jf) | (picks[2] == jf)
                shift = jnp.where(chosen, (-slope * blk) * (own_f - jf), NEG)
                m_new = jnp.maximum(m, jnp.max(s, axis=1, keepdims=True) + shift)
                p = jnp.exp(s - (m_new - shift))
                alpha = jnp.exp(m - m_new)
                l = alpha * l + jnp.sum(p, axis=1, keepdims=True)
                acc = alpha * acc + _dot(p.astype(BF16), v_ref[pl.ds(start, blk), cols])
                return m_new, l, acc

            _, l, acc = lax.fori_loop(0, own, body, (m0, l0, acc0))
            outs.append(acc / l)
        o_ref[:, cols] = jnp.where(low_half, outs[0], outs[1]).astype(BF16)


def _attention(p, kmean_pad):
    tq = ATTN_TQ
    q_tiles = SEQ // tq
    resident = functools.partial(pl.BlockSpec, pipeline_mode=pl.Buffered(1))
    return pl.pallas_call(
        _attn_kernel,
        grid=(BATCH, q_tiles),
        in_specs=[
            pl.BlockSpec((tq, BRANCH_WIDTH), lambda b, i: (b * q_tiles + i, 2)),
            resident((SEQ, BRANCH_WIDTH), lambda b, i: (b, 3)),
            resident((SEQ, BRANCH_WIDTH), lambda b, i: (b, 4)),
            pl.BlockSpec((1, LANES, BRANCH_WIDTH), lambda b, i: (b, 0, 0)),
        ],
        out_specs=pl.BlockSpec((tq, BRANCH_WIDTH), lambda b, i: (b * q_tiles + i, 0)),
        out_shape=jax.ShapeDtypeStruct((M_ROWS, BRANCH_WIDTH), BF16),
        compiler_params=_params("arbitrary", "arbitrary"),
        name="moba_attention",
    )(p, p, p, kmean_pad)


def _merge_kernel(ys_ref, ym_ref, yp_ref, gs0, gs1, gm0, gm1, gp0, gp1,
                  ws_ref, wm_ref, wp_ref, wo_ref, x_ref, g_ref, gt_ref, o_ref):
    ys, ym, yp = ys_ref[...], ym_ref[...], yp_ref[...]
    halves = []
    half_w = D_MODEL // 2
    for hf, (gs, gm, gp) in enumerate(((gs0, gm0, gp0), (gs1, gm1, gp1))):
        cols = slice(hf * half_w, (hf + 1) * half_w)
        merged = (gs[...].astype(F32) * _dot(ys, ws_ref[:, cols])
                  + gm[...].astype(F32) * _dot(ym, wm_ref[:, cols])
                  + gp[...].astype(F32) * _dot(yp, wp_ref[:, cols]))
        halves.append(merged.astype(BF16))
    merged = jnp.concatenate(halves, axis=1)
    y = _dot(merged, wo_ref[...])
    o_ref[...] = x_ref[...] + gt_ref[0] * _rms_norm(y, g_ref[...])


def _merge(ysgu, ymoba, ypool, p, w_sgu_out, w_moba_out, w_pool_out, w_out, x, g, gate):
    tm = MERGE_TM
    tiles_per_seq = SEQ // tm
    half_w = D_MODEL // 2
    gate_col0 = 6 * BRANCH_WIDTH // half_w
    resident = functools.partial(pl.BlockSpec, pipeline_mode=pl.Buffered(1))
    branch = pl.BlockSpec((tm, BRANCH_WIDTH), lambda i: (i, 0))
    gate_specs = [pl.BlockSpec((tm, half_w), functools.partial(lambda i, c: (i, c), c=gate_col0 + c))
                  for c in range(6)]
    w_branch = resident((BRANCH_WIDTH, D_MODEL), lambda i: (0, 0))
    return pl.pallas_call(
        _merge_kernel,
        grid=(M_ROWS // tm,),
        in_specs=[branch, branch, branch, *gate_specs, w_branch, w_branch, w_branch,
                  resident((D_MODEL, D_MODEL), lambda i: (0, 0)),
                  pl.BlockSpec((tm, D_MODEL), lambda i: (i, 0)),
                  pl.BlockSpec((1, D_MODEL), lambda i: (0, 0)),
                  pl.BlockSpec((1, 1, D_MODEL), lambda i: (i // tiles_per_seq, 0, 0))],
        out_specs=pl.BlockSpec((tm, D_MODEL), lambda i: (i, 0)),
        out_shape=jax.ShapeDtypeStruct((M_ROWS, D_MODEL), F32),
        compiler_params=_params("arbitrary"),
        name="merge_outproj",
    )(ysgu, ymoba, ypool, p, p, p, p, p, p, w_sgu_out, w_moba_out, w_pool_out, w_out, x, g, gate)


def _up_kernel(x_ref, xh_ref, g_ref, sc_ref, sh_ref, wg_ref, wv_ref, cg_ref, cv_ref, bg_ref, bv_ref,
               a_ref, h_ref):
    i = pl.program_id(0)
    j = pl.program_id(1)
    tm, halo = UP_TM, CONV_HALO

    @pl.when(j == 0)
    def _():
        scale = 1.0 + sc_ref[0]
        shift = sh_ref[0]
        h_ref[halo:, :] = (_rms_norm(x_ref[...], g_ref[...]) * scale + shift).astype(BF16)
        first = (i % (SEQ // tm)) == 0
        hh = _rms_norm(xh_ref[...], g_ref[...]) * scale + shift
        h_ref[:halo, :] = jnp.where(first, 0.0, hh).astype(BF16)

    h = h_ref[...]

    def conv(z, cw, cb):
        return (cw[0:1, :] * z[halo - 2:halo - 2 + tm] + cw[1:2, :] * z[halo - 1:halo - 1 + tm]
                + cw[2:3, :] * z[halo:halo + tm] + cb)

    gate = conv(_dot(h, wg_ref[...]), cg_ref[...], bg_ref[...])
    val = conv(_dot(h, wv_ref[...]), cv_ref[...], bv_ref[...])
    a_ref[...] = (_gelu_tanh(gate) * val).astype(BF16)


def _ffn_up(x, g, scale, shift, w_up, w_conv, b_conv):
    tm, tn, halo = UP_TM, UP_TN, CONV_HALO
    tiles_per_seq = SEQ // tm
    nj = D_FF // tn
    halo_blocks = tm // halo
    return pl.pallas_call(
        _up_kernel,
        grid=(M_ROWS // tm, nj),
        in_specs=[
            pl.BlockSpec((tm, D_MODEL), lambda i, j: (i, 0)),
            pl.BlockSpec((halo, D_MODEL), lambda i, j: (jnp.maximum(i * halo_blocks - 1, 0), 0)),
            pl.BlockSpec((1, D_MODEL), lambda i, j: (0, 0)),
            pl.BlockSpec((1, 1, D_MODEL), lambda i, j: (i // tiles_per_seq, 0, 0)),
            pl.BlockSpec((1, 1, D_MODEL), lambda i, j: (i // tiles_per_seq, 0, 0)),
            pl.BlockSpec((D_MODEL, tn), lambda i, j: (0, j)),
            pl.BlockSpec((D_MODEL, tn), lambda i, j: (0, nj + j)),
            pl.BlockSpec((3, tn), lambda i, j: (0, j)),
            pl.BlockSpec((3, tn), lambda i, j: (0, nj + j)),
            pl.BlockSpec((1, tn), lambda i, j: (0, j)),
            pl.BlockSpec((1, tn), lambda i, j: (0, nj + j)),
        ],
        out_specs=pl.BlockSpec((tm, tn), lambda i, j: (i, j)),
        out_shape=jax.ShapeDtypeStruct((M_ROWS, D_FF), BF16),
        scratch_shapes=[pltpu.VMEM((tm + halo, D_MODEL), BF16)],
        compiler_params=_params("arbitrary", "arbitrary"),
        name="ffn_up",
    )(x, x, g, scale, shift, w_up, w_up, w_conv, w_conv, b_conv, b_conv)


def _down_kernel(a_ref, w_ref, x_ref, g_ref, gt_ref, o_ref, acc_ref):
    k = pl.program_id(1)

    @pl.when(k == 0)
    def _():
        acc_ref[...] = jnp.zeros_like(acc_ref)

    acc_ref[...] += _dot(a_ref[...], w_ref[...])

    @pl.when(k == pl.num_programs(1) - 1)
    def _():
        o_ref[...] = x_ref[...] + gt_ref[0] * _rms_norm(acc_ref[...], g_ref[...])


def _ffn_down(a, w_down, x, g, gate):
    tm, tk = DOWN_TM, DOWN_TK
    tiles_per_seq = SEQ // tm
    return pl.pallas_call(
        _down_kernel,
        grid=(M_ROWS // tm, D_FF // tk),
        in_specs=[
            pl.BlockSpec((tm, tk), lambda i, k: (i, k)),
            pl.BlockSpec((tk, D_MODEL), lambda i, k: (k, 0)),
            pl.BlockSpec((tm, D_MODEL), lambda i, k: (i, 0)),
            pl.BlockSpec((1, D_MODEL), lambda i, k: (0, 0)),
            pl.BlockSpec((1, 1, D_MODEL), lambda i, k: (i // tiles_per_seq, 0, 0)),
        ],
        out_specs=pl.BlockSpec((tm, D_MODEL), lambda i, k: (i, 0)),
        out_shape=jax.ShapeDtypeStruct((M_ROWS, D_MODEL), F32),
        scratch_shapes=[pltpu.VMEM((tm, D_MODEL), F32)],
        compiler_params=_params("arbitrary", "arbitrary"),
        name="ffn_down",
    )(a, w_down, x, g, gate)


def kernel(x, c, g_pre_mix, g_post_mix, g_pre_ffn, g_post_ffn, w_ada, b_ada, w_in, b_in,
           sgu_norm_g, sgu_w, sgu_b, pool_w, pool_scale, w_sgu_out, w_moba_out, w_pool_out,
           w_out, w_up, w_conv, b_conv, w_down):
    c_pad = jnp.pad(c, ((0, 8 - BATCH), (0, 0)))
    mod = _modulation(c_pad, w_ada, b_ada)[:, :BATCH]
    mod = mod.reshape(DEPTH, BATCH, N_MOD, 1, D_MODEL)

    xf = x.reshape(M_ROWS, D_MODEL)
    for l in range(DEPTH):
        sh1, sc1, gt1, sh2, sc2, gt2 = [mod[l, :, n] for n in range(N_MOD)]
        row = lambda v: v[l].reshape(1, -1)

        p, kmean = _inproj(xf, row(g_pre_mix), sc1, sh1, w_in[l].astype(BF16), row(b_in))
        kmean = kmean.reshape(BATCH, N_KV_BLOCKS, BRANCH_WIDTH)
        kmean_pad = jnp.pad(kmean, ((0, 0), (0, LANES - N_KV_BLOCKS), (0, 0)))

        sgu_bias_full = jnp.repeat(sgu_b[l].T, SGU_GROUP_DIM, axis=1)
        ysgu, ypool = _branches(p, row(sgu_norm_g), sgu_w[l], sgu_bias_full, pool_w[l],
                                row(pool_scale))
        ymoba = _attention(p, kmean_pad)
        xf = _merge(ysgu, ymoba, ypool, p, w_sgu_out[l].astype(BF16), w_moba_out[l].astype(BF16),
                    w_pool_out[l].astype(BF16), w_out[l].astype(BF16), xf, row(g_post_mix), gt1)

        a = _ffn_up(xf, row(g_pre_ffn), sc2, sh2, w_up[l].astype(BF16), w_conv[l], row(b_conv))
        xf = _ffn_down(a, w_down[l].astype(BF16), xf, row(g_post_ffn), gt2)
    return xf.reshape(BATCH, SEQ, D_MODEL)
```

```python
import functools

import jax
import jax.numpy as jnp
from jax import lax
from jax.experimental import pallas as pl
from jax.experimental.pallas import tpu as pltpu

F32 = jnp.float32
BF16 = jnp.bfloat16

D_MODEL = 2048
BATCH = 4
SEQ = 8192
DEPTH = 2
EPS = 1e-6
BRANCH_WIDTH = 512
SGU_GROUPS = 8
SGU_GROUP_DIM = 64
SGU_CHUNK = 128
MOBA_HEADS = 8
MOBA_HEAD_DIM = 64
MOBA_BLOCK = 256
MOBA_TOPK = 3
POOL_WINDOWS = (2, 4, 8, 16)
POOL_GROUPS = 4
POOL_GROUP_DIM = 128
IN_WIDTH = 6 * BRANCH_WIDTH + 3 * D_MODEL
D_FF = 5632
N_MOD = 6
M_ROWS = BATCH * SEQ
N_KV_BLOCKS = SEQ // MOBA_BLOCK

VMEM_LIMIT_BYTES = 56 * 1024 * 1024
LANES = 128
NEG = -1e30

MOD_TN = 1024
INPROJ_TM, INPROJ_TN = 1024, 1024
BRANCH_TR = 512
ATTN_TQ = MOBA_BLOCK
MERGE_TM = 512
UP_TM, UP_TN = 1024, 512
CONV_HALO = 16
DOWN_TM, DOWN_TK = 512, 1408
POOL_HALO = 16


def _gelu_tanh(x):
    c = 0.7978845608028654
    return 0.5 * x * (1.0 + jnp.tanh(c * (x + 0.044715 * (x * x * x))))


def _sigmoid(x):
    return 1.0 / (1.0 + jnp.exp(-x))


def _rms_norm(x, g):
    ms = jnp.mean(x * x, axis=-1, keepdims=True)
    return x * lax.rsqrt(ms + EPS) * g


def _dot(a, b):
    return jnp.dot(a, b, preferred_element_type=F32)


def _dot_nt(a, b):
    return lax.dot_general(a, b, (((1,), (1,)), ((), ())), preferred_element_type=F32)


def _params(*sem):
    return pltpu.CompilerParams(dimension_semantics=sem, vmem_limit_bytes=VMEM_LIMIT_BYTES)


def _mod_kernel(c_ref, w_ref, b_ref, o_ref):
    c = c_ref[...]
    cond = (c * _sigmoid(c)).astype(BF16)
    o_ref[0] = _dot(cond, w_ref[0].astype(BF16)) + b_ref[0]


def _modulation(c_pad, w_ada, b_ada):
    rows = c_pad.shape[0]
    n = N_MOD * D_MODEL
    return pl.pallas_call(
        _mod_kernel,
        grid=(DEPTH, n // MOD_TN),
        in_specs=[
            pl.BlockSpec((rows, D_MODEL), lambda l, j: (0, 0)),
            pl.BlockSpec((1, D_MODEL, MOD_TN), lambda l, j: (l, 0, j)),
            pl.BlockSpec((1, 1, MOD_TN), lambda l, j: (l, 0, j)),
        ],
        out_specs=pl.BlockSpec((1, rows, MOD_TN), lambda l, j: (l, 0, j)),
        out_shape=jax.ShapeDtypeStruct((DEPTH, rows, n), F32),
        compiler_params=_params("arbitrary", "arbitrary"),
        name="adaln_mod",
    )(c_pad, w_ada, b_ada.reshape(DEPTH, 1, n))


def _inproj_kernel(x_ref, g_ref, sc_ref, sh_ref, w_ref, b_ref, p_ref, kmean_ref, vt_ref, h_ref):
    j = pl.program_id(1)

    @pl.when(j == 0)
    def _():
        h = _rms_norm(x_ref[...], g_ref[...]) * (1.0 + sc_ref[0]) + sh_ref[0]
        h_ref[...] = h.astype(BF16)

    acc = _dot(h_ref[...], w_ref[...]) + b_ref[...]

    @pl.when(j == 0)
    def _():
        p_ref[...] = _gelu_tanh(acc).astype(BF16)

    @pl.when(j == 1)
    def _():
        p_ref[...] = acc.astype(BF16)
        k = acc[:, BRANCH_WIDTH:]
        for r in range(INPROJ_TM // MOBA_BLOCK):
            kmean_ref[0, r:r + 1, :] = jnp.mean(
                k[r * MOBA_BLOCK:(r + 1) * MOBA_BLOCK], axis=0, keepdims=True)

    @pl.when(j == 2)
    def _():
        p_ref[...] = acc.astype(BF16)
        for r in range(INPROJ_TM // MOBA_BLOCK):
            vt_ref[r] = jnp.transpose(
                acc[r * MOBA_BLOCK:(r + 1) * MOBA_BLOCK, :BRANCH_WIDTH]).astype(BF16)

    @pl.when(j >= 3)
    def _():
        p_ref[...] = _sigmoid(acc).astype(BF16)


def _inproj(x, g, scale, shift, w, b):
    tiles_per_seq = SEQ // INPROJ_TM
    blocks_per_tile = INPROJ_TM // MOBA_BLOCK
    return pl.pallas_call(
        _inproj_kernel,
        grid=(M_ROWS // INPROJ_TM, IN_WIDTH // INPROJ_TN),
        in_specs=[
            pl.BlockSpec((INPROJ_TM, D_MODEL), lambda i, j: (i, 0)),
            pl.BlockSpec((1, D_MODEL), lambda i, j: (0, 0)),
            pl.BlockSpec((1, 1, D_MODEL), lambda i, j: (i // tiles_per_seq, 0, 0)),
            pl.BlockSpec((1, 1, D_MODEL), lambda i, j: (i // tiles_per_seq, 0, 0)),
            pl.BlockSpec((D_MODEL, INPROJ_TN), lambda i, j: (0, j)),
            pl.BlockSpec((1, INPROJ_TN), lambda i, j: (0, j)),
        ],
        out_specs=[
            pl.BlockSpec((INPROJ_TM, INPROJ_TN), lambda i, j: (i, j)),
            pl.BlockSpec((1, blocks_per_tile, BRANCH_WIDTH), lambda i, j: (i, 0, 0)),
            pl.BlockSpec((blocks_per_tile, BRANCH_WIDTH, MOBA_BLOCK), lambda i, j: (i, 0, 0)),
        ],
        out_shape=[
            jax.ShapeDtypeStruct((M_ROWS, IN_WIDTH), BF16),
            jax.ShapeDtypeStruct((M_ROWS // INPROJ_TM, blocks_per_tile, BRANCH_WIDTH), F32),
            jax.ShapeDtypeStruct((M_ROWS // MOBA_BLOCK, BRANCH_WIDTH, MOBA_BLOCK), BF16),
        ],
        scratch_shapes=[pltpu.VMEM((INPROJ_TM, D_MODEL), BF16)],
        compiler_params=_params("arbitrary", "arbitrary"),
        name="inproj",
    )(x, g, scale, shift, w, b)


def _branch_kernel(u_ref, v_ref, z_ref, zprev_ref, ng_ref, sw_ref, sb_ref, pw_ref, ps_ref,
                   ysgu_ref, ypool_ref):
    i = pl.program_id(0)
    tr = BRANCH_TR
    first = (i % (SEQ // tr)) == 0

    v = v_ref[...].astype(F32)
    mu = jnp.mean(v, axis=-1, keepdims=True)
    vc = v - mu
    var = jnp.mean(vc * vc, axis=-1, keepdims=True)
    vn = (vc * lax.rsqrt(var + EPS) * ng_ref[...]).astype(BF16)

    row = lax.broadcasted_iota(jnp.int32, (SGU_CHUNK, SGU_CHUNK), 0)
    col = lax.broadcasted_iota(jnp.int32, (SGU_CHUNK, SGU_CHUNK), 1)
    causal = row >= col
    wm = [jnp.where(causal, sw_ref[g], 0.0).astype(BF16) for g in range(SGU_GROUPS)]
    lane = lax.broadcasted_iota(jnp.int32, (SGU_CHUNK, LANES), 1)
    low_half = lane < SGU_GROUP_DIM
    zero = jnp.zeros((SGU_CHUNK, LANES), BF16)

    for c in range(tr // SGU_CHUNK):
        rows = slice(c * SGU_CHUNK, (c + 1) * SGU_CHUNK)
        for pr in range(SGU_GROUPS // 2):
            cols = slice(pr * LANES, (pr + 1) * LANES)
            blk = vn[rows, cols]
            mixed = (_dot(wm[2 * pr], jnp.where(low_half, blk, zero))
                     + _dot(wm[2 * pr + 1], jnp.where(low_half, zero, blk))
                     + sb_ref[:, cols])
            ysgu_ref[rows, cols] = (u_ref[rows, cols].astype(F32) * mixed).astype(BF16)

    z = z_ref[...].astype(F32)
    zprev = jnp.where(first, 0.0, zprev_ref[...].astype(F32))
    pos = (i % (SEQ // tr)) * tr + lax.broadcasted_iota(jnp.int32, (tr, POOL_GROUP_DIM), 0)
    for gi, w in enumerate(POOL_WINDOWS):
        cols = slice(gi * POOL_GROUP_DIM, (gi + 1) * POOL_GROUP_DIM)
        zg = z[:, cols]
        cur = jnp.concatenate([zprev[:, cols], zg], axis=0)
        d = 1
        while d < w:
            cur = cur[d:] + cur[:-d]
            d *= 2
        win_sum = cur[cur.shape[0] - tr:]
        count = jnp.minimum(pos + 1, w).astype(F32)
        pooled = win_sum / count - zg
        y = _dot(pooled.astype(BF16), pw_ref[gi].astype(BF16)) * ps_ref[:, cols]
        ypool_ref[:, cols] = y.astype(BF16)


def _branches(p, sgu_norm_g, sgu_w, sgu_bias_full, pool_w, pool_scale):
    tr = BRANCH_TR
    halo_blocks = tr // POOL_HALO
    return pl.pallas_call(
        _branch_kernel,
        grid=(M_ROWS // tr,),
        in_specs=[
            pl.BlockSpec((tr, BRANCH_WIDTH), lambda i: (i, 0)),
            pl.BlockSpec((tr, BRANCH_WIDTH), lambda i: (i, 1)),
            pl.BlockSpec((tr, BRANCH_WIDTH), lambda i: (i, 5)),
            pl.BlockSpec((POOL_HALO, BRANCH_WIDTH),
                         lambda i: (jnp.maximum(i * halo_blocks - 1, 0), 5)),
            pl.BlockSpec((1, BRANCH_WIDTH), lambda i: (0, 0)),
            pl.BlockSpec((SGU_GROUPS, SGU_CHUNK, SGU_CHUNK), lambda i: (0, 0, 0)),
            pl.BlockSpec((SGU_CHUNK, BRANCH_WIDTH), lambda i: (0, 0)),
            pl.BlockSpec((POOL_GROUPS, POOL_GROUP_DIM, POOL_GROUP_DIM), lambda i: (0, 0, 0)),
            pl.BlockSpec((1, BRANCH_WIDTH), lambda i: (0, 0)),
        ],
        out_specs=[
            pl.BlockSpec((tr, BRANCH_WIDTH), lambda i: (i, 0)),
            pl.BlockSpec((tr, BRANCH_WIDTH), lambda i: (i, 0)),
        ],
        out_shape=[
            jax.ShapeDtypeStruct((M_ROWS, BRANCH_WIDTH), BF16),
            jax.ShapeDtypeStruct((M_ROWS, BRANCH_WIDTH), BF16),
        ],
        compiler_params=_params("arbitrary"),
        name="sgu_pool",
    )(p, p, p, p, sgu_norm_g, sgu_w, sgu_bias_full, pool_w, pool_scale)


def _attn_kernel(q_ref, k_ref, vt_ref, km_ref, o_ref,
                 qs_ref, bias_ref, pick_ref, m_ref, l_ref, acc_ref, s_ref):
    own = pl.program_id(1)
    tq, blk, hd = ATTN_TQ, MOBA_BLOCK, MOBA_HEAD_DIM
    own_f = own.astype(F32)

    rel = (lax.broadcasted_iota(jnp.int32, (blk, tq), 0)
           - lax.broadcasted_iota(jnp.int32, (blk, tq), 1))
    rel_f = rel.astype(F32)
    causal = jnp.where(rel > 0, NEG, 0.0)
    blk_id = lax.broadcasted_iota(jnp.int32, (N_KV_BLOCKS, tq), 0)
    blk_id_f = blk_id.astype(F32)
    past = blk_id < own
    head_rows = lax.broadcasted_iota(jnp.int32, (LANES, tq), 0) < hd

    q_t = jnp.transpose(q_ref[...].astype(F32))
    k_own = pl.multiple_of(own * blk, blk)

    for pr in range(MOBA_HEADS // 2):
        cols = slice(pr * LANES, (pr + 1) * LANES)
        q_pair = q_t[pr * LANES:(pr + 1) * LANES].astype(BF16)
        km = km_ref[0, :, cols]
        km_hi = km.astype(BF16)
        km_lo = (km - km_hi.astype(F32)).astype(BF16)
        for hh in range(2):
            h = 2 * pr + hh
            slope = 2.0 ** -(h + 1)
            qh = jnp.where(head_rows if hh == 0 else ~head_rows, q_pair, jnp.zeros_like(q_pair))

            sc = _dot(km_hi, qh) + _dot(km_lo, qh)
            sc = jnp.where(past, sc, -jnp.inf)
            for r in range(MOBA_TOPK):
                best = jnp.max(sc, axis=0, keepdims=True)
                idx = jnp.min(jnp.where(sc == best, blk_id_f, 1e9), axis=0, keepdims=True)
                pick_ref[h, r] = jnp.where(best > -jnp.inf, idx, -1.0)
                sc = jnp.where(blk_id_f == idx, -jnp.inf, sc)

            qs = qh * (hd ** -0.5)
            qs_ref[h] = qs
            bias = slope * rel_f
            bias_ref[h] = bias

            s = _dot(k_ref[pl.ds(k_own, blk), cols], qs) + bias + causal
            m = jnp.max(s, axis=0, keepdims=True)
            p = jnp.exp(s - m)
            m_ref[h] = m
            l_ref[h] = jnp.sum(p, axis=0, keepdims=True)
            acc_ref[h] = _dot(vt_ref[own, h * hd:(h + 1) * hd, :], p.astype(BF16))

    def scores(j):
        start = pl.multiple_of(j * blk, blk)
        for pr in range(MOBA_HEADS // 2):
            kj = k_ref[pl.ds(start, blk), pr * LANES:(pr + 1) * LANES]
            for hh in range(2):
                h = 2 * pr + hh
                s_ref[h] = _dot(kj, qs_ref[h]) + bias_ref[h]

    scores(0)

    def body(j, carry):
        jf = j.astype(F32)
        for h in range(MOBA_HEADS):
            slope = 2.0 ** -(h + 1)
            chosen = (pick_ref[h, 0] == jf) | (pick_ref[h, 1] == jf) | (pick_ref[h, 2] == jf)
            shift = jnp.where(chosen, (-slope * blk) * (own_f - jf), NEG)
            m = m_ref[h]
            m_new = jnp.maximum(m, jnp.max(s_ref[h], axis=0, keepdims=True) + shift)
            p = jnp.exp(s_ref[h] - (m_new - shift))
            alpha = jnp.exp(m - m_new)
            m_ref[h] = m_new
            l_ref[h] = alpha * l_ref[h] + jnp.sum(p, axis=0, keepdims=True)
            acc_ref[h] = alpha * acc_ref[h] + _dot(vt_ref[j, h * hd:(h + 1) * hd, :],
                                                   p.astype(BF16))
        scores(j + 1)
        return carry

    lax.fori_loop(0, own, body, 0)

    o_t = jnp.concatenate([acc_ref[h] / l_ref[h] for h in range(MOBA_HEADS)], axis=0)
    o_ref[...] = jnp.transpose(o_t).astype(BF16)


def _attention(p, kmean, v_t):
    tq = ATTN_TQ
    q_tiles = SEQ // tq
    resident = functools.partial(pl.BlockSpec, pipeline_mode=pl.Buffered(1))
    return pl.pallas_call(
        _attn_kernel,
        grid=(BATCH, q_tiles),
        in_specs=[
            pl.BlockSpec((tq, BRANCH_WIDTH), lambda b, i: (b * q_tiles + i, 2)),
            resident((SEQ, BRANCH_WIDTH), lambda b, i: (b, 3)),
            resident((N_KV_BLOCKS, BRANCH_WIDTH, MOBA_BLOCK), lambda b, i: (b, 0, 0)),
            pl.BlockSpec((1, N_KV_BLOCKS, BRANCH_WIDTH), lambda b, i: (b, 0, 0)),
        ],
        out_specs=pl.BlockSpec((tq, BRANCH_WIDTH), lambda b, i: (b * q_tiles + i, 0)),
        out_shape=jax.ShapeDtypeStruct((M_ROWS, BRANCH_WIDTH), BF16),
        scratch_shapes=[
            pltpu.VMEM((MOBA_HEADS, LANES, tq), BF16),
            pltpu.VMEM((MOBA_HEADS, MOBA_BLOCK, tq), F32),
            pltpu.VMEM((MOBA_HEADS, MOBA_TOPK, 1, tq), F32),
            pltpu.VMEM((MOBA_HEADS, 1, tq), F32),
            pltpu.VMEM((MOBA_HEADS, 1, tq), F32),
            pltpu.VMEM((MOBA_HEADS, MOBA_HEAD_DIM, tq), F32),
            pltpu.VMEM((MOBA_HEADS, MOBA_BLOCK, tq), F32),
        ],
        compiler_params=_params("arbitrary", "arbitrary"),
        name="moba_attention",
    )(p, p, v_t, kmean)


def _merge_kernel(ys_ref, ym_ref, yp_ref, gs0, gs1, gm0, gm1, gp0, gp1,
                  ws_ref, wm_ref, wp_ref, wo_ref, x_ref, g_ref, gt_ref, o_ref):
    ys, ym, yp = ys_ref[...], ym_ref[...], yp_ref[...]
    halves = []
    half_w = D_MODEL // 2
    for hf, (gs, gm, gp) in enumerate(((gs0, gm0, gp0), (gs1, gm1, gp1))):
        cols = slice(hf * half_w, (hf + 1) * half_w)
        merged = (gs[...].astype(F32) * _dot(ys, ws_ref[:, cols])
                  + gm[...].astype(F32) * _dot(ym, wm_ref[:, cols])
                  + gp[...].astype(F32) * _dot(yp, wp_ref[:, cols]))
        halves.append(merged.astype(BF16))
    merged = jnp.concatenate(halves, axis=1)
    y = _dot(merged, wo_ref[...])
    o_ref[...] = x_ref[...] + gt_ref[0] * _rms_norm(y, g_ref[...])


def _merge(ysgu, ymoba, ypool, p, w_sgu_out, w_moba_out, w_pool_out, w_out, x, g, gate):
    tm = MERGE_TM
    tiles_per_seq = SEQ // tm
    half_w = D_MODEL // 2
    gate_col0 = 6 * BRANCH_WIDTH // half_w
    resident = functools.partial(pl.BlockSpec, pipeline_mode=pl.Buffered(1))
    branch = pl.BlockSpec((tm, BRANCH_WIDTH), lambda i: (i, 0))
    gate_specs = [pl.BlockSpec((tm, half_w), functools.partial(lambda i, c: (i, c), c=gate_col0 + c))
                  for c in range(6)]
    w_branch = resident((BRANCH_WIDTH, D_MODEL), lambda i: (0, 0))
    return pl.pallas_call(
        _merge_kernel,
        grid=(M_ROWS // tm,),
        in_specs=[branch, branch, branch, *gate_specs, w_branch, w_branch, w_branch,
                  resident((D_MODEL, D_MODEL), lambda i: (0, 0)),
                  pl.BlockSpec((tm, D_MODEL), lambda i: (i, 0)),
                  pl.BlockSpec((1, D_MODEL), lambda i: (0, 0)),
                  pl.BlockSpec((1, 1, D_MODEL), lambda i: (i // tiles_per_seq, 0, 0))],
        out_specs=pl.BlockSpec((tm, D_MODEL), lambda i: (i, 0)),
        out_shape=jax.ShapeDtypeStruct((M_ROWS, D_MODEL), F32),
        compiler_params=_params("arbitrary"),
        name="merge_outproj",
    )(ysgu, ymoba, ypool, p, p, p, p, p, p, w_sgu_out, w_moba_out, w_pool_out, w_out, x, g, gate)


def _up_kernel(x_ref, xh_ref, g_ref, sc_ref, sh_ref, wg_ref, wv_ref, cg_ref, cv_ref, bg_ref, bv_ref,
               a_ref, h_ref):
    i = pl.program_id(0)
    j = pl.program_id(1)
    tm, halo = UP_TM, CONV_HALO

    @pl.when(j == 0)
    def _():
        scale = 1.0 + sc_ref[0]
        shift = sh_ref[0]
        h_ref[halo:, :] = (_rms_norm(x_ref[...], g_ref[...]) * scale + shift).astype(BF16)
        first = (i % (SEQ // tm)) == 0
        hh = _rms_norm(xh_ref[...], g_ref[...]) * scale + shift
        h_ref[:halo, :] = jnp.where(first, 0.0, hh).astype(BF16)

    h = h_ref[...]

    def conv(z, cw, cb):
        return (cw[0:1, :] * z[halo - 2:halo - 2 + tm] + cw[1:2, :] * z[halo - 1:halo - 1 + tm]
                + cw[2:3, :] * z[halo:halo + tm] + cb)

    gate = conv(_dot(h, wg_ref[...]), cg_ref[...], bg_ref[...])
    val = conv(_dot(h, wv_ref[...]), cv_ref[...], bv_ref[...])
    a_ref[...] = (_gelu_tanh(gate) * val).astype(BF16)


def _ffn_up(x, g, scale, shift, w_up, w_conv, b_conv):
    tm, tn, halo = UP_TM, UP_TN, CONV_HALO
    tiles_per_seq = SEQ // tm
    nj = D_FF // tn
    halo_blocks = tm // halo
    return pl.pallas_call(
        _up_kernel,
        grid=(M_ROWS // tm, nj),
        in_specs=[
            pl.BlockSpec((tm, D_MODEL), lambda i, j: (i, 0)),
            pl.BlockSpec((halo, D_MODEL), lambda i, j: (jnp.maximum(i * halo_blocks - 1, 0), 0)),
            pl.BlockSpec((1, D_MODEL), lambda i, j: (0, 0)),
            pl.BlockSpec((1, 1, D_MODEL), lambda i, j: (i // tiles_per_seq, 0, 0)),
            pl.BlockSpec((1, 1, D_MODEL), lambda i, j: (i // tiles_per_seq, 0, 0)),
            pl.BlockSpec((D_MODEL, tn), lambda i, j: (0, j)),
            pl.BlockSpec((D_MODEL, tn), lambda i, j: (0, nj + j)),
            pl.BlockSpec((3, tn), lambda i, j: (0, j)),
            pl.BlockSpec((3, tn), lambda i, j: (0, nj + j)),
            pl.BlockSpec((1, tn), lambda i, j: (0, j)),
            pl.BlockSpec((1, tn), lambda i, j: (0, nj + j)),
        ],
        out_specs=pl.BlockSpec((tm, tn), lambda i, j: (i, j)),
        out_shape=jax.ShapeDtypeStruct((M_ROWS, D_FF), BF16),
        scratch_shapes=[pltpu.VMEM((tm + halo, D_MODEL), BF16)],
        compiler_params=_params("arbitrary", "arbitrary"),
        name="ffn_up",
    )(x, x, g, scale, shift, w_up, w_up, w_conv, w_conv, b_conv, b_conv)


def _down_kernel(a_ref, w_ref, x_ref, g_ref, gt_ref, o_ref, acc_ref):
    k = pl.program_id(1)

    @pl.when(k == 0)
    def _():
        acc_ref[...] = jnp.zeros_like(acc_ref)

    acc_ref[...] += _dot(a_ref[...], w_ref[...])

    @pl.when(k == pl.num_programs(1) - 1)
    def _():
        o_ref[...] = x_ref[...] + gt_ref[0] * _rms_norm(acc_ref[...], g_ref[...])


def _ffn_down(a, w_down, x, g, gate):
    tm, tk = DOWN_TM, DOWN_TK
    tiles_per_seq = SEQ // tm
    return pl.pallas_call(
        _down_kernel,
        grid=(M_ROWS // tm, D_FF // tk),
        in_specs=[
            pl.BlockSpec((tm, tk), lambda i, k: (i, k)),
            pl.BlockSpec((tk, D_MODEL), lambda i, k: (k, 0)),
            pl.BlockSpec((tm, D_MODEL), lambda i, k: (i, 0)),
            pl.BlockSpec((1, D_MODEL), lambda i, k: (0, 0)),
            pl.BlockSpec((1, 1, D_MODEL), lambda i, k: (i // tiles_per_seq, 0, 0)),
        ],
        out_specs=pl.BlockSpec((tm, D_MODEL), lambda i, k: (i, 0)),
        out_shape=jax.ShapeDtypeStruct((M_ROWS, D_MODEL), F32),
        scratch_shapes=[pltpu.VMEM((tm, D_MODEL), F32)],
        compiler_params=_params("arbitrary", "arbitrary"),
        name="ffn_down",
    )(a, w_down, x, g, gate)


def kernel(x, c, g_pre_mix, g_post_mix, g_pre_ffn, g_post_ffn, w_ada, b_ada, w_in, b_in,
           sgu_norm_g, sgu_w, sgu_b, pool_w, pool_scale, w_sgu_out, w_moba_out, w_pool_out,
           w_out, w_up, w_conv, b_conv, w_down):
    c_pad = jnp.pad(c, ((0, 8 - BATCH), (0, 0)))
    mod = _modulation(c_pad, w_ada, b_ada)[:, :BATCH]
    mod = mod.reshape(DEPTH, BATCH, N_MOD, 1, D_MODEL)

    xf = x.reshape(M_ROWS, D_MODEL)
    for l in range(DEPTH):
        sh1, sc1, gt1, sh2, sc2, gt2 = [mod[l, :, n] for n in range(N_MOD)]
        row = lambda v: v[l].reshape(1, -1)

        p, kmean, v_t = _inproj(xf, row(g_pre_mix), sc1, sh1, w_in[l].astype(BF16), row(b_in))
        kmean = kmean.reshape(BATCH, N_KV_BLOCKS, BRANCH_WIDTH)

        sgu_bias_full = jnp.repeat(sgu_b[l].T, SGU_GROUP_DIM, axis=1)
        ysgu, ypool = _branches(p, row(sgu_norm_g), sgu_w[l], sgu_bias_full, pool_w[l],
                                row(pool_scale))
        ymoba = _attention(p, kmean, v_t)
        xf = _merge(ysgu, ymoba, ypool, p, w_sgu_out[l].astype(BF16), w_moba_out[l].astype(BF16),
                    w_pool_out[l].astype(BF16), w_out[l].astype(BF16), xf, row(g_post_mix), gt1)

        a = _ffn_up(xf, row(g_pre_ffn), sc2, sh2, w_up[l].astype(BF16), w_conv[l], row(b_conv))
        xf = _ffn_down(a, w_down[l].astype(BF16), xf, row(g_post_ffn), gt2)
    return xf.reshape(BATCH, SEQ, D_MODEL)
```

```python
import functools

import jax
import jax.numpy as jnp
from jax import lax
from jax.experimental import pallas as pl
from jax.experimental.pallas import tpu as pltpu

F32 = jnp.float32
BF16 = jnp.bfloat16

D_MODEL = 2048
BATCH = 4
SEQ = 8192
DEPTH = 2
EPS = 1e-6
BRANCH_WIDTH = 512
SGU_GROUPS = 8
SGU_GROUP_DIM = 64
SGU_CHUNK = 128
MOBA_HEADS = 8
MOBA_HEAD_DIM = 64
MOBA_BLOCK = 256
MOBA_TOPK = 3
POOL_WINDOWS = (2, 4, 8, 16)
POOL_GROUPS = 4
POOL_GROUP_DIM = 128
IN_WIDTH = 6 * BRANCH_WIDTH + 3 * D_MODEL
D_FF = 5632
N_MOD = 6
M_ROWS = BATCH * SEQ
N_KV_BLOCKS = SEQ // MOBA_BLOCK

VMEM_LIMIT_BYTES = 56 * 1024 * 1024
LANES = 128
NEG = -1e30
LOG2E = 1.4426950408889634
ONES_ROWS = 16

MOD_TN = 1024
PRENORM_TM = 512
QKV_TM = 1024
UVG_TM, UVG_TN = 1024, 1024
BRANCH_TR = 512
ATTN_TQ = MOBA_BLOCK
MERGE_TM = 512
UP_TM, UP_TN = 1024, 512
CONV_HALO = 16
DOWN_TM, DOWN_TK = 512, 1408
POOL_HALO = 16


GELU_C = 0.7978845608028654
GELU_A = 0.044715


def _sigmoid(x):
    return 1.0 / (1.0 + jnp.exp(-x))


def _gelu_arg(x):
    return (2.0 * GELU_C) * (x + GELU_A * (x * x * x))


def _gelu_tanh(x):
    return x * _sigmoid(_gelu_arg(x))


def _rms_norm(x, g):
    ms = jnp.mean(x * x, axis=-1, keepdims=True)
    return x * lax.rsqrt(ms + EPS) * g


def _modulated_norm(x, g, scale, shift):
    return (_rms_norm(x, g) * (1.0 + scale) + shift).astype(BF16)


def _dot(a, b):
    return jnp.dot(a, b, preferred_element_type=F32)


def _dot_nt(a, b):
    return lax.dot_general(a, b, (((1,), (1,)), ((), ())), preferred_element_type=F32)


def _params(*sem):
    return pltpu.CompilerParams(dimension_semantics=sem, vmem_limit_bytes=VMEM_LIMIT_BYTES)


def _mod_kernel(c_ref, w_ref, b_ref, o_ref):
    c = c_ref[...]
    cond = (c * _sigmoid(c)).astype(BF16)
    o_ref[0] = _dot(cond, w_ref[0].astype(BF16)) + b_ref[0]


def _modulation(c_pad, w_ada, b_ada):
    rows = c_pad.shape[0]
    n = N_MOD * D_MODEL
    return pl.pallas_call(
        _mod_kernel,
        grid=(DEPTH, n // MOD_TN),
        in_specs=[
            pl.BlockSpec((rows, D_MODEL), lambda l, j: (0, 0)),
            pl.BlockSpec((1, D_MODEL, MOD_TN), lambda l, j: (l, 0, j)),
            pl.BlockSpec((1, 1, MOD_TN), lambda l, j: (l, 0, j)),
        ],
        out_specs=pl.BlockSpec((1, rows, MOD_TN), lambda l, j: (l, 0, j)),
        out_shape=jax.ShapeDtypeStruct((DEPTH, rows, n), F32),
        compiler_params=_params("arbitrary", "arbitrary"),
        name="adaln_mod",
    )(c_pad, w_ada, b_ada.reshape(DEPTH, 1, n))


def _prenorm_kernel(x_ref, g_ref, sc_ref, sh_ref, h_ref):
    h_ref[...] = _modulated_norm(x_ref[...], g_ref[...], sc_ref[0], sh_ref[0])


def _prenorm(x, g, scale, shift):
    tm = PRENORM_TM
    tiles_per_seq = SEQ // tm
    return pl.pallas_call(
        _prenorm_kernel,
        grid=(M_ROWS // tm,),
        in_specs=[
            pl.BlockSpec((tm, D_MODEL), lambda i: (i, 0)),
            pl.BlockSpec((1, D_MODEL), lambda i: (0, 0)),
            pl.BlockSpec((1, 1, D_MODEL), lambda i: (i // tiles_per_seq, 0, 0)),
            pl.BlockSpec((1, 1, D_MODEL), lambda i: (i // tiles_per_seq, 0, 0)),
        ],
        out_specs=pl.BlockSpec((tm, D_MODEL), lambda i: (i, 0)),
        out_shape=jax.ShapeDtypeStruct((M_ROWS, D_MODEL), BF16),
        compiler_params=_params("arbitrary"),
        name="prenorm",
    )(x, g, scale, shift)


def _qkv_kernel(h_ref, w_ref, b_ref, o_ref, kmean_ref, vt_ref):
    acc = _dot(h_ref[...], w_ref[...]) + b_ref[...]
    o_ref[...] = acc.astype(BF16)
    for r in range(QKV_TM // MOBA_BLOCK):
        rows = slice(r * MOBA_BLOCK, (r + 1) * MOBA_BLOCK)
        kmean_ref[0, r:r + 1, :] = jnp.mean(acc[rows, BRANCH_WIDTH:2 * BRANCH_WIDTH],
                                            axis=0, keepdims=True)
        vt_ref[r] = jnp.transpose(acc[rows, 2 * BRANCH_WIDTH:3 * BRANCH_WIDTH]).astype(BF16)


def _qkv_proj(h, w, b):
    tm = QKV_TM
    n = 4 * BRANCH_WIDTH
    blocks_per_tile = tm // MOBA_BLOCK
    return pl.pallas_call(
        _qkv_kernel,
        grid=(M_ROWS // tm,),
        in_specs=[
            pl.BlockSpec((tm, D_MODEL), lambda i: (i, 0)),
            pl.BlockSpec((D_MODEL, n), lambda i: (0, 0), pipeline_mode=pl.Buffered(1)),
            pl.BlockSpec((1, n), lambda i: (0, 0)),
        ],
        out_specs=[
            pl.BlockSpec((tm, n), lambda i: (i, 0)),
            pl.BlockSpec((1, blocks_per_tile, BRANCH_WIDTH), lambda i: (i, 0, 0)),
            pl.BlockSpec((blocks_per_tile, BRANCH_WIDTH, MOBA_BLOCK), lambda i: (i, 0, 0)),
        ],
        out_shape=[
            jax.ShapeDtypeStruct((M_ROWS, n), BF16),
            jax.ShapeDtypeStruct((M_ROWS // tm, blocks_per_tile, BRANCH_WIDTH), F32),
            jax.ShapeDtypeStruct((M_ROWS // MOBA_BLOCK, BRANCH_WIDTH, MOBA_BLOCK), BF16),
        ],
        compiler_params=_params("arbitrary"),
        name="qkv_proj",
    )(h, w, b)


def _uvg_kernel(h_ref, w_ref, b_ref, o_ref):
    is_gelu = pl.program_id(1) == 0
    acc = _dot(h_ref[...], w_ref[...]) + b_ref[...]
    a_arg = jnp.where(is_gelu, acc, 1.0)
    b_arg = jnp.where(is_gelu, _gelu_arg(acc), acc)
    o_ref[...] = (a_arg * _sigmoid(b_arg)).astype(BF16)


def _uvg_proj(h, w, b):
    tm, tn = UVG_TM, UVG_TN
    n = 2 * BRANCH_WIDTH + 3 * D_MODEL
    return pl.pallas_call(
        _uvg_kernel,
        grid=(M_ROWS // tm, n // tn),
        in_specs=[
            pl.BlockSpec((tm, D_MODEL), lambda i, j: (i, 0)),
            pl.BlockSpec((D_MODEL, tn), lambda i, j: (0, j)),
            pl.BlockSpec((1, tn), lambda i, j: (0, j)),
        ],
        out_specs=pl.BlockSpec((tm, tn), lambda i, j: (i, j)),
        out_shape=jax.ShapeDtypeStruct((M_ROWS, n), BF16),
        compiler_params=_params("arbitrary", "arbitrary"),
        name="uvg_proj",
    )(h, w, b)


def _branch_kernel(u_ref, v_ref, z_ref, zprev_ref, ng_ref, sw_ref, sb_ref, pw_ref, ps_ref,
                   ysgu_ref, ypool_ref):
    i = pl.program_id(0)
    tr = BRANCH_TR
    first = (i % (SEQ // tr)) == 0

    v = v_ref[...].astype(F32)
    mu = jnp.mean(v, axis=-1, keepdims=True)
    vc = v - mu
    var = jnp.mean(vc * vc, axis=-1, keepdims=True)
    vn = (vc * lax.rsqrt(var + EPS) * ng_ref[...]).astype(BF16)

    row = lax.broadcasted_iota(jnp.int32, (SGU_CHUNK, SGU_CHUNK), 0)
    col = lax.broadcasted_iota(jnp.int32, (SGU_CHUNK, SGU_CHUNK), 1)
    causal = row >= col
    wm = [jnp.where(causal, sw_ref[g], 0.0).astype(BF16) for g in range(SGU_GROUPS)]
    lane = lax.broadcasted_iota(jnp.int32, (SGU_CHUNK, LANES), 1)
    low_half = lane < SGU_GROUP_DIM
    zero = jnp.zeros((SGU_CHUNK, LANES), BF16)

    for c in range(tr // SGU_CHUNK):
        rows = slice(c * SGU_CHUNK, (c + 1) * SGU_CHUNK)
        for pr in range(SGU_GROUPS // 2):
            cols = slice(pr * LANES, (pr + 1) * LANES)
            blk = vn[rows, cols]
            mixed = (_dot(wm[2 * pr], jnp.where(low_half, blk, zero))
                     + _dot(wm[2 * pr + 1], jnp.where(low_half, zero, blk))
                     + sb_ref[:, cols])
            ysgu_ref[rows, cols] = (u_ref[rows, cols].astype(F32) * mixed).astype(BF16)

    z = z_ref[...].astype(F32)
    zprev = jnp.where(first, 0.0, zprev_ref[...].astype(F32))
    pos = (i % (SEQ // tr)) * tr + lax.broadcasted_iota(jnp.int32, (tr, POOL_GROUP_DIM), 0)
    for gi, w in enumerate(POOL_WINDOWS):
        cols = slice(gi * POOL_GROUP_DIM, (gi + 1) * POOL_GROUP_DIM)
        zg = z[:, cols]
        cur = jnp.concatenate([zprev[:, cols], zg], axis=0)
        d = 1
        while d < w:
            cur = cur[d:] + cur[:-d]
            d *= 2
        win_sum = cur[cur.shape[0] - tr:]
        count = jnp.minimum(pos + 1, w).astype(F32)
        pooled = win_sum / count - zg
        y = _dot(pooled.astype(BF16), pw_ref[gi].astype(BF16)) * ps_ref[:, cols]
        ypool_ref[:, cols] = y.astype(BF16)


def _branches(uvg, qkvz, sgu_norm_g, sgu_w, sgu_bias_full, pool_w, pool_scale):
    tr = BRANCH_TR
    halo_blocks = tr // POOL_HALO
    return pl.pallas_call(
        _branch_kernel,
        grid=(M_ROWS // tr,),
        in_specs=[
            pl.BlockSpec((tr, BRANCH_WIDTH), lambda i: (i, 0)),
            pl.BlockSpec((tr, BRANCH_WIDTH), lambda i: (i, 1)),
            pl.BlockSpec((tr, BRANCH_WIDTH), lambda i: (i, 3)),
            pl.BlockSpec((POOL_HALO, BRANCH_WIDTH),
                         lambda i: (jnp.maximum(i * halo_blocks - 1, 0), 3)),
            pl.BlockSpec((1, BRANCH_WIDTH), lambda i: (0, 0)),
            pl.BlockSpec((SGU_GROUPS, SGU_CHUNK, SGU_CHUNK), lambda i: (0, 0, 0)),
            pl.BlockSpec((SGU_CHUNK, BRANCH_WIDTH), lambda i: (0, 0)),
            pl.BlockSpec((POOL_GROUPS, POOL_GROUP_DIM, POOL_GROUP_DIM), lambda i: (0, 0, 0)),
            pl.BlockSpec((1, BRANCH_WIDTH), lambda i: (0, 0)),
        ],
        out_specs=[
            pl.BlockSpec((tr, BRANCH_WIDTH), lambda i: (i, 0)),
            pl.BlockSpec((tr, BRANCH_WIDTH), lambda i: (i, 0)),
        ],
        out_shape=[
            jax.ShapeDtypeStruct((M_ROWS, BRANCH_WIDTH), BF16),
            jax.ShapeDtypeStruct((M_ROWS, BRANCH_WIDTH), BF16),
        ],
        compiler_params=_params("arbitrary"),
        name="sgu_pool",
    )(uvg, uvg, qkvz, qkvz, sgu_norm_g, sgu_w, sgu_bias_full, pool_w, pool_scale)


def _attn_kernel(q_ref, k_ref, vt_ref, km_ref, o_ref,
                 qs_ref, bias_ref, pick_ref, m_ref, acc_ref, s_ref):
    own = pl.program_id(1)
    tq, blk, hd = ATTN_TQ, MOBA_BLOCK, MOBA_HEAD_DIM
    own_f = own.astype(F32)

    key_row = lax.broadcasted_iota(jnp.int32, (blk, tq), 0)
    causal = jnp.where(key_row > lax.broadcasted_iota(jnp.int32, (blk, tq), 1), NEG, 0.0)
    key_row_f = key_row.astype(F32)
    blk_id = lax.broadcasted_iota(jnp.int32, (N_KV_BLOCKS, tq), 0)
    blk_id_f = blk_id.astype(F32)
    past = blk_id < own
    head_rows = lax.broadcasted_iota(jnp.int32, (LANES, tq), 0) < hd
    ones_rows = jnp.ones((ONES_ROWS, blk), BF16)

    q_t = jnp.transpose(q_ref[...].astype(F32))
    k_own = pl.multiple_of(own * blk, blk)

    def values(j, h):
        return jnp.concatenate([vt_ref[j, h * hd:(h + 1) * hd, :], ones_rows], axis=0)

    for pr in range(MOBA_HEADS // 2):
        cols = slice(pr * LANES, (pr + 1) * LANES)
        q_pair = q_t[pr * LANES:(pr + 1) * LANES]
        km = km_ref[0, :, cols]
        km_hi = km.astype(BF16)
        km_lo = (km - km_hi.astype(F32)).astype(BF16)
        for hh in range(2):
            h = 2 * pr + hh
            slope = 2.0 ** -(h + 1)
            mine = head_rows if hh == 0 else ~head_rows
            qh = jnp.where(mine, q_pair, 0.0).astype(BF16)

            sc = _dot(km_hi, qh) + _dot(km_lo, qh)
            sc = jnp.where(past, sc, -jnp.inf)
            for r in range(MOBA_TOPK):
                best = jnp.max(sc, axis=0, keepdims=True)
                idx = jnp.min(jnp.where(sc == best, blk_id_f, 1e9), axis=0, keepdims=True)
                pick_ref[h, r] = jnp.where(best > -jnp.inf, idx, -1.0)
                sc = jnp.where(blk_id_f == idx, -jnp.inf, sc)

            qs = jnp.where(mine, q_pair * (hd ** -0.5 * LOG2E), 0.0).astype(BF16)
            qs_ref[h] = qs
            bias = (slope * LOG2E) * key_row_f
            bias_ref[h] = bias

            s = _dot(k_ref[pl.ds(k_own, blk), cols], qs) + bias + causal
            m = jnp.max(s, axis=0, keepdims=True)
            m_ref[h] = m
            acc_ref[h] = _dot(values(own, h), jnp.exp2(s - m).astype(BF16))

    def scores(j, h, slot):
        start = pl.multiple_of(j * blk, blk)
        pr = h // 2
        kj = k_ref[pl.ds(start, blk), pr * LANES:(pr + 1) * LANES]
        s_ref[slot, h] = _dot(kj, qs_ref[h]) + bias_ref[h]

    def absorb(j, h, slot):
        jf = j.astype(F32)
        slope = 2.0 ** -(h + 1)
        chosen = (pick_ref[h, 0] == jf) | (pick_ref[h, 1] == jf) | (pick_ref[h, 2] == jf)
        shift = jnp.where(chosen, (-slope * blk * LOG2E) * (own_f - jf), NEG)
        m = m_ref[h]
        m_new = jnp.maximum(m, jnp.max(s_ref[slot, h], axis=0, keepdims=True) + shift)
        p = jnp.exp2(s_ref[slot, h] - (m_new - shift))
        m_ref[h] = m_new
        acc_ref[h] = jnp.exp2(m - m_new) * acc_ref[h] + _dot(values(j, h), p.astype(BF16))

    for h in range(MOBA_HEADS):
        scores(0, h, 0)

    def body(i, carry):
        a = 2 * i
        b = jnp.minimum(a + 1, own)
        c = jnp.minimum(a + 2, own)
        for h in range(MOBA_HEADS):
            scores(b, h, 1)
            absorb(a, h, 0)
        for h in range(MOBA_HEADS):
            scores(c, h, 0)
            absorb(b, h, 1)
        return carry

    lax.fori_loop(0, (own + 1) // 2, body, 0)

    o_t = jnp.concatenate([acc_ref[h, :hd] / acc_ref[h, hd:hd + 1] for h in range(MOBA_HEADS)],
                          axis=0)
    o_ref[...] = jnp.transpose(o_t).astype(BF16)


def _attention(qkvz, kmean, v_t):
    tq = ATTN_TQ
    q_tiles = SEQ // tq
    resident = functools.partial(pl.BlockSpec, pipeline_mode=pl.Buffered(1))
    return pl.pallas_call(
        _attn_kernel,
        grid=(BATCH, q_tiles),
        in_specs=[
            pl.BlockSpec((tq, BRANCH_WIDTH), lambda b, i: (b * q_tiles + i, 0)),
            resident((SEQ, BRANCH_WIDTH), lambda b, i: (b, 1)),
            resident((N_KV_BLOCKS, BRANCH_WIDTH, MOBA_BLOCK), lambda b, i: (b, 0, 0)),
            pl.BlockSpec((1, N_KV_BLOCKS, BRANCH_WIDTH), lambda b, i: (b, 0, 0)),
        ],
        out_specs=pl.BlockSpec((tq, BRANCH_WIDTH), lambda b, i: (b * q_tiles + i, 0)),
        out_shape=jax.ShapeDtypeStruct((M_ROWS, BRANCH_WIDTH), BF16),
        scratch_shapes=[
            pltpu.VMEM((MOBA_HEADS, LANES, tq), BF16),
            pltpu.VMEM((MOBA_HEADS, MOBA_BLOCK, tq), F32),
            pltpu.VMEM((MOBA_HEADS, MOBA_TOPK, 1, tq), F32),
            pltpu.VMEM((MOBA_HEADS, 1, tq), F32),
            pltpu.VMEM((MOBA_HEADS, MOBA_HEAD_DIM + ONES_ROWS, tq), F32),
            pltpu.VMEM((2, MOBA_HEADS, MOBA_BLOCK, tq), F32),
        ],
        compiler_params=_params("arbitrary", "arbitrary"),
        name="moba_attention",
    )(qkvz, qkvz, v_t, kmean)


def _merge_kernel(ys_ref, ym_ref, yp_ref, gs0, gs1, gm0, gm1, gp0, gp1,
                  ws_ref, wm_ref, wp_ref, wo_ref, x_ref, g_ref, gt_ref, g2_ref, sc2_ref, sh2_ref,
                  o_ref, h_ref):
    ys, ym, yp = ys_ref[...], ym_ref[...], yp_ref[...]
    halves = []
    half_w = D_MODEL // 2
    for hf, (gs, gm, gp) in enumerate(((gs0, gm0, gp0), (gs1, gm1, gp1))):
        cols = slice(hf * half_w, (hf + 1) * half_w)
        merged = (gs[...].astype(F32) * _dot(ys, ws_ref[:, cols])
                  + gm[...].astype(F32) * _dot(ym, wm_ref[:, cols])
                  + gp[...].astype(F32) * _dot(yp, wp_ref[:, cols]))
        halves.append(merged.astype(BF16))
    merged = jnp.concatenate(halves, axis=1)
    y = _dot(merged, wo_ref[...])
    x_new = x_ref[...] + gt_ref[0] * _rms_norm(y, g_ref[...])
    o_ref[...] = x_new
    h_ref[...] = _modulated_norm(x_new, g2_ref[...], sc2_ref[0], sh2_ref[0])


def _merge(ysgu, ymoba, ypool, uvg, w_sgu_out, w_moba_out, w_pool_out, w_out, x, g, gate,
           g_next, scale_next, shift_next):
    tm = MERGE_TM
    tiles_per_seq = SEQ // tm
    half_w = D_MODEL // 2
    gate_col0 = 2 * BRANCH_WIDTH // half_w
    resident = functools.partial(pl.BlockSpec, pipeline_mode=pl.Buffered(1))
    branch = pl.BlockSpec((tm, BRANCH_WIDTH), lambda i: (i, 0))
    gate_specs = [pl.BlockSpec((tm, half_w), functools.partial(lambda i, c: (i, c), c=gate_col0 + c))
                  for c in range(6)]
    w_branch = resident((BRANCH_WIDTH, D_MODEL), lambda i: (0, 0))
    rows = pl.BlockSpec((tm, D_MODEL), lambda i: (i, 0))
    vec = pl.BlockSpec((1, D_MODEL), lambda i: (0, 0))
    per_batch = pl.BlockSpec((1, 1, D_MODEL), lambda i: (i // tiles_per_seq, 0, 0))
    return pl.pallas_call(
        _merge_kernel,
        grid=(M_ROWS // tm,),
        in_specs=[branch, branch, branch, *gate_specs, w_branch, w_branch, w_branch,
                  resident((D_MODEL, D_MODEL), lambda i: (0, 0)),
                  rows, vec, per_batch, vec, per_batch, per_batch],
        out_specs=[rows, rows],
        out_shape=[jax.ShapeDtypeStruct((M_ROWS, D_MODEL), F32),
                   jax.ShapeDtypeStruct((M_ROWS, D_MODEL), BF16)],
        compiler_params=_params("arbitrary"),
        name="merge_outproj",
    )(ysgu, ymoba, ypool, uvg, uvg, uvg, uvg, uvg, uvg, w_sgu_out, w_moba_out, w_pool_out, w_out,
      x, g, gate, g_next, scale_next, shift_next)


def _up_kernel(h_ref, hh_ref, wg_ref, wv_ref, cg_ref, cv_ref, bg_ref, bv_ref, a_ref):
    tm = UP_TM
    first = (pl.program_id(0) % (SEQ // tm)) == 0
    h = h_ref[...]
    h_prev = jnp.where(first, jnp.zeros_like(hh_ref), hh_ref[...])

    def conv(w_ref, cw_ref, cb_ref):
        w = w_ref[...]
        z = _dot(h, w)
        z_prev = _dot(h_prev, w)[CONV_HALO - 8:]
        ze = jnp.concatenate([z_prev, z], axis=0)
        cw = cw_ref[...]
        return (cw[0:1, :] * ze[6:6 + tm] + cw[1:2, :] * ze[7:7 + tm] + cw[2:3, :] * z
                + cb_ref[...])

    gate = conv(wg_ref, cg_ref, bg_ref)
    val = conv(wv_ref, cv_ref, bv_ref)
    a_ref[...] = (_gelu_tanh(gate) * val).astype(BF16)


def _ffn_up(h, w_up, w_conv, b_conv):
    tm, tn, halo = UP_TM, UP_TN, CONV_HALO
    nj = D_FF // tn
    halo_blocks = tm // halo
    return pl.pallas_call(
        _up_kernel,
        grid=(M_ROWS // tm, nj),
        in_specs=[
            pl.BlockSpec((tm, D_MODEL), lambda i, j: (i, 0)),
            pl.BlockSpec((halo, D_MODEL), lambda i, j: (jnp.maximum(i * halo_blocks - 1, 0), 0)),
            pl.BlockSpec((D_MODEL, tn), lambda i, j: (0, j)),
            pl.BlockSpec((D_MODEL, tn), lambda i, j: (0, nj + j)),
            pl.BlockSpec((3, tn), lambda i, j: (0, j)),
            pl.BlockSpec((3, tn), lambda i, j: (0, nj + j)),
            pl.BlockSpec((1, tn), lambda i, j: (0, j)),
            pl.BlockSpec((1, tn), lambda i, j: (0, nj + j)),
        ],
        out_specs=pl.BlockSpec((tm, tn), lambda i, j: (i, j)),
        out_shape=jax.ShapeDtypeStruct((M_ROWS, D_FF), BF16),
        compiler_params=_params("arbitrary", "arbitrary"),
        name="ffn_up",
    )(h, h, w_up, w_up, w_conv, w_conv, b_conv, b_conv)


def _down_kernel(emit_h, a_ref, w_ref, x_ref, g_ref, gt_ref, *rest):
    if emit_h:
        g2_ref, sc2_ref, sh2_ref, o_ref, h_ref, acc_ref = rest
    else:
        o_ref, acc_ref = rest
    k = pl.program_id(1)

    @pl.when(k == 0)
    def _():
        acc_ref[...] = jnp.zeros_like(acc_ref)

    acc_ref[...] += _dot(a_ref[...], w_ref[...])

    @pl.when(k == pl.num_programs(1) - 1)
    def _():
        x_new = x_ref[...] + gt_ref[0] * _rms_norm(acc_ref[...], g_ref[...])
        o_ref[...] = x_new
        if emit_h:
            h_ref[...] = _modulated_norm(x_new, g2_ref[...], sc2_ref[0], sh2_ref[0])


def _ffn_down(a, w_down, x, g, gate, next_norm=None):
    tm, tk = DOWN_TM, DOWN_TK
    tiles_per_seq = SEQ // tm
    rows = pl.BlockSpec((tm, D_MODEL), lambda i, k: (i, 0))
    vec = pl.BlockSpec((1, D_MODEL), lambda i, k: (0, 0))
    per_batch = pl.BlockSpec((1, 1, D_MODEL), lambda i, k: (i // tiles_per_seq, 0, 0))
    emit_h = next_norm is not None
    in_specs = [pl.BlockSpec((tm, tk), lambda i, k: (i, k)),
                pl.BlockSpec((tk, D_MODEL), lambda i, k: (k, 0)),
                rows, vec, per_batch]
    out_specs = [rows]
    out_shape = [jax.ShapeDtypeStruct((M_ROWS, D_MODEL), F32)]
    args = [a, w_down, x, g, gate]
    if emit_h:
        in_specs += [vec, per_batch, per_batch]
        out_specs.append(rows)
        out_shape.append(jax.ShapeDtypeStruct((M_ROWS, D_MODEL), BF16))
        args += list(next_norm)
    return pl.pallas_call(
        functools.partial(_down_kernel, emit_h),
        grid=(M_ROWS // tm, D_FF // tk),
        in_specs=in_specs,
        out_specs=out_specs,
        out_shape=out_shape,
        scratch_shapes=[pltpu.VMEM((tm, D_MODEL), F32)],
        compiler_params=_params("arbitrary", "arbitrary"),
        name="ffn_down",
    )(*args)


def kernel(x, c, g_pre_mix, g_post_mix, g_pre_ffn, g_post_ffn, w_ada, b_ada, w_in, b_in,
           sgu_norm_g, sgu_w, sgu_b, pool_w, pool_scale, w_sgu_out, w_moba_out, w_pool_out,
           w_out, w_up, w_conv, b_conv, w_down):
    c_pad = jnp.pad(c, ((0, 8 - BATCH), (0, 0)))
    mod = _modulation(c_pad, w_ada, b_ada)[:, :BATCH]
    mod = mod.reshape(DEPTH, BATCH, N_MOD, 1, D_MODEL)
    shift1, scale1, gate1, shift2, scale2, gate2 = [mod[:, :, n] for n in range(N_MOD)]
    row = lambda v, l: v[l].reshape(1, -1)
    qkv_cols = slice(2 * BRANCH_WIDTH, 6 * BRANCH_WIDTH)

    xf = x.reshape(M_ROWS, D_MODEL)
    h = _prenorm(xf, row(g_pre_mix, 0), scale1[0], shift1[0])
    for l in range(DEPTH):
        w_in_l = w_in[l].astype(BF16)
        w_uvg = jnp.concatenate([w_in_l[:, :2 * BRANCH_WIDTH], w_in_l[:, 6 * BRANCH_WIDTH:]], axis=1)
        b_uvg = jnp.concatenate([b_in[l, :2 * BRANCH_WIDTH], b_in[l, 6 * BRANCH_WIDTH:]])
        uvg = _uvg_proj(h, w_uvg, b_uvg.reshape(1, -1))
        qkvz, kmean, v_t = _qkv_proj(h, w_in_l[:, qkv_cols], b_in[l, qkv_cols].reshape(1, -1))
        kmean = kmean.reshape(BATCH, N_KV_BLOCKS, BRANCH_WIDTH)

        sgu_bias_full = jnp.repeat(sgu_b[l].T, SGU_GROUP_DIM, axis=1)
        ysgu, ypool = _branches(uvg, qkvz, row(sgu_norm_g, l), sgu_w[l], sgu_bias_full, pool_w[l],
                                row(pool_scale, l))
        ymoba = _attention(qkvz, kmean, v_t)
        xf, h = _merge(ysgu, ymoba, ypool, uvg, w_sgu_out[l].astype(BF16),
                       w_moba_out[l].astype(BF16), w_pool_out[l].astype(BF16),
                       w_out[l].astype(BF16), xf, row(g_post_mix, l), gate1[l],
                       row(g_pre_ffn, l), scale2[l], shift2[l])

        a = _ffn_up(h, w_up[l].astype(BF16), w_conv[l], row(b_conv, l))
        if l + 1 < DEPTH:
            xf, h = _ffn_down(a, w_down[l].astype(BF16), xf, row(g_post_ffn, l), gate2[l],
                              (row(g_pre_mix, l + 1), scale1[l + 1], shift1[l + 1]))
        else:
            (xf,) = _ffn_down(a, w_down[l].astype(BF16), xf, row(g_post_ffn, l), gate2[l])
    return xf.reshape(BATCH, SEQ, D_MODEL)
```

```python
import functools

import jax
import jax.numpy as jnp
from jax import lax
from jax.experimental import pallas as pl
from jax.experimental.pallas import tpu as pltpu

F32 = jnp.float32
BF16 = jnp.bfloat16

D_MODEL = 2048
BATCH = 4
SEQ = 8192
DEPTH = 2
EPS = 1e-6
BRANCH_WIDTH = 512
SGU_GROUPS = 8
SGU_GROUP_DIM = 64
SGU_CHUNK = 128
MOBA_HEADS = 8
MOBA_HEAD_DIM = 64
MOBA_BLOCK = 256
MOBA_TOPK = 3
POOL_WINDOWS = (2, 4, 8, 16)
POOL_GROUPS = 4
POOL_GROUP_DIM = 128
IN_WIDTH = 6 * BRANCH_WIDTH + 3 * D_MODEL
D_FF = 5632
N_MOD = 6
M_ROWS = BATCH * SEQ
N_KV_BLOCKS = SEQ // MOBA_BLOCK

VMEM_LIMIT_BYTES = 56 * 1024 * 1024
LANES = 128
NEG = -1e30
LOG2E = 1.4426950408889634
ONES_ROWS = 16

MOD_TN = 1024
PRENORM_TM = 512
QKV_TM = 1024
UVG_TM, UVG_TN = 1024, 1024
BRANCH_TR = 512
ATTN_TQ = MOBA_BLOCK
MERGE_TM = 512
UP_TM, UP_TN = 1024, 512
UP_COLS = 512
CONV_HALO = 16
DOWN_TM, DOWN_TK = 512, 1408
POOL_HALO = 16


GELU_C = 0.7978845608028654
GELU_A = 0.044715


def _sigmoid(x):
    return 1.0 / (1.0 + jnp.exp(-x))


def _sigmoid2(x):
    return 1.0 / (1.0 + jnp.exp2(-LOG2E * x))


def _gelu_tanh(x):
    k1 = -2.0 * GELU_C * LOG2E
    return x / (1.0 + jnp.exp2(x * (k1 + (k1 * GELU_A) * (x * x))))


def _rms_norm(x, g):
    ms = jnp.mean(x * x, axis=-1, keepdims=True)
    return x * lax.rsqrt(ms + EPS) * g


def _modulated_norm(x, g, scale, shift):
    return (_rms_norm(x, g) * (1.0 + scale) + shift).astype(BF16)


def _dot(a, b):
    return jnp.dot(a, b, preferred_element_type=F32)


def _dot_nt(a, b):
    return lax.dot_general(a, b, (((1,), (1,)), ((), ())), preferred_element_type=F32)


def _params(*sem):
    return pltpu.CompilerParams(dimension_semantics=sem, vmem_limit_bytes=VMEM_LIMIT_BYTES)


def _mod_kernel(c_ref, w_ref, b_ref, o_ref):
    c = c_ref[...]
    cond = (c * _sigmoid(c)).astype(BF16)
    o_ref[0] = _dot(cond, w_ref[0].astype(BF16)) + b_ref[0]


def _modulation(c_pad, w_ada, b_ada):
    rows = c_pad.shape[0]
    n = N_MOD * D_MODEL
    return pl.pallas_call(
        _mod_kernel,
        grid=(DEPTH, n // MOD_TN),
        in_specs=[
            pl.BlockSpec((rows, D_MODEL), lambda l, j: (0, 0)),
            pl.BlockSpec((1, D_MODEL, MOD_TN), lambda l, j: (l, 0, j)),
            pl.BlockSpec((1, 1, MOD_TN), lambda l, j: (l, 0, j)),
        ],
        out_specs=pl.BlockSpec((1, rows, MOD_TN), lambda l, j: (l, 0, j)),
        out_shape=jax.ShapeDtypeStruct((DEPTH, rows, n), F32),
        compiler_params=_params("arbitrary", "arbitrary"),
        name="adaln_mod",
    )(c_pad, w_ada, b_ada.reshape(DEPTH, 1, n))


def _prenorm_kernel(x_ref, g_ref, sc_ref, sh_ref, h_ref):
    h_ref[...] = _modulated_norm(x_ref[...], g_ref[...], sc_ref[0], sh_ref[0])


def _prenorm(x, g, scale, shift):
    tm = PRENORM_TM
    tiles_per_seq = SEQ // tm
    return pl.pallas_call(
        _prenorm_kernel,
        grid=(M_ROWS // tm,),
        in_specs=[
            pl.BlockSpec((tm, D_MODEL), lambda i: (i, 0)),
            pl.BlockSpec((1, D_MODEL), lambda i: (0, 0)),
            pl.BlockSpec((1, 1, D_MODEL), lambda i: (i // tiles_per_seq, 0, 0)),
            pl.BlockSpec((1, 1, D_MODEL), lambda i: (i // tiles_per_seq, 0, 0)),
        ],
        out_specs=pl.BlockSpec((tm, D_MODEL), lambda i: (i, 0)),
        out_shape=jax.ShapeDtypeStruct((M_ROWS, D_MODEL), BF16),
        compiler_params=_params("arbitrary"),
        name="prenorm",
    )(x, g, scale, shift)


def _qkv_kernel(h_ref, w_ref, b_ref, o_ref, kmean_ref, vt_ref):
    acc = _dot(h_ref[...], w_ref[...]) + b_ref[...]
    o_ref[...] = acc.astype(BF16)
    for r in range(QKV_TM // MOBA_BLOCK):
        rows = slice(r * MOBA_BLOCK, (r + 1) * MOBA_BLOCK)
        kmean_ref[0, r:r + 1, :] = jnp.mean(acc[rows, BRANCH_WIDTH:2 * BRANCH_WIDTH],
                                            axis=0, keepdims=True)
        vt_ref[r] = jnp.transpose(acc[rows, 2 * BRANCH_WIDTH:3 * BRANCH_WIDTH]).astype(BF16)


def _qkv_proj(h, w, b):
    tm = QKV_TM
    n = 4 * BRANCH_WIDTH
    blocks_per_tile = tm // MOBA_BLOCK
    return pl.pallas_call(
        _qkv_kernel,
        grid=(M_ROWS // tm,),
        in_specs=[
            pl.BlockSpec((tm, D_MODEL), lambda i: (i, 0)),
            pl.BlockSpec((D_MODEL, n), lambda i: (0, 0), pipeline_mode=pl.Buffered(1)),
            pl.BlockSpec((1, n), lambda i: (0, 0)),
        ],
        out_specs=[
            pl.BlockSpec((tm, n), lambda i: (i, 0)),
            pl.BlockSpec((1, blocks_per_tile, BRANCH_WIDTH), lambda i: (i, 0, 0)),
            pl.BlockSpec((blocks_per_tile, BRANCH_WIDTH, MOBA_BLOCK), lambda i: (i, 0, 0)),
        ],
        out_shape=[
            jax.ShapeDtypeStruct((M_ROWS, n), BF16),
            jax.ShapeDtypeStruct((M_ROWS // tm, blocks_per_tile, BRANCH_WIDTH), F32),
            jax.ShapeDtypeStruct((M_ROWS // MOBA_BLOCK, BRANCH_WIDTH, MOBA_BLOCK), BF16),
        ],
        compiler_params=_params("arbitrary"),
        name="qkv_proj",
    )(h, w, b)


def _uvg_kernel(h_ref, w_ref, b_ref, o_ref):
    j = pl.program_id(1)

    @pl.when(j == 0)
    def _():
        o_ref[...] = _gelu_tanh(_dot(h_ref[...], w_ref[...]) + b_ref[...]).astype(BF16)

    @pl.when(j > 0)
    def _():
        o_ref[...] = _sigmoid2(_dot(h_ref[...], w_ref[...]) + b_ref[...]).astype(BF16)


def _uvg_proj(h, w, b):
    tm, tn = UVG_TM, UVG_TN
    n = 2 * BRANCH_WIDTH + 3 * D_MODEL
    return pl.pallas_call(
        _uvg_kernel,
        grid=(M_ROWS // tm, n // tn),
        in_specs=[
            pl.BlockSpec((tm, D_MODEL), lambda i, j: (i, 0)),
            pl.BlockSpec((D_MODEL, tn), lambda i, j: (0, j)),
            pl.BlockSpec((1, tn), lambda i, j: (0, j)),
        ],
        out_specs=pl.BlockSpec((tm, tn), lambda i, j: (i, j)),
        out_shape=jax.ShapeDtypeStruct((M_ROWS, n), BF16),
        compiler_params=_params("arbitrary", "arbitrary"),
        name="uvg_proj",
    )(h, w, b)


def _branch_kernel(u_ref, v_ref, z_ref, zprev_ref, ng_ref, sw_ref, sb_ref, pw_ref, ps_ref,
                   ysgu_ref, ypool_ref):
    i = pl.program_id(0)
    tr = BRANCH_TR
    first = (i % (SEQ // tr)) == 0

    v = v_ref[...].astype(F32)
    mu = jnp.mean(v, axis=-1, keepdims=True)
    vc = v - mu
    var = jnp.mean(vc * vc, axis=-1, keepdims=True)
    vn = (vc * lax.rsqrt(var + EPS) * ng_ref[...]).astype(BF16)

    row = lax.broadcasted_iota(jnp.int32, (SGU_CHUNK, SGU_CHUNK), 0)
    col = lax.broadcasted_iota(jnp.int32, (SGU_CHUNK, SGU_CHUNK), 1)
    causal = row >= col
    wm = [jnp.where(causal, sw_ref[g], 0.0).astype(BF16) for g in range(SGU_GROUPS)]
    lane = lax.broadcasted_iota(jnp.int32, (SGU_CHUNK, LANES), 1)
    low_half = lane < SGU_GROUP_DIM
    zero = jnp.zeros((SGU_CHUNK, LANES), BF16)

    for c in range(tr // SGU_CHUNK):
        rows = slice(c * SGU_CHUNK, (c + 1) * SGU_CHUNK)
        for pr in range(SGU_GROUPS // 2):
            cols = slice(pr * LANES, (pr + 1) * LANES)
            blk = vn[rows, cols]
            mixed = (_dot(wm[2 * pr], jnp.where(low_half, blk, zero))
                     + _dot(wm[2 * pr + 1], jnp.where(low_half, zero, blk))
                     + sb_ref[:, cols])
            ysgu_ref[rows, cols] = (u_ref[rows, cols].astype(F32) * mixed).astype(BF16)

    z = z_ref[...].astype(F32)
    zprev = jnp.where(first, 0.0, zprev_ref[...].astype(F32))
    pos = (i % (SEQ // tr)) * tr + lax.broadcasted_iota(jnp.int32, (tr, POOL_GROUP_DIM), 0)
    for gi, w in enumerate(POOL_WINDOWS):
        cols = slice(gi * POOL_GROUP_DIM, (gi + 1) * POOL_GROUP_DIM)
        zg = z[:, cols]
        cur = jnp.concatenate([zprev[:, cols], zg], axis=0)
        d = 1
        while d < w:
            cur = cur[d:] + cur[:-d]
            d *= 2
        win_sum = cur[cur.shape[0] - tr:]
        count = jnp.minimum(pos + 1, w).astype(F32)
        pooled = win_sum / count - zg
        y = _dot(pooled.astype(BF16), pw_ref[gi].astype(BF16)) * ps_ref[:, cols]
        ypool_ref[:, cols] = y.astype(BF16)


def _branches(uvg, qkvz, sgu_norm_g, sgu_w, sgu_bias_full, pool_w, pool_scale):
    tr = BRANCH_TR
    halo_blocks = tr // POOL_HALO
    return pl.pallas_call(
        _branch_kernel,
        grid=(M_ROWS // tr,),
        in_specs=[
            pl.BlockSpec((tr, BRANCH_WIDTH), lambda i: (i, 0)),
            pl.BlockSpec((tr, BRANCH_WIDTH), lambda i: (i, 1)),
            pl.BlockSpec((tr, BRANCH_WIDTH), lambda i: (i, 3)),
            pl.BlockSpec((POOL_HALO, BRANCH_WIDTH),
                         lambda i: (jnp.maximum(i * halo_blocks - 1, 0), 3)),
            pl.BlockSpec((1, BRANCH_WIDTH), lambda i: (0, 0)),
            pl.BlockSpec((SGU_GROUPS, SGU_CHUNK, SGU_CHUNK), lambda i: (0, 0, 0)),
            pl.BlockSpec((SGU_CHUNK, BRANCH_WIDTH), lambda i: (0, 0)),
            pl.BlockSpec((POOL_GROUPS, POOL_GROUP_DIM, POOL_GROUP_DIM), lambda i: (0, 0, 0)),
            pl.BlockSpec((1, BRANCH_WIDTH), lambda i: (0, 0)),
        ],
        out_specs=[
            pl.BlockSpec((tr, BRANCH_WIDTH), lambda i: (i, 0)),
            pl.BlockSpec((tr, BRANCH_WIDTH), lambda i: (i, 0)),
        ],
        out_shape=[
            jax.ShapeDtypeStruct((M_ROWS, BRANCH_WIDTH), BF16),
            jax.ShapeDtypeStruct((M_ROWS, BRANCH_WIDTH), BF16),
        ],
        compiler_params=_params("arbitrary"),
        name="sgu_pool",
    )(uvg, uvg, qkvz, qkvz, sgu_norm_g, sgu_w, sgu_bias_full, pool_w, pool_scale)


def _attn_kernel(q_ref, k_ref, vt_ref, km_ref, o_ref,
                 qs_ref, bias_ref, pick_ref, m_ref, acc_ref, s_ref):
    own = pl.program_id(1)
    tq, blk, hd = ATTN_TQ, MOBA_BLOCK, MOBA_HEAD_DIM
    own_f = own.astype(F32)

    key_row = lax.broadcasted_iota(jnp.int32, (blk, tq), 0)
    causal = jnp.where(key_row > lax.broadcasted_iota(jnp.int32, (blk, tq), 1), NEG, 0.0)
    key_row_f = key_row.astype(F32)
    blk_id = lax.broadcasted_iota(jnp.int32, (N_KV_BLOCKS, tq), 0)
    blk_id_f = blk_id.astype(F32)
    past = blk_id < own
    head_rows = lax.broadcasted_iota(jnp.int32, (LANES, tq), 0) < hd
    ones_rows = jnp.ones((ONES_ROWS, blk), BF16)

    q_t = jnp.transpose(q_ref[...].astype(F32))
    k_own = pl.multiple_of(own * blk, blk)

    def values(j, h):
        return jnp.concatenate([vt_ref[j, h * hd:(h + 1) * hd, :], ones_rows], axis=0)

    for pr in range(MOBA_HEADS // 2):
        cols = slice(pr * LANES, (pr + 1) * LANES)
        q_pair = q_t[pr * LANES:(pr + 1) * LANES]
        km = km_ref[0, :, cols]
        km_hi = km.astype(BF16)
        km_lo = (km - km_hi.astype(F32)).astype(BF16)
        for hh in range(2):
            h = 2 * pr + hh
            slope = 2.0 ** -(h + 1)
            mine = head_rows if hh == 0 else ~head_rows
            qh = jnp.where(mine, q_pair, 0.0).astype(BF16)

            sc = _dot(km_hi, qh) + _dot(km_lo, qh)
            sc = jnp.where(past, sc, -jnp.inf)
            for r in range(MOBA_TOPK):
                best = jnp.max(sc, axis=0, keepdims=True)
                idx = jnp.min(jnp.where(sc == best, blk_id_f, 1e9), axis=0, keepdims=True)
                pick_ref[h, r] = jnp.where(best > -jnp.inf, idx, -1.0)
                sc = jnp.where(blk_id_f == idx, -jnp.inf, sc)

            qs = jnp.where(mine, q_pair * (hd ** -0.5 * LOG2E), 0.0).astype(BF16)
            qs_ref[h] = qs
            bias = (slope * LOG2E) * key_row_f
            bias_ref[h] = bias

            s = _dot(k_ref[pl.ds(k_own, blk), cols], qs) + bias + causal
            m = jnp.max(s, axis=0, keepdims=True)
            m_ref[h] = m
            acc_ref[h] = _dot(values(own, h), jnp.exp2(s - m).astype(BF16))

    def scores(j, h, slot):
        start = pl.multiple_of(j * blk, blk)
        pr = h // 2
        kj = k_ref[pl.ds(start, blk), pr * LANES:(pr + 1) * LANES]
        s_ref[slot, h] = _dot(kj, qs_ref[h]) + bias_ref[h]

    def absorb(j, h, slot):
        jf = j.astype(F32)
        slope = 2.0 ** -(h + 1)
        chosen = (pick_ref[h, 0] == jf) | (pick_ref[h, 1] == jf) | (pick_ref[h, 2] == jf)
        shift = jnp.where(chosen, (-slope * blk * LOG2E) * (own_f - jf), NEG)
        m = m_ref[h]
        m_new = jnp.maximum(m, jnp.max(s_ref[slot, h], axis=0, keepdims=True) + shift)
        p = jnp.exp2(s_ref[slot, h] - (m_new - shift))
        m_ref[h] = m_new
        acc_ref[h] = jnp.exp2(m - m_new) * acc_ref[h] + _dot(values(j, h), p.astype(BF16))

    for h in range(MOBA_HEADS):
        scores(0, h, 0)

    def body(i, carry):
        a = 2 * i
        b = jnp.minimum(a + 1, own)
        c = jnp.minimum(a + 2, own)
        for h in range(MOBA_HEADS):
            scores(b, h, 1)
            absorb(a, h, 0)
        for h in range(MOBA_HEADS):
            scores(c, h, 0)
            absorb(b, h, 1)
        return carry

    lax.fori_loop(0, (own + 1) // 2, body, 0)

    o_t = jnp.concatenate([acc_ref[h, :hd] / acc_ref[h, hd:hd + 1] for h in range(MOBA_HEADS)],
                          axis=0)
    o_ref[...] = jnp.transpose(o_t).astype(BF16)


def _attention(qkvz, kmean, v_t):
    tq = ATTN_TQ
    q_tiles = SEQ // tq
    resident = functools.partial(pl.BlockSpec, pipeline_mode=pl.Buffered(1))
    return pl.pallas_call(
        _attn_kernel,
        grid=(BATCH, q_tiles),
        in_specs=[
            pl.BlockSpec((tq, BRANCH_WIDTH), lambda b, i: (b * q_tiles + i, 0)),
            resident((SEQ, BRANCH_WIDTH), lambda b, i: (b, 1)),
            resident((N_KV_BLOCKS, BRANCH_WIDTH, MOBA_BLOCK), lambda b, i: (b, 0, 0)),
            pl.BlockSpec((1, N_KV_BLOCKS, BRANCH_WIDTH), lambda b, i: (b, 0, 0)),
        ],
        out_specs=pl.BlockSpec((tq, BRANCH_WIDTH), lambda b, i: (b * q_tiles + i, 0)),
        out_shape=jax.ShapeDtypeStruct((M_ROWS, BRANCH_WIDTH), BF16),
        scratch_shapes=[
            pltpu.VMEM((MOBA_HEADS, LANES, tq), BF16),
            pltpu.VMEM((MOBA_HEADS, MOBA_BLOCK, tq), F32),
            pltpu.VMEM((MOBA_HEADS, MOBA_TOPK, 1, tq), F32),
            pltpu.VMEM((MOBA_HEADS, 1, tq), F32),
            pltpu.VMEM((MOBA_HEADS, MOBA_HEAD_DIM + ONES_ROWS, tq), F32),
            pltpu.VMEM((2, MOBA_HEADS, MOBA_BLOCK, tq), F32),
        ],
        compiler_params=_params("arbitrary", "arbitrary"),
        name="moba_attention",
    )(qkvz, qkvz, v_t, kmean)


def _merge_kernel(ys_ref, ym_ref, yp_ref, gs0, gs1, gm0, gm1, gp0, gp1,
                  ws_ref, wm_ref, wp_ref, wo_ref, x_ref, g_ref, gt_ref, g2_ref, sc2_ref, sh2_ref,
                  o_ref, h_ref):
    ys, ym, yp = ys_ref[...], ym_ref[...], yp_ref[...]
    halves = []
    half_w = D_MODEL // 2
    for hf, (gs, gm, gp) in enumerate(((gs0, gm0, gp0), (gs1, gm1, gp1))):
        cols = slice(hf * half_w, (hf + 1) * half_w)
        merged = (gs[...].astype(F32) * _dot(ys, ws_ref[:, cols])
                  + gm[...].astype(F32) * _dot(ym, wm_ref[:, cols])
                  + gp[...].astype(F32) * _dot(yp, wp_ref[:, cols]))
        halves.append(merged.astype(BF16))
    merged = jnp.concatenate(halves, axis=1)
    y = _dot(merged, wo_ref[...])
    x_new = x_ref[...] + gt_ref[0] * _rms_norm(y, g_ref[...])
    o_ref[...] = x_new
    h_ref[...] = _modulated_norm(x_new, g2_ref[...], sc2_ref[0], sh2_ref[0])


def _merge(ysgu, ymoba, ypool, uvg, w_sgu_out, w_moba_out, w_pool_out, w_out, x, g, gate,
           g_next, scale_next, shift_next):
    tm = MERGE_TM
    tiles_per_seq = SEQ // tm
    half_w = D_MODEL // 2
    gate_col0 = 2 * BRANCH_WIDTH // half_w
    resident = functools.partial(pl.BlockSpec, pipeline_mode=pl.Buffered(1))
    branch = pl.BlockSpec((tm, BRANCH_WIDTH), lambda i: (i, 0))
    gate_specs = [pl.BlockSpec((tm, half_w), functools.partial(lambda i, c: (i, c), c=gate_col0 + c))
                  for c in range(6)]
    w_branch = resident((BRANCH_WIDTH, D_MODEL), lambda i: (0, 0))
    rows = pl.BlockSpec((tm, D_MODEL), lambda i: (i, 0))
    vec = pl.BlockSpec((1, D_MODEL), lambda i: (0, 0))
    per_batch = pl.BlockSpec((1, 1, D_MODEL), lambda i: (i // tiles_per_seq, 0, 0))
    return pl.pallas_call(
        _merge_kernel,
        grid=(M_ROWS // tm,),
        in_specs=[branch, branch, branch, *gate_specs, w_branch, w_branch, w_branch,
                  resident((D_MODEL, D_MODEL), lambda i: (0, 0)),
                  rows, vec, per_batch, vec, per_batch, per_batch],
        out_specs=[rows, rows],
        out_shape=[jax.ShapeDtypeStruct((M_ROWS, D_MODEL), F32),
                   jax.ShapeDtypeStruct((M_ROWS, D_MODEL), BF16)],
        compiler_params=_params("arbitrary"),
        name="merge_outproj",
    )(ysgu, ymoba, ypool, uvg, uvg, uvg, uvg, uvg, uvg, w_sgu_out, w_moba_out, w_pool_out, w_out,
      x, g, gate, g_next, scale_next, shift_next)


def _up_kernel(h_ref, hh_ref, wg_ref, wv_ref, cg_ref, cv_ref, bg_ref, bv_ref, a_ref):
    tm = UP_TM
    first_in_seq = (pl.program_id(0) % (SEQ // tm)) == 0
    h = h_ref[...]
    h_prev = jnp.where(first_in_seq, jnp.zeros_like(hh_ref), hh_ref[...])

    for c in range(UP_TN // UP_COLS):
        cols = slice(c * UP_COLS, (c + 1) * UP_COLS)

        def conv(w_ref, cw_ref, cb_ref):
            w = w_ref[:, cols]
            z = _dot(h, w)
            z_prev = _dot(h_prev, w)[CONV_HALO - 8:]
            ze = jnp.concatenate([z_prev, z], axis=0)
            return (cw_ref[0:1, cols] * ze[6:6 + tm] + cw_ref[1:2, cols] * ze[7:7 + tm]
                    + cw_ref[2:3, cols] * z + cb_ref[:, cols])

        gate = conv(wg_ref, cg_ref, bg_ref)
        val = conv(wv_ref, cv_ref, bv_ref)
        a_ref[:, cols] = (_gelu_tanh(gate) * val).astype(BF16)


def _ffn_up(h, w_up, w_conv, b_conv):
    tm, tn, halo = UP_TM, UP_TN, CONV_HALO
    nj = D_FF // tn
    halo_blocks = tm // halo
    return pl.pallas_call(
        _up_kernel,
        grid=(M_ROWS // tm, nj),
        in_specs=[
            pl.BlockSpec((tm, D_MODEL), lambda i, j: (i, 0)),
            pl.BlockSpec((halo, D_MODEL), lambda i, j: (jnp.maximum(i * halo_blocks - 1, 0), 0)),
            pl.BlockSpec((D_MODEL, tn), lambda i, j: (0, j)),
            pl.BlockSpec((D_MODEL, tn), lambda i, j: (0, nj + j)),
            pl.BlockSpec((3, tn), lambda i, j: (0, j)),
            pl.BlockSpec((3, tn), lambda i, j: (0, nj + j)),
            pl.BlockSpec((1, tn), lambda i, j: (0, j)),
            pl.BlockSpec((1, tn), lambda i, j: (0, nj + j)),
        ],
        out_specs=pl.BlockSpec((tm, tn), lambda i, j: (i, j)),
        out_shape=jax.ShapeDtypeStruct((M_ROWS, D_FF), BF16),
        compiler_params=_params("arbitrary", "arbitrary"),
        name="ffn_up",
    )(h, h, w_up, w_up, w_conv, w_conv, b_conv, b_conv)


def _down_kernel(emit_h, a_ref, w_ref, x_ref, g_ref, gt_ref, *rest):
    if emit_h:
        g2_ref, sc2_ref, sh2_ref, o_ref, h_ref, acc_ref = rest
    else:
        o_ref, acc_ref = rest
    k = pl.program_id(1)
    last = pl.num_programs(1) - 1

    @pl.when(k == 0)
    def _():
        acc_ref[...] = _dot(a_ref[...], w_ref[...])

    @pl.when((k > 0) & (k < last))
    def _():
        acc_ref[...] += _dot(a_ref[...], w_ref[...])

    @pl.when(k == last)
    def _():
        y = acc_ref[...] + _dot(a_ref[...], w_ref[...])
        x_new = x_ref[...] + gt_ref[0] * _rms_norm(y, g_ref[...])
        o_ref[...] = x_new
        if emit_h:
            h_ref[...] = _modulated_norm(x_new, g2_ref[...], sc2_ref[0], sh2_ref[0])


def _ffn_down(a, w_down, x, g, gate, next_norm=None):
    tm, tk = DOWN_TM, DOWN_TK
    tiles_per_seq = SEQ // tm
    rows = pl.BlockSpec((tm, D_MODEL), lambda i, k: (i, 0))
    vec = pl.BlockSpec((1, D_MODEL), lambda i, k: (0, 0))
    per_batch = pl.BlockSpec((1, 1, D_MODEL), lambda i, k: (i // tiles_per_seq, 0, 0))
    emit_h = next_norm is not None
    in_specs = [pl.BlockSpec((tm, tk), lambda i, k: (i, k)),
                pl.BlockSpec((tk, D_MODEL), lambda i, k: (k, 0)),
                rows, vec, per_batch]
    out_specs = [rows]
    out_shape = [jax.ShapeDtypeStruct((M_ROWS, D_MODEL), F32)]
    args = [a, w_down, x, g, gate]
    if emit_h:
        in_specs += [vec, per_batch, per_batch]
        out_specs.append(rows)
        out_shape.append(jax.ShapeDtypeStruct((M_ROWS, D_MODEL), BF16))
        args += list(next_norm)
    return pl.pallas_call(
        functools.partial(_down_kernel, emit_h),
        grid=(M_ROWS // tm, D_FF // tk),
        in_specs=in_specs,
        out_specs=out_specs,
        out_shape=out_shape,
        scratch_shapes=[pltpu.VMEM((tm, D_MODEL), F32)],
        compiler_params=_params("arbitrary", "arbitrary"),
        name="ffn_down",
    )(*args)


def kernel(x, c, g_pre_mix, g_post_mix, g_pre_ffn, g_post_ffn, w_ada, b_ada, w_in, b_in,
           sgu_norm_g, sgu_w, sgu_b, pool_w, pool_scale, w_sgu_out, w_moba_out, w_pool_out,
           w_out, w_up, w_conv, b_conv, w_down):
    c_pad = jnp.pad(c, ((0, 8 - BATCH), (0, 0)))
    mod = _modulation(c_pad, w_ada, b_ada)[:, :BATCH]
    mod = mod.reshape(DEPTH, BATCH, N_MOD, 1, D_MODEL)
    shift1, scale1, gate1, shift2, scale2, gate2 = [mod[:, :, n] for n in range(N_MOD)]
    row = lambda v, l: v[l].reshape(1, -1)
    qkv_cols = slice(2 * BRANCH_WIDTH, 6 * BRANCH_WIDTH)

    xf = x.reshape(M_ROWS, D_MODEL)
    h = _prenorm(xf, row(g_pre_mix, 0), scale1[0], shift1[0])
    for l in range(DEPTH):
        w_in_l = w_in[l].astype(BF16)
        w_uvg = jnp.concatenate([w_in_l[:, :2 * BRANCH_WIDTH], w_in_l[:, 6 * BRANCH_WIDTH:]], axis=1)
        b_uvg = jnp.concatenate([b_in[l, :2 * BRANCH_WIDTH], b_in[l, 6 * BRANCH_WIDTH:]])
        uvg = _uvg_proj(h, w_uvg, b_uvg.reshape(1, -1))
        qkvz, kmean, v_t = _qkv_proj(h, w_in_l[:, qkv_cols], b_in[l, qkv_cols].reshape(1, -1))
        kmean = kmean.reshape(BATCH, N_KV_BLOCKS, BRANCH_WIDTH)

        sgu_bias_full = jnp.repeat(sgu_b[l].T, SGU_GROUP_DIM, axis=1)
        ysgu, ypool = _branches(uvg, qkvz, row(sgu_norm_g, l), sgu_w[l], sgu_bias_full, pool_w[l],
                                row(pool_scale, l))
        ymoba = _attention(qkvz, kmean, v_t)
        xf, h = _merge(ysgu, ymoba, ypool, uvg, w_sgu_out[l].astype(BF16),
                       w_moba_out[l].astype(BF16), w_pool_out[l].astype(BF16),
                       w_out[l].astype(BF16), xf, row(g_post_mix, l), gate1[l],
                       row(g_pre_ffn, l), scale2[l], shift2[l])

        a = _ffn_up(h, w_up[l].astype(BF16), w_conv[l], row(b_conv, l))
        if l + 1 < DEPTH:
            xf, h = _ffn_down(a, w_down[l].astype(BF16), xf, row(g_post_ffn, l), gate2[l],
                              (row(g_pre_mix, l + 1), scale1[l + 1], shift1[l + 1]))
        else:
            (xf,) = _ffn_down(a, w_down[l].astype(BF16), xf, row(g_post_ffn, l), gate2[l])
    return xf.reshape(BATCH, SEQ, D_MODEL)
```

```python
import functools

import jax
import jax.numpy as jnp
from jax import lax
from jax.experimental import pallas as pl
from jax.experimental.pallas import tpu as pltpu

F32 = jnp.float32
BF16 = jnp.bfloat16

D_MODEL = 2048
BATCH = 4
SEQ = 8192
DEPTH = 2
EPS = 1e-6
BRANCH_WIDTH = 512
SGU_GROUPS = 8
SGU_GROUP_DIM = 64
SGU_CHUNK = 128
MOBA_HEADS = 8
MOBA_HEAD_DIM = 64
MOBA_BLOCK = 256
MOBA_TOPK = 3
POOL_WINDOWS = (2, 4, 8, 16)
POOL_GROUPS = 4
POOL_GROUP_DIM = 128
IN_WIDTH = 6 * BRANCH_WIDTH + 3 * D_MODEL
D_FF = 5632
N_MOD = 6
M_ROWS = BATCH * SEQ
N_KV_BLOCKS = SEQ // MOBA_BLOCK

VMEM_LIMIT_BYTES = 56 * 1024 * 1024
LANES = 128
NEG = -1e30
LOG2E = 1.4426950408889634
ONES_ROWS = 16

MOD_TN = 1024
PRENORM_TM = 512
QKV_TM = 1024
UVG_TM, UVG_TN = 2048, 1024
BRANCH_TR = 512
ATTN_TQ = MOBA_BLOCK
MERGE_TM = 512
UP_TM, UP_TN = 1024, 512
UP_COLS = 512
CONV_HALO = 16
DOWN_TM, DOWN_TK = 512, 1408
POOL_HALO = 16


GELU_C = 0.7978845608028654
GELU_A = 0.044715


def _sigmoid(x):
    return 1.0 / (1.0 + jnp.exp(-x))


def _sigmoid2(x):
    return 1.0 / (1.0 + jnp.exp2(-LOG2E * x))


def _gelu_tanh(x):
    k1 = -2.0 * GELU_C * LOG2E
    return x / (1.0 + jnp.exp2(x * (k1 + (k1 * GELU_A) * (x * x))))


def _rms_norm(x, g):
    ms = jnp.mean(x * x, axis=-1, keepdims=True)
    return (x * lax.rsqrt(ms + EPS)) * g


def _modulated_norm(x, g, scale, shift):
    return (_rms_norm(x, g * (1.0 + scale)) + shift).astype(BF16)


def _dot(a, b):
    return jnp.dot(a, b, preferred_element_type=F32)


def _dot_nt(a, b):
    return lax.dot_general(a, b, (((1,), (1,)), ((), ())), preferred_element_type=F32)


def _params(*sem):
    return pltpu.CompilerParams(dimension_semantics=sem, vmem_limit_bytes=VMEM_LIMIT_BYTES)


def _mod_kernel(c_ref, w_ref, b_ref, o_ref):
    c = c_ref[...]
    cond = (c * _sigmoid(c)).astype(BF16)
    o_ref[0] = _dot(cond, w_ref[0].astype(BF16)) + b_ref[0]


def _modulation(c_pad, w_ada, b_ada):
    rows = c_pad.shape[0]
    n = N_MOD * D_MODEL
    return pl.pallas_call(
        _mod_kernel,
        grid=(DEPTH, n // MOD_TN),
        in_specs=[
            pl.BlockSpec((rows, D_MODEL), lambda l, j: (0, 0)),
            pl.BlockSpec((1, D_MODEL, MOD_TN), lambda l, j: (l, 0, j)),
            pl.BlockSpec((1, 1, MOD_TN), lambda l, j: (l, 0, j)),
        ],
        out_specs=pl.BlockSpec((1, rows, MOD_TN), lambda l, j: (l, 0, j)),
        out_shape=jax.ShapeDtypeStruct((DEPTH, rows, n), F32),
        compiler_params=_params("arbitrary", "arbitrary"),
        name="adaln_mod",
    )(c_pad, w_ada, b_ada.reshape(DEPTH, 1, n))


def _prenorm_kernel(x_ref, g_ref, sc_ref, sh_ref, h_ref):
    h_ref[...] = _modulated_norm(x_ref[...], g_ref[...], sc_ref[0], sh_ref[0])


def _prenorm(x, g, scale, shift):
    tm = PRENORM_TM
    tiles_per_seq = SEQ // tm
    return pl.pallas_call(
        _prenorm_kernel,
        grid=(M_ROWS // tm,),
        in_specs=[
            pl.BlockSpec((tm, D_MODEL), lambda i: (i, 0)),
            pl.BlockSpec((1, D_MODEL), lambda i: (0, 0)),
            pl.BlockSpec((1, 1, D_MODEL), lambda i: (i // tiles_per_seq, 0, 0)),
            pl.BlockSpec((1, 1, D_MODEL), lambda i: (i // tiles_per_seq, 0, 0)),
        ],
        out_specs=pl.BlockSpec((tm, D_MODEL), lambda i: (i, 0)),
        out_shape=jax.ShapeDtypeStruct((M_ROWS, D_MODEL), BF16),
        compiler_params=_params("arbitrary"),
        name="prenorm",
    )(x, g, scale, shift)


def _qkv_kernel(h_ref, w_ref, b_ref, o_ref, kmean_ref, vt_ref):
    acc = _dot(h_ref[...], w_ref[...]) + b_ref[...]
    o_ref[...] = acc.astype(BF16)
    for r in range(QKV_TM // MOBA_BLOCK):
        rows = slice(r * MOBA_BLOCK, (r + 1) * MOBA_BLOCK)
        kmean_ref[0, r:r + 1, :] = jnp.mean(acc[rows, BRANCH_WIDTH:2 * BRANCH_WIDTH],
                                            axis=0, keepdims=True)
        vt_ref[r] = jnp.transpose(acc[rows, 2 * BRANCH_WIDTH:3 * BRANCH_WIDTH]).astype(BF16)


def _qkv_proj(h, w, b):
    tm = QKV_TM
    n = 4 * BRANCH_WIDTH
    blocks_per_tile = tm // MOBA_BLOCK
    return pl.pallas_call(
        _qkv_kernel,
        grid=(M_ROWS // tm,),
        in_specs=[
            pl.BlockSpec((tm, D_MODEL), lambda i: (i, 0)),
            pl.BlockSpec((D_MODEL, n), lambda i: (0, 0), pipeline_mode=pl.Buffered(1)),
            pl.BlockSpec((1, n), lambda i: (0, 0)),
        ],
        out_specs=[
            pl.BlockSpec((tm, n), lambda i: (i, 0)),
            pl.BlockSpec((1, blocks_per_tile, BRANCH_WIDTH), lambda i: (i, 0, 0)),
            pl.BlockSpec((blocks_per_tile, BRANCH_WIDTH, MOBA_BLOCK), lambda i: (i, 0, 0)),
        ],
        out_shape=[
            jax.ShapeDtypeStruct((M_ROWS, n), BF16),
            jax.ShapeDtypeStruct((M_ROWS // tm, blocks_per_tile, BRANCH_WIDTH), F32),
            jax.ShapeDtypeStruct((M_ROWS // MOBA_BLOCK, BRANCH_WIDTH, MOBA_BLOCK), BF16),
        ],
        compiler_params=_params("arbitrary"),
        name="qkv_proj",
    )(h, w, b)


def _uvg_kernel(h_ref, w_ref, b_ref, o_ref):
    j = pl.program_id(1)

    @pl.when(j == 0)
    def _():
        o_ref[...] = _gelu_tanh(_dot(h_ref[...], w_ref[...]) + b_ref[...]).astype(BF16)

    @pl.when(j > 0)
    def _():
        o_ref[...] = _sigmoid2(_dot(h_ref[...], w_ref[...]) + b_ref[...]).astype(BF16)


def _uvg_proj(h, w, b):
    tm, tn = UVG_TM, UVG_TN
    n = 2 * BRANCH_WIDTH + 3 * D_MODEL
    return pl.pallas_call(
        _uvg_kernel,
        grid=(M_ROWS // tm, n // tn),
        in_specs=[
            pl.BlockSpec((tm, D_MODEL), lambda i, j: (i, 0)),
            pl.BlockSpec((D_MODEL, tn), lambda i, j: (0, j)),
            pl.BlockSpec((1, tn), lambda i, j: (0, j)),
        ],
        out_specs=pl.BlockSpec((tm, tn), lambda i, j: (i, j)),
        out_shape=jax.ShapeDtypeStruct((M_ROWS, n), BF16),
        compiler_params=_params("arbitrary", "arbitrary"),
        name="uvg_proj",
    )(h, w, b)


def _branch_kernel(u_ref, v_ref, z_ref, zprev_ref, ng_ref, sw_ref, sb_ref, pw_ref, ps_ref,
                   ysgu_ref, ypool_ref):
    i = pl.program_id(0)
    tr = BRANCH_TR
    first = (i % (SEQ // tr)) == 0

    v = v_ref[...].astype(F32)
    mu = jnp.mean(v, axis=-1, keepdims=True)
    vc = v - mu
    var = jnp.mean(vc * vc, axis=-1, keepdims=True)
    vn = (vc * lax.rsqrt(var + EPS) * ng_ref[...]).astype(BF16)

    row = lax.broadcasted_iota(jnp.int32, (SGU_CHUNK, SGU_CHUNK), 0)
    col = lax.broadcasted_iota(jnp.int32, (SGU_CHUNK, SGU_CHUNK), 1)
    causal = row >= col
    wm = [jnp.where(causal, sw_ref[g], 0.0).astype(BF16) for g in range(SGU_GROUPS)]
    lane = lax.broadcasted_iota(jnp.int32, (SGU_CHUNK, LANES), 1)
    low_half = lane < SGU_GROUP_DIM
    zero = jnp.zeros((SGU_CHUNK, LANES), BF16)

    for c in range(tr // SGU_CHUNK):
        rows = slice(c * SGU_CHUNK, (c + 1) * SGU_CHUNK)
        for pr in range(SGU_GROUPS // 2):
            cols = slice(pr * LANES, (pr + 1) * LANES)
            blk = vn[rows, cols]
            mixed = (_dot(wm[2 * pr], jnp.where(low_half, blk, zero))
                     + _dot(wm[2 * pr + 1], jnp.where(low_half, zero, blk))
                     + sb_ref[:, cols])
            ysgu_ref[rows, cols] = (u_ref[rows, cols].astype(F32) * mixed).astype(BF16)

    z = z_ref[...].astype(F32)
    zprev = jnp.where(first, 0.0, zprev_ref[...].astype(F32))
    pos = (i % (SEQ // tr)) * tr + lax.broadcasted_iota(jnp.int32, (tr, POOL_GROUP_DIM), 0)
    for gi, w in enumerate(POOL_WINDOWS):
        cols = slice(gi * POOL_GROUP_DIM, (gi + 1) * POOL_GROUP_DIM)
        zg = z[:, cols]
        cur = jnp.concatenate([zprev[:, cols], zg], axis=0)
        d = 1
        while d < w:
            cur = cur[d:] + cur[:-d]
            d *= 2
        win_sum = cur[cur.shape[0] - tr:]
        count = jnp.minimum(pos + 1, w).astype(F32)
        pooled = win_sum / count - zg
        y = _dot(pooled.astype(BF16), pw_ref[gi].astype(BF16)) * ps_ref[:, cols]
        ypool_ref[:, cols] = y.astype(BF16)


def _branches(uvg, qkvz, sgu_norm_g, sgu_w, sgu_bias_full, pool_w, pool_scale):
    tr = BRANCH_TR
    halo_blocks = tr // POOL_HALO
    return pl.pallas_call(
        _branch_kernel,
        grid=(M_ROWS // tr,),
        in_specs=[
            pl.BlockSpec((tr, BRANCH_WIDTH), lambda i: (i, 0)),
            pl.BlockSpec((tr, BRANCH_WIDTH), lambda i: (i, 1)),
            pl.BlockSpec((tr, BRANCH_WIDTH), lambda i: (i, 3)),
            pl.BlockSpec((POOL_HALO, BRANCH_WIDTH),
                         lambda i: (jnp.maximum(i * halo_blocks - 1, 0), 3)),
            pl.BlockSpec((1, BRANCH_WIDTH), lambda i: (0, 0)),
            pl.BlockSpec((SGU_GROUPS, SGU_CHUNK, SGU_CHUNK), lambda i: (0, 0, 0)),
            pl.BlockSpec((SGU_CHUNK, BRANCH_WIDTH), lambda i: (0, 0)),
            pl.BlockSpec((POOL_GROUPS, POOL_GROUP_DIM, POOL_GROUP_DIM), lambda i: (0, 0, 0)),
            pl.BlockSpec((1, BRANCH_WIDTH), lambda i: (0, 0)),
        ],
        out_specs=[
            pl.BlockSpec((tr, BRANCH_WIDTH), lambda i: (i, 0)),
            pl.BlockSpec((tr, BRANCH_WIDTH), lambda i: (i, 0)),
        ],
        out_shape=[
            jax.ShapeDtypeStruct((M_ROWS, BRANCH_WIDTH), BF16),
            jax.ShapeDtypeStruct((M_ROWS, BRANCH_WIDTH), BF16),
        ],
        compiler_params=_params("arbitrary"),
        name="sgu_pool",
    )(uvg, uvg, qkvz, qkvz, sgu_norm_g, sgu_w, sgu_bias_full, pool_w, pool_scale)


def _attn_kernel(q_ref, k_ref, vt_ref, km_ref, o_ref,
                 qs_ref, bias_ref, pick_ref, m_ref, acc_ref, s_ref):
    own = pl.program_id(1)
    tq, blk, hd = ATTN_TQ, MOBA_BLOCK, MOBA_HEAD_DIM
    own_f = own.astype(F32)

    key_row = lax.broadcasted_iota(jnp.int32, (blk, tq), 0)
    causal = jnp.where(key_row > lax.broadcasted_iota(jnp.int32, (blk, tq), 1), NEG, 0.0)
    key_row_f = key_row.astype(F32)
    blk_id = lax.broadcasted_iota(jnp.int32, (N_KV_BLOCKS, tq), 0)
    blk_id_f = blk_id.astype(F32)
    past = blk_id < own
    head_rows = lax.broadcasted_iota(jnp.int32, (LANES, tq), 0) < hd
    ones_rows = jnp.ones((ONES_ROWS, blk), BF16)

    q_t = jnp.transpose(q_ref[...].astype(F32))
    k_own = pl.multiple_of(own * blk, blk)

    def values(j, h):
        return jnp.concatenate([vt_ref[j, h * hd:(h + 1) * hd, :], ones_rows], axis=0)

    for pr in range(MOBA_HEADS // 2):
        cols = slice(pr * LANES, (pr + 1) * LANES)
        q_pair = q_t[pr * LANES:(pr + 1) * LANES]
        km = km_ref[0, :, cols]
        km_hi = km.astype(BF16)
        km_lo = (km - km_hi.astype(F32)).astype(BF16)
        for hh in range(2):
            h = 2 * pr + hh
            slope = 2.0 ** -(h + 1)
            mine = head_rows if hh == 0 else ~head_rows
            qh = jnp.where(mine, q_pair, 0.0).astype(BF16)

            sc = _dot(km_hi, qh) + _dot(km_lo, qh)
            sc = jnp.where(past, sc, -jnp.inf)
            for r in range(MOBA_TOPK):
                best = jnp.max(sc, axis=0, keepdims=True)
                idx = jnp.min(jnp.where(sc == best, blk_id_f, 1e9), axis=0, keepdims=True)
                pick_ref[h, r] = jnp.where(best > -jnp.inf, idx, -1.0)
                sc = jnp.where(blk_id_f == idx, -jnp.inf, sc)

            qs = jnp.where(mine, q_pair * (hd ** -0.5 * LOG2E), 0.0).astype(BF16)
            qs_ref[h] = qs
            bias = (slope * LOG2E) * key_row_f
            bias_ref[h] = bias

            s = _dot(k_ref[pl.ds(k_own, blk), cols], qs) + bias + causal
            m = jnp.max(s, axis=0, keepdims=True)
            m_ref[h] = m
            acc_ref[h] = _dot(values(own, h), jnp.exp2(s - m).astype(BF16))

    def scores(j, h, slot):
        start = pl.multiple_of(j * blk, blk)
        pr = h // 2
        kj = k_ref[pl.ds(start, blk), pr * LANES:(pr + 1) * LANES]
        s_ref[slot, h] = _dot(kj, qs_ref[h]) + bias_ref[h]

    def absorb(j, h, slot):
        jf = j.astype(F32)
        slope = 2.0 ** -(h + 1)
        chosen = (pick_ref[h, 0] == jf) | (pick_ref[h, 1] == jf) | (pick_ref[h, 2] == jf)
        shift = jnp.where(chosen, (-slope * blk * LOG2E) * (own_f - jf), NEG)
        m = m_ref[h]
        m_new = jnp.maximum(m, jnp.max(s_ref[slot, h], axis=0, keepdims=True) + shift)
        p = jnp.exp2(s_ref[slot, h] - (m_new - shift))
        m_ref[h] = m_new
        acc_ref[h] = jnp.exp2(m - m_new) * acc_ref[h] + _dot(values(j, h), p.astype(BF16))

    for h in range(MOBA_HEADS):
        scores(0, h, 0)

    def body(i, carry):
        a = 2 * i
        b = jnp.minimum(a + 1, own)
        c = jnp.minimum(a + 2, own)
        for h in range(MOBA_HEADS):
            scores(b, h, 1)
            absorb(a, h, 0)
        for h in range(MOBA_HEADS):
            scores(c, h, 0)
            absorb(b, h, 1)
        return carry

    lax.fori_loop(0, (own + 1) // 2, body, 0)

    o_t = jnp.concatenate([acc_ref[h, :hd] / acc_ref[h, hd:hd + 1] for h in range(MOBA_HEADS)],
                          axis=0)
    o_ref[...] = jnp.transpose(o_t).astype(BF16)


def _attention(qkvz, kmean, v_t):
    tq = ATTN_TQ
    q_tiles = SEQ // tq
    resident = functools.partial(pl.BlockSpec, pipeline_mode=pl.Buffered(1))
    return pl.pallas_call(
        _attn_kernel,
        grid=(BATCH, q_tiles),
        in_specs=[
            pl.BlockSpec((tq, BRANCH_WIDTH), lambda b, i: (b * q_tiles + i, 0)),
            resident((SEQ, BRANCH_WIDTH), lambda b, i: (b, 1)),
            resident((N_KV_BLOCKS, BRANCH_WIDTH, MOBA_BLOCK), lambda b, i: (b, 0, 0)),
            pl.BlockSpec((1, N_KV_BLOCKS, BRANCH_WIDTH), lambda b, i: (b, 0, 0)),
        ],
        out_specs=pl.BlockSpec((tq, BRANCH_WIDTH), lambda b, i: (b * q_tiles + i, 0)),
        out_shape=jax.ShapeDtypeStruct((M_ROWS, BRANCH_WIDTH), BF16),
        scratch_shapes=[
            pltpu.VMEM((MOBA_HEADS, LANES, tq), BF16),
            pltpu.VMEM((MOBA_HEADS, MOBA_BLOCK, tq), F32),
            pltpu.VMEM((MOBA_HEADS, MOBA_TOPK, 1, tq), F32),
            pltpu.VMEM((MOBA_HEADS, 1, tq), F32),
            pltpu.VMEM((MOBA_HEADS, MOBA_HEAD_DIM + ONES_ROWS, tq), F32),
            pltpu.VMEM((2, MOBA_HEADS, MOBA_BLOCK, tq), F32),
        ],
        compiler_params=_params("arbitrary", "arbitrary"),
        name="moba_attention",
    )(qkvz, qkvz, v_t, kmean)


def _merge_kernel(ys_ref, ym_ref, yp_ref, gs0, gs1, gm0, gm1, gp0, gp1,
                  ws_ref, wm_ref, wp_ref, wo_ref, x_ref, g_ref, gt_ref, g2_ref, sc2_ref, sh2_ref,
                  o_ref, h_ref):
    ys, ym, yp = ys_ref[...], ym_ref[...], yp_ref[...]
    halves = []
    half_w = D_MODEL // 2
    for hf, (gs, gm, gp) in enumerate(((gs0, gm0, gp0), (gs1, gm1, gp1))):
        cols = slice(hf * half_w, (hf + 1) * half_w)
        merged = (gs[...].astype(F32) * _dot(ys, ws_ref[:, cols])
                  + gm[...].astype(F32) * _dot(ym, wm_ref[:, cols])
                  + gp[...].astype(F32) * _dot(yp, wp_ref[:, cols]))
        halves.append(merged.astype(BF16))
    merged = jnp.concatenate(halves, axis=1)
    y = _dot(merged, wo_ref[...])
    x_new = x_ref[...] + _rms_norm(y, g_ref[...] * gt_ref[0])
    o_ref[...] = x_new
    h_ref[...] = _modulated_norm(x_new, g2_ref[...], sc2_ref[0], sh2_ref[0])


def _merge(ysgu, ymoba, ypool, uvg, w_sgu_out, w_moba_out, w_pool_out, w_out, x, g, gate,
           g_next, scale_next, shift_next):
    tm = MERGE_TM
    tiles_per_seq = SEQ // tm
    half_w = D_MODEL // 2
    gate_col0 = 2 * BRANCH_WIDTH // half_w
    resident = functools.partial(pl.BlockSpec, pipeline_mode=pl.Buffered(1))
    branch = pl.BlockSpec((tm, BRANCH_WIDTH), lambda i: (i, 0))
    gate_specs = [pl.BlockSpec((tm, half_w), functools.partial(lambda i, c: (i, c), c=gate_col0 + c))
                  for c in range(6)]
    w_branch = resident((BRANCH_WIDTH, D_MODEL), lambda i: (0, 0))
    rows = pl.BlockSpec((tm, D_MODEL), lambda i: (i, 0))
    vec = pl.BlockSpec((1, D_MODEL), lambda i: (0, 0))
    per_batch = pl.BlockSpec((1, 1, D_MODEL), lambda i: (i // tiles_per_seq, 0, 0))
    return pl.pallas_call(
        _merge_kernel,
        grid=(M_ROWS // tm,),
        in_specs=[branch, branch, branch, *gate_specs, w_branch, w_branch, w_branch,
                  resident((D_MODEL, D_MODEL), lambda i: (0, 0)),
                  rows, vec, per_batch, vec, per_batch, per_batch],
        out_specs=[rows, rows],
        out_shape=[jax.ShapeDtypeStruct((M_ROWS, D_MODEL), F32),
                   jax.ShapeDtypeStruct((M_ROWS, D_MODEL), BF16)],
        compiler_params=_params("arbitrary"),
        name="merge_outproj",
    )(ysgu, ymoba, ypool, uvg, uvg, uvg, uvg, uvg, uvg, w_sgu_out, w_moba_out, w_pool_out, w_out,
      x, g, gate, g_next, scale_next, shift_next)


def _up_kernel(h_ref, hh_ref, wg_ref, wv_ref, cg_ref, cv_ref, bg_ref, bv_ref, a_ref,
               zg_ref, zv_ref):
    tm = UP_TM
    first_in_seq = (pl.program_id(0) % (SEQ // tm)) == 0
    h = h_ref[...]
    h_prev = jnp.where(first_in_seq, jnp.zeros_like(hh_ref), hh_ref[...])

    def conv(c0, w_ref, z_ref, cw_ref, cb_ref):
        w = w_ref[:, c0:c0 + UP_COLS]
        z = _dot(h, w)
        z_prev = _dot(h_prev, w)[CONV_HALO - 8:]
        taps = []
        for s in range(UP_COLS // LANES):
            zc = slice(s * LANES, (s + 1) * LANES)
            cols = slice(c0 + s * LANES, c0 + (s + 1) * LANES)
            slab = cols.start // LANES
            z_ref[slab, 8:, :] = z[:, zc]
            z_ref[slab, :8, :] = z_prev[:, zc]
            taps.append(cw_ref[0:1, cols] * z_ref[slab, 6:6 + tm, :]
                        + cw_ref[1:2, cols] * z_ref[slab, 7:7 + tm, :]
                        + cw_ref[2:3, cols] * z[:, zc] + cb_ref[:, cols])
        return jnp.concatenate(taps, axis=1)

    for c0 in range(0, UP_TN, UP_COLS):
        gate = conv(c0, wg_ref, zg_ref, cg_ref, bg_ref)
        val = conv(c0, wv_ref, zv_ref, cv_ref, bv_ref)
        a_ref[:, c0:c0 + UP_COLS] = (_gelu_tanh(gate) * val).astype(BF16)


def _ffn_up(h, w_up, w_conv, b_conv):
    tm, tn, halo = UP_TM, UP_TN, CONV_HALO
    nj = D_FF // tn
    halo_blocks = tm // halo
    return pl.pallas_call(
        _up_kernel,
        grid=(M_ROWS // tm, nj),
        in_specs=[
            pl.BlockSpec((tm, D_MODEL), lambda i, j: (i, 0)),
            pl.BlockSpec((halo, D_MODEL), lambda i, j: (jnp.maximum(i * halo_blocks - 1, 0), 0)),
            pl.BlockSpec((D_MODEL, tn), lambda i, j: (0, j)),
            pl.BlockSpec((D_MODEL, tn), lambda i, j: (0, nj + j)),
            pl.BlockSpec((3, tn), lambda i, j: (0, j)),
            pl.BlockSpec((3, tn), lambda i, j: (0, nj + j)),
            pl.BlockSpec((1, tn), lambda i, j: (0, j)),
            pl.BlockSpec((1, tn), lambda i, j: (0, nj + j)),
        ],
        out_specs=pl.BlockSpec((tm, tn), lambda i, j: (i, j)),
        out_shape=jax.ShapeDtypeStruct((M_ROWS, D_FF), BF16),
        scratch_shapes=[pltpu.VMEM((tn // LANES, tm + 8, LANES), F32) for _ in range(2)],
        compiler_params=_params("arbitrary", "arbitrary"),
        name="ffn_up",
    )(h, h, w_up, w_up, w_conv, w_conv, b_conv, b_conv)


def _down_kernel(emit_h, a_ref, w_ref, x_ref, g_ref, gt_ref, *rest):
    if emit_h:
        g2_ref, sc2_ref, sh2_ref, o_ref, h_ref, acc_ref = rest
    else:
        o_ref, acc_ref = rest
    k = pl.program_id(1)
    last = pl.num_programs(1) - 1

    @pl.when(k == 0)
    def _():
        acc_ref[...] = _dot(a_ref[...], w_ref[...])

    @pl.when((k > 0) & (k < last))
    def _():
        acc_ref[...] += _dot(a_ref[...], w_ref[...])

    @pl.when(k == last)
    def _():
        y = acc_ref[...] + _dot(a_ref[...], w_ref[...])
        x_new = x_ref[...] + _rms_norm(y, g_ref[...] * gt_ref[0])
        o_ref[...] = x_new
        if emit_h:
            h_ref[...] = _modulated_norm(x_new, g2_ref[...], sc2_ref[0], sh2_ref[0])


def _ffn_down(a, w_down, x, g, gate, next_norm=None):
    tm, tk = DOWN_TM, DOWN_TK
    tiles_per_seq = SEQ // tm
    rows = pl.BlockSpec((tm, D_MODEL), lambda i, k: (i, 0))
    vec = pl.BlockSpec((1, D_MODEL), lambda i, k: (0, 0))
    per_batch = pl.BlockSpec((1, 1, D_MODEL), lambda i, k: (i // tiles_per_seq, 0, 0))
    emit_h = next_norm is not None
    in_specs = [pl.BlockSpec((tm, tk), lambda i, k: (i, k)),
                pl.BlockSpec((tk, D_MODEL), lambda i, k: (k, 0)),
                rows, vec, per_batch]
    out_specs = [rows]
    out_shape = [jax.ShapeDtypeStruct((M_ROWS, D_MODEL), F32)]
    args = [a, w_down, x, g, gate]
    if emit_h:
        in_specs += [vec, per_batch, per_batch]
        out_specs.append(rows)
        out_shape.append(jax.ShapeDtypeStruct((M_ROWS, D_MODEL), BF16))
        args += list(next_norm)
    return pl.pallas_call(
        functools.partial(_down_kernel, emit_h),
        grid=(M_ROWS // tm, D_FF // tk),
        in_specs=in_specs,
        out_specs=out_specs,
        out_shape=out_shape,
        scratch_shapes=[pltpu.VMEM((tm, D_MODEL), F32)],
        compiler_params=_params("arbitrary", "arbitrary"),
        name="ffn_down",
    )(*args)


def kernel(x, c, g_pre_mix, g_post_mix, g_pre_ffn, g_post_ffn, w_ada, b_ada, w_in, b_in,
           sgu_norm_g, sgu_w, sgu_b, pool_w, pool_scale, w_sgu_out, w_moba_out, w_pool_out,
           w_out, w_up, w_conv, b_conv, w_down):
    c_pad = jnp.pad(c, ((0, 8 - BATCH), (0, 0)))
    mod = _modulation(c_pad, w_ada, b_ada)[:, :BATCH]
    mod = mod.reshape(DEPTH, BATCH, N_MOD, 1, D_MODEL)
    shift1, scale1, gate1, shift2, scale2, gate2 = [mod[:, :, n] for n in range(N_MOD)]
    row = lambda v, l: v[l].reshape(1, -1)
    qkv_cols = slice(2 * BRANCH_WIDTH, 6 * BRANCH_WIDTH)

    xf = x.reshape(M_ROWS, D_MODEL)
    h = _prenorm(xf, row(g_pre_mix, 0), scale1[0], shift1[0])
    for l in range(DEPTH):
        w_in_l = w_in[l].astype(BF16)
        w_uvg = jnp.concatenate([w_in_l[:, :2 * BRANCH_WIDTH], w_in_l[:, 6 * BRANCH_WIDTH:]], axis=1)
        b_uvg = jnp.concatenate([b_in[l, :2 * BRANCH_WIDTH], b_in[l, 6 * BRANCH_WIDTH:]])
        uvg = _uvg_proj(h, w_uvg, b_uvg.reshape(1, -1))
        qkvz, kmean, v_t = _qkv_proj(h, w_in_l[:, qkv_cols], b_in[l, qkv_cols].reshape(1, -1))
        kmean = kmean.reshape(BATCH, N_KV_BLOCKS, BRANCH_WIDTH)

        sgu_bias_full = jnp.repeat(sgu_b[l].T, SGU_GROUP_DIM, axis=1)
        ysgu, ypool = _branches(uvg, qkvz, row(sgu_norm_g, l), sgu_w[l], sgu_bias_full, pool_w[l],
                                row(pool_scale, l))
        ymoba = _attention(qkvz, kmean, v_t)
        xf, h = _merge(ysgu, ymoba, ypool, uvg, w_sgu_out[l].astype(BF16),
                       w_moba_out[l].astype(BF16), w_pool_out[l].astype(BF16),
                       w_out[l].astype(BF16), xf, row(g_post_mix, l), gate1[l],
                       row(g_pre_ffn, l), scale2[l], shift2[l])

        a = _ffn_up(h, w_up[l].astype(BF16), w_conv[l], row(b_conv, l))
        if l + 1 < DEPTH:
            xf, h = _ffn_down(a, w_down[l].astype(BF16), xf, row(g_post_ffn, l), gate2[l],
                              (row(g_pre_mix, l + 1), scale1[l + 1], shift1[l + 1]))
        else:
            (xf,) = _ffn_down(a, w_down[l].astype(BF16), xf, row(g_post_ffn, l), gate2[l])
    return xf.reshape(BATCH, SEQ, D_MODEL)
```

```python
import functools

import jax
import jax.numpy as jnp
from jax import lax
from jax.experimental import pallas as pl
from jax.experimental.pallas import tpu as pltpu

F32 = jnp.float32
BF16 = jnp.bfloat16

D_MODEL = 2048
BATCH = 4
SEQ = 8192
DEPTH = 2
EPS = 1e-6
BRANCH_WIDTH = 512
SGU_GROUPS = 8
SGU_GROUP_DIM = 64
SGU_CHUNK = 128
MOBA_HEADS = 8
MOBA_HEAD_DIM = 64
MOBA_BLOCK = 256
MOBA_TOPK = 3
POOL_WINDOWS = (2, 4, 8, 16)
POOL_GROUPS = 4
POOL_GROUP_DIM = 128
IN_WIDTH = 6 * BRANCH_WIDTH + 3 * D_MODEL
D_FF = 5632
N_MOD = 6
M_ROWS = BATCH * SEQ
N_KV_BLOCKS = SEQ // MOBA_BLOCK

VMEM_LIMIT_BYTES = 56 * 1024 * 1024
LANES = 128
NEG = -1e30
LOG2E = 1.4426950408889634
ONES_ROWS = 16

MOD_TN = 1024
PRENORM_TM = 512
QKV_TM = 1024
UVG_TM, UVG_TN = 2048, 1024
BRANCH_TR = 512
ATTN_TQ = MOBA_BLOCK
MERGE_TM = 512
MERGE_ROWS = 256
UP_TM, UP_TN = 1024, 512
CONV_HALO = 16
DOWN_TM = 256
DOWN_ROWS = 128
POOL_HALO = 16


GELU_C = 0.7978845608028654
GELU_A = 0.044715


def _sigmoid(x):
    return 1.0 / (1.0 + jnp.exp(-x))


def _sigmoid2(x):
    return 1.0 / (1.0 + jnp.exp2(-LOG2E * x))


def _gelu_tanh(x):
    k1 = -2.0 * GELU_C * LOG2E
    return x / (1.0 + jnp.exp2(x * (k1 + (k1 * GELU_A) * (x * x))))


def _rms_norm(x, g):
    ms = jnp.mean(x * x, axis=-1, keepdims=True)
    return (x * lax.rsqrt(ms + EPS)) * g


def _modulated_norm(x, g, scale, shift):
    return (_rms_norm(x, g * (1.0 + scale)) + shift).astype(BF16)


def _dot(a, b):
    return jnp.dot(a, b, preferred_element_type=F32)


def _dot_nt(a, b):
    return lax.dot_general(a, b, (((1,), (1,)), ((), ())), preferred_element_type=F32)


def _params(*sem):
    return pltpu.CompilerParams(dimension_semantics=sem, vmem_limit_bytes=VMEM_LIMIT_BYTES)


def _mod_kernel(c_ref, w_ref, b_ref, o_ref):
    c = c_ref[...]
    cond = (c * _sigmoid(c)).astype(BF16)
    o_ref[0] = _dot(cond, w_ref[0].astype(BF16)) + b_ref[0]


def _modulation(c_pad, w_ada, b_ada):
    rows = c_pad.shape[0]
    n = N_MOD * D_MODEL
    return pl.pallas_call(
        _mod_kernel,
        grid=(DEPTH, n // MOD_TN),
        in_specs=[
            pl.BlockSpec((rows, D_MODEL), lambda l, j: (0, 0)),
            pl.BlockSpec((1, D_MODEL, MOD_TN), lambda l, j: (l, 0, j)),
            pl.BlockSpec((1, 1, MOD_TN), lambda l, j: (l, 0, j)),
        ],
        out_specs=pl.BlockSpec((1, rows, MOD_TN), lambda l, j: (l, 0, j)),
        out_shape=jax.ShapeDtypeStruct((DEPTH, rows, n), F32),
        compiler_params=_params("arbitrary", "arbitrary"),
        name="adaln_mod",
    )(c_pad, w_ada, b_ada.reshape(DEPTH, 1, n))


def _prenorm_kernel(x_ref, g_ref, sc_ref, sh_ref, h_ref):
    h_ref[...] = _modulated_norm(x_ref[...], g_ref[...], sc_ref[0], sh_ref[0])


def _prenorm(x, g, scale, shift):
    tm = PRENORM_TM
    tiles_per_seq = SEQ // tm
    return pl.pallas_call(
        _prenorm_kernel,
        grid=(M_ROWS // tm,),
        in_specs=[
            pl.BlockSpec((tm, D_MODEL), lambda i: (i, 0)),
            pl.BlockSpec((1, D_MODEL), lambda i: (0, 0)),
            pl.BlockSpec((1, 1, D_MODEL), lambda i: (i // tiles_per_seq, 0, 0)),
            pl.BlockSpec((1, 1, D_MODEL), lambda i: (i // tiles_per_seq, 0, 0)),
        ],
        out_specs=pl.BlockSpec((tm, D_MODEL), lambda i: (i, 0)),
        out_shape=jax.ShapeDtypeStruct((M_ROWS, D_MODEL), BF16),
        compiler_params=_params("arbitrary"),
        name="prenorm",
    )(x, g, scale, shift)


def _qkv_kernel(h_ref, w_ref, b_ref, o_ref, kmean_ref, vt_ref):
    acc = _dot(h_ref[...], w_ref[...]) + b_ref[...]
    o_ref[...] = acc.astype(BF16)
    for r in range(QKV_TM // MOBA_BLOCK):
        rows = slice(r * MOBA_BLOCK, (r + 1) * MOBA_BLOCK)
        kmean_ref[0, r:r + 1, :] = jnp.mean(acc[rows, BRANCH_WIDTH:2 * BRANCH_WIDTH],
                                            axis=0, keepdims=True)
        vt_ref[r] = jnp.transpose(acc[rows, 2 * BRANCH_WIDTH:3 * BRANCH_WIDTH]).astype(BF16)


def _qkv_proj(h, w, b):
    tm = QKV_TM
    n = 4 * BRANCH_WIDTH
    blocks_per_tile = tm // MOBA_BLOCK
    return pl.pallas_call(
        _qkv_kernel,
        grid=(M_ROWS // tm,),
        in_specs=[
            pl.BlockSpec((tm, D_MODEL), lambda i: (i, 0)),
            pl.BlockSpec((D_MODEL, n), lambda i: (0, 0), pipeline_mode=pl.Buffered(1)),
            pl.BlockSpec((1, n), lambda i: (0, 0)),
        ],
        out_specs=[
            pl.BlockSpec((tm, n), lambda i: (i, 0)),
            pl.BlockSpec((1, blocks_per_tile, BRANCH_WIDTH), lambda i: (i, 0, 0)),
            pl.BlockSpec((blocks_per_tile, BRANCH_WIDTH, MOBA_BLOCK), lambda i: (i, 0, 0)),
        ],
        out_shape=[
            jax.ShapeDtypeStruct((M_ROWS, n), BF16),
            jax.ShapeDtypeStruct((M_ROWS // tm, blocks_per_tile, BRANCH_WIDTH), F32),
            jax.ShapeDtypeStruct((M_ROWS // MOBA_BLOCK, BRANCH_WIDTH, MOBA_BLOCK), BF16),
        ],
        compiler_params=_params("arbitrary"),
        name="qkv_proj",
    )(h, w, b)


def _uvg_kernel(h_ref, w_ref, b_ref, o_ref):
    j = pl.program_id(1)

    @pl.when(j == 0)
    def _():
        o_ref[...] = _gelu_tanh(_dot(h_ref[...], w_ref[...]) + b_ref[...]).astype(BF16)

    @pl.when(j > 0)
    def _():
        o_ref[...] = _sigmoid2(_dot(h_ref[...], w_ref[...]) + b_ref[...]).astype(BF16)


def _uvg_proj(h, w, b):
    tm, tn = UVG_TM, UVG_TN
    n = 2 * BRANCH_WIDTH + 3 * D_MODEL
    return pl.pallas_call(
        _uvg_kernel,
        grid=(M_ROWS // tm, n // tn),
        in_specs=[
            pl.BlockSpec((tm, D_MODEL), lambda i, j: (i, 0)),
            pl.BlockSpec((D_MODEL, tn), lambda i, j: (0, j)),
            pl.BlockSpec((1, tn), lambda i, j: (0, j)),
        ],
        out_specs=pl.BlockSpec((tm, tn), lambda i, j: (i, j)),
        out_shape=jax.ShapeDtypeStruct((M_ROWS, n), BF16),
        compiler_params=_params("arbitrary", "arbitrary"),
        name="uvg_proj",
    )(h, w, b)


def _branch_kernel(u_ref, v_ref, z_ref, zprev_ref, ng_ref, sw_ref, sb_ref, pw_ref, ps_ref,
                   ysgu_ref, ypool_ref):
    i = pl.program_id(0)
    tr = BRANCH_TR
    first = (i % (SEQ // tr)) == 0

    v = v_ref[...].astype(F32)
    mu = jnp.mean(v, axis=-1, keepdims=True)
    vc = v - mu
    var = jnp.mean(vc * vc, axis=-1, keepdims=True)
    vn = (vc * lax.rsqrt(var + EPS) * ng_ref[...]).astype(BF16)

    row = lax.broadcasted_iota(jnp.int32, (SGU_CHUNK, SGU_CHUNK), 0)
    col = lax.broadcasted_iota(jnp.int32, (SGU_CHUNK, SGU_CHUNK), 1)
    causal = row >= col
    wm = [jnp.where(causal, sw_ref[g], 0.0).astype(BF16) for g in range(SGU_GROUPS)]
    lane = lax.broadcasted_iota(jnp.int32, (SGU_CHUNK, LANES), 1)
    low_half = lane < SGU_GROUP_DIM
    zero = jnp.zeros((SGU_CHUNK, LANES), BF16)

    for c in range(tr // SGU_CHUNK):
        rows = slice(c * SGU_CHUNK, (c + 1) * SGU_CHUNK)
        for pr in range(SGU_GROUPS // 2):
            cols = slice(pr * LANES, (pr + 1) * LANES)
            blk = vn[rows, cols]
            mixed = (_dot(wm[2 * pr], jnp.where(low_half, blk, zero))
                     + _dot(wm[2 * pr + 1], jnp.where(low_half, zero, blk))
                     + sb_ref[:, cols])
            ysgu_ref[rows, cols] = (u_ref[rows, cols].astype(F32) * mixed).astype(BF16)

    z = z_ref[...].astype(F32)
    zprev = jnp.where(first, 0.0, zprev_ref[...].astype(F32))
    pos = (i % (SEQ // tr)) * tr + lax.broadcasted_iota(jnp.int32, (tr, POOL_GROUP_DIM), 0)
    for gi, w in enumerate(POOL_WINDOWS):
        cols = slice(gi * POOL_GROUP_DIM, (gi + 1) * POOL_GROUP_DIM)
        zg = z[:, cols]
        cur = jnp.concatenate([zprev[:, cols], zg], axis=0)
        d = 1
        while d < w:
            cur = cur[d:] + cur[:-d]
            d *= 2
        win_sum = cur[cur.shape[0] - tr:]
        count = jnp.minimum(pos + 1, w).astype(F32)
        pooled = win_sum / count - zg
        y = _dot(pooled.astype(BF16), pw_ref[gi].astype(BF16)) * ps_ref[:, cols]
        ypool_ref[:, cols] = y.astype(BF16)


def _branches(uvg, qkvz, sgu_norm_g, sgu_w, sgu_bias_full, pool_w, pool_scale):
    tr = BRANCH_TR
    halo_blocks = tr // POOL_HALO
    return pl.pallas_call(
        _branch_kernel,
        grid=(M_ROWS // tr,),
        in_specs=[
            pl.BlockSpec((tr, BRANCH_WIDTH), lambda i: (i, 0)),
            pl.BlockSpec((tr, BRANCH_WIDTH), lambda i: (i, 1)),
            pl.BlockSpec((tr, BRANCH_WIDTH), lambda i: (i, 3)),
            pl.BlockSpec((POOL_HALO, BRANCH_WIDTH),
                         lambda i: (jnp.maximum(i * halo_blocks - 1, 0), 3)),
            pl.BlockSpec((1, BRANCH_WIDTH), lambda i: (0, 0)),
            pl.BlockSpec((SGU_GROUPS, SGU_CHUNK, SGU_CHUNK), lambda i: (0, 0, 0)),
            pl.BlockSpec((SGU_CHUNK, BRANCH_WIDTH), lambda i: (0, 0)),
            pl.BlockSpec((POOL_GROUPS, POOL_GROUP_DIM, POOL_GROUP_DIM), lambda i: (0, 0, 0)),
            pl.BlockSpec((1, BRANCH_WIDTH), lambda i: (0, 0)),
        ],
        out_specs=[
            pl.BlockSpec((tr, BRANCH_WIDTH), lambda i: (i, 0)),
            pl.BlockSpec((tr, BRANCH_WIDTH), lambda i: (i, 0)),
        ],
        out_shape=[
            jax.ShapeDtypeStruct((M_ROWS, BRANCH_WIDTH), BF16),
            jax.ShapeDtypeStruct((M_ROWS, BRANCH_WIDTH), BF16),
        ],
        compiler_params=_params("arbitrary"),
        name="sgu_pool",
    )(uvg, uvg, qkvz, qkvz, sgu_norm_g, sgu_w, sgu_bias_full, pool_w, pool_scale)


def _attn_kernel(q_ref, k_ref, vt_ref, km_ref, o_ref,
                 qs_ref, bias_ref, pick_ref, m_ref, acc_ref, s_ref):
    own = pl.program_id(1)
    tq, blk, hd = ATTN_TQ, MOBA_BLOCK, MOBA_HEAD_DIM
    own_f = own.astype(F32)

    key_row = lax.broadcasted_iota(jnp.int32, (blk, tq), 0)
    causal = jnp.where(key_row > lax.broadcasted_iota(jnp.int32, (blk, tq), 1), NEG, 0.0)
    key_row_f = key_row.astype(F32)
    blk_id = lax.broadcasted_iota(jnp.int32, (N_KV_BLOCKS, tq), 0)
    blk_id_f = blk_id.astype(F32)
    past = blk_id < own
    head_rows = lax.broadcasted_iota(jnp.int32, (LANES, tq), 0) < hd
    ones_rows = jnp.ones((ONES_ROWS, blk), BF16)

    q_t = jnp.transpose(q_ref[...].astype(F32))
    k_own = pl.multiple_of(own * blk, blk)

    def values(j, h):
        return jnp.concatenate([vt_ref[j, h * hd:(h + 1) * hd, :], ones_rows], axis=0)

    for pr in range(MOBA_HEADS // 2):
        cols = slice(pr * LANES, (pr + 1) * LANES)
        q_pair = q_t[pr * LANES:(pr + 1) * LANES]
        km = km_ref[0, :, cols]
        km_hi = km.astype(BF16)
        km_lo = (km - km_hi.astype(F32)).astype(BF16)
        for hh in range(2):
            h = 2 * pr + hh
            slope = 2.0 ** -(h + 1)
            mine = head_rows if hh == 0 else ~head_rows
            qh = jnp.where(mine, q_pair, 0.0).astype(BF16)

            sc = _dot(km_hi, qh) + _dot(km_lo, qh)
            sc = jnp.where(past, sc, -jnp.inf)
            for r in range(MOBA_TOPK):
                best = jnp.max(sc, axis=0, keepdims=True)
                idx = jnp.min(jnp.where(sc == best, blk_id_f, 1e9), axis=0, keepdims=True)
                pick_ref[h, r] = jnp.where(best > -jnp.inf, idx, -1.0)
                sc = jnp.where(blk_id_f == idx, -jnp.inf, sc)

            qs = jnp.where(mine, q_pair * (hd ** -0.5 * LOG2E), 0.0).astype(BF16)
            qs_ref[h] = qs
            bias = (slope * LOG2E) * key_row_f
            bias_ref[h] = bias

            s = _dot(k_ref[pl.ds(k_own, blk), cols], qs) + bias + causal
            m = jnp.max(s, axis=0, keepdims=True)
            m_ref[h] = m
            acc_ref[h] = _dot(values(own, h), jnp.exp2(s - m).astype(BF16))

    def scores(j, h, slot):
        start = pl.multiple_of(j * blk, blk)
        pr = h // 2
        kj = k_ref[pl.ds(start, blk), pr * LANES:(pr + 1) * LANES]
        s_ref[slot, h] = _dot(kj, qs_ref[h]) + bias_ref[h]

    def absorb(j, h, slot):
        jf = j.astype(F32)
        slope = 2.0 ** -(h + 1)
        chosen = (pick_ref[h, 0] == jf) | (pick_ref[h, 1] == jf) | (pick_ref[h, 2] == jf)
        shift = jnp.where(chosen, (-slope * blk * LOG2E) * (own_f - jf), NEG)
        m = m_ref[h]
        m_new = jnp.maximum(m, jnp.max(s_ref[slot, h], axis=0, keepdims=True) + shift)
        p = jnp.exp2(s_ref[slot, h] - (m_new - shift))
        m_ref[h] = m_new
        acc_ref[h] = jnp.exp2(m - m_new) * acc_ref[h] + _dot(values(j, h), p.astype(BF16))

    for h in range(MOBA_HEADS):
        scores(0, h, 0)

    def body(i, carry):
        a = 2 * i
        b = jnp.minimum(a + 1, own)
        c = jnp.minimum(a + 2, own)
        for h in range(MOBA_HEADS):
            scores(b, h, 1)
            absorb(a, h, 0)
        for h in range(MOBA_HEADS):
            scores(c, h, 0)
            absorb(b, h, 1)
        return carry

    lax.fori_loop(0, (own + 1) // 2, body, 0)

    o_t = jnp.concatenate([acc_ref[h, :hd] / acc_ref[h, hd:hd + 1] for h in range(MOBA_HEADS)],
                          axis=0)
    o_ref[...] = jnp.transpose(o_t).astype(BF16)


def _attention(qkvz, kmean, v_t):
    tq = ATTN_TQ
    q_tiles = SEQ // tq
    return pl.pallas_call(
        _attn_kernel,
        grid=(BATCH, q_tiles),
        in_specs=[
            pl.BlockSpec((tq, BRANCH_WIDTH), lambda b, i: (b * q_tiles + i, 0)),
            pl.BlockSpec((SEQ, BRANCH_WIDTH), lambda b, i: (b, 1)),
            pl.BlockSpec((N_KV_BLOCKS, BRANCH_WIDTH, MOBA_BLOCK), lambda b, i: (b, 0, 0)),
            pl.BlockSpec((1, N_KV_BLOCKS, BRANCH_WIDTH), lambda b, i: (b, 0, 0)),
        ],
        out_specs=pl.BlockSpec((tq, BRANCH_WIDTH), lambda b, i: (b * q_tiles + i, 0)),
        out_shape=jax.ShapeDtypeStruct((M_ROWS, BRANCH_WIDTH), BF16),
        scratch_shapes=[
            pltpu.VMEM((MOBA_HEADS, LANES, tq), BF16),
            pltpu.VMEM((MOBA_HEADS, MOBA_BLOCK, tq), F32),
            pltpu.VMEM((MOBA_HEADS, MOBA_TOPK, 1, tq), F32),
            pltpu.VMEM((MOBA_HEADS, 1, tq), F32),
            pltpu.VMEM((MOBA_HEADS, MOBA_HEAD_DIM + ONES_ROWS, tq), F32),
            pltpu.VMEM((2, MOBA_HEADS, MOBA_BLOCK, tq), F32),
        ],
        compiler_params=_params("arbitrary", "arbitrary"),
        name="moba_attention",
    )(qkvz, qkvz, v_t, kmean)


def _merge_kernel(ys_ref, ym_ref, yp_ref, gs0, gs1, gm0, gm1, gp0, gp1,
                  ws_ref, wm_ref, wp_ref, wo_ref, x_ref, g_ref, gt_ref, g2_ref, sc2_ref, sh2_ref,
                  o_ref, h_ref):
    half_w = D_MODEL // 2
    for r0 in range(0, MERGE_TM, MERGE_ROWS):
        rows = slice(r0, r0 + MERGE_ROWS)
        ys, ym, yp = ys_ref[rows, :], ym_ref[rows, :], yp_ref[rows, :]
        halves = []
        for hf, (gs, gm, gp) in enumerate(((gs0, gm0, gp0), (gs1, gm1, gp1))):
            cols = slice(hf * half_w, (hf + 1) * half_w)
            merged = (gs[rows, :].astype(F32) * _dot(ys, ws_ref[:, cols])
                      + gm[rows, :].astype(F32) * _dot(ym, wm_ref[:, cols])
                      + gp[rows, :].astype(F32) * _dot(yp, wp_ref[:, cols]))
            halves.append(merged.astype(BF16))
        y = _dot(jnp.concatenate(halves, axis=1), wo_ref[...])
        x_new = x_ref[rows, :] + _rms_norm(y, g_ref[...] * gt_ref[0])
        o_ref[rows, :] = x_new
        h_ref[rows, :] = _modulated_norm(x_new, g2_ref[...], sc2_ref[0], sh2_ref[0])


def _merge(ysgu, ymoba, ypool, uvg, w_sgu_out, w_moba_out, w_pool_out, w_out, x, g, gate,
           g_next, scale_next, shift_next):
    tm = MERGE_TM
    tiles_per_seq = SEQ // tm
    half_w = D_MODEL // 2
    gate_col0 = 2 * BRANCH_WIDTH // half_w
    resident = functools.partial(pl.BlockSpec, pipeline_mode=pl.Buffered(1))
    branch = pl.BlockSpec((tm, BRANCH_WIDTH), lambda i: (i, 0))
    gate_specs = [pl.BlockSpec((tm, half_w), functools.partial(lambda i, c: (i, c), c=gate_col0 + c))
                  for c in range(6)]
    w_branch = resident((BRANCH_WIDTH, D_MODEL), lambda i: (0, 0))
    rows = pl.BlockSpec((tm, D_MODEL), lambda i: (i, 0))
    vec = pl.BlockSpec((1, D_MODEL), lambda i: (0, 0))
    per_batch = pl.BlockSpec((1, 1, D_MODEL), lambda i: (i // tiles_per_seq, 0, 0))
    return pl.pallas_call(
        _merge_kernel,
        grid=(M_ROWS // tm,),
        in_specs=[branch, branch, branch, *gate_specs, w_branch, w_branch, w_branch,
                  resident((D_MODEL, D_MODEL), lambda i: (0, 0)),
                  rows, vec, per_batch, vec, per_batch, per_batch],
        out_specs=[rows, rows],
        out_shape=[jax.ShapeDtypeStruct((M_ROWS, D_MODEL), F32),
                   jax.ShapeDtypeStruct((M_ROWS, D_MODEL), BF16)],
        compiler_params=_params("arbitrary"),
        name="merge_outproj",
    )(ysgu, ymoba, ypool, uvg, uvg, uvg, uvg, uvg, uvg, w_sgu_out, w_moba_out, w_pool_out, w_out,
      x, g, gate, g_next, scale_next, shift_next)


def _up_kernel(h_ref, hh_ref, wg_ref, wv_ref, cg_ref, cv_ref, bg_ref, bv_ref, a_ref,
               zg_ref, zv_ref):
    tm = UP_TM
    first_in_seq = (pl.program_id(0) % (SEQ // tm)) == 0
    h = h_ref[...]
    h_prev = jnp.where(first_in_seq, jnp.zeros_like(hh_ref), hh_ref[...])

    def conv(w_ref, z_ref, cw_ref, cb_ref):
        w = w_ref[...]
        z = _dot(h, w)
        z_prev = _dot(h_prev, w)[CONV_HALO - 8:]
        taps = []
        for s in range(UP_TN // LANES):
            cols = slice(s * LANES, (s + 1) * LANES)
            z_ref[s, 8:, :] = z[:, cols]
            z_ref[s, :8, :] = z_prev[:, cols]
            taps.append(cw_ref[0:1, cols] * z_ref[s, 6:6 + tm, :]
                        + cw_ref[1:2, cols] * z_ref[s, 7:7 + tm, :]
                        + cw_ref[2:3, cols] * z[:, cols] + cb_ref[:, cols])
        return jnp.concatenate(taps, axis=1)

    gate = conv(wg_ref, zg_ref, cg_ref, bg_ref)
    val = conv(wv_ref, zv_ref, cv_ref, bv_ref)
    a_ref[...] = (_gelu_tanh(gate) * val).astype(BF16)


def _ffn_up(h, w_up, w_conv, b_conv):
    tm, tn, halo = UP_TM, UP_TN, CONV_HALO
    nj = D_FF // tn
    halo_blocks = tm // halo
    return pl.pallas_call(
        _up_kernel,
        grid=(M_ROWS // tm, nj),
        in_specs=[
            pl.BlockSpec((tm, D_MODEL), lambda i, j: (i, 0)),
            pl.BlockSpec((halo, D_MODEL), lambda i, j: (jnp.maximum(i * halo_blocks - 1, 0), 0)),
            pl.BlockSpec((D_MODEL, tn), lambda i, j: (0, j)),
            pl.BlockSpec((D_MODEL, tn), lambda i, j: (0, nj + j)),
            pl.BlockSpec((3, tn), lambda i, j: (0, j)),
            pl.BlockSpec((3, tn), lambda i, j: (0, nj + j)),
            pl.BlockSpec((1, tn), lambda i, j: (0, j)),
            pl.BlockSpec((1, tn), lambda i, j: (0, nj + j)),
        ],
        out_specs=pl.BlockSpec((tm, tn), lambda i, j: (i, j)),
        out_shape=jax.ShapeDtypeStruct((M_ROWS, D_FF), BF16),
        scratch_shapes=[pltpu.VMEM((tn // LANES, tm + 8, LANES), F32) for _ in range(2)],
        compiler_params=_params("arbitrary", "arbitrary"),
        name="ffn_up",
    )(h, h, w_up, w_up, w_conv, w_conv, b_conv, b_conv)


def _down_kernel(emit_h, a_ref, w_ref, x_ref, g_ref, gt_ref, *rest):
    if emit_h:
        g2_ref, sc2_ref, sh2_ref, o_ref, h_ref = rest
    else:
        (o_ref,) = rest
    for r0 in range(0, DOWN_TM, DOWN_ROWS):
        rows = slice(r0, r0 + DOWN_ROWS)
        y = _dot(a_ref[rows, :], w_ref[...])
        x_new = x_ref[rows, :] + _rms_norm(y, g_ref[...] * gt_ref[0])
        o_ref[rows, :] = x_new
        if emit_h:
            h_ref[rows, :] = _modulated_norm(x_new, g2_ref[...], sc2_ref[0], sh2_ref[0])


def _ffn_down(a, w_down, x, g, gate, next_norm=None):
    tm = DOWN_TM
    tiles_per_seq = SEQ // tm
    rows = pl.BlockSpec((tm, D_MODEL), lambda i: (i, 0))
    vec = pl.BlockSpec((1, D_MODEL), lambda i: (0, 0))
    per_batch = pl.BlockSpec((1, 1, D_MODEL), lambda i: (i // tiles_per_seq, 0, 0))
    emit_h = next_norm is not None
    in_specs = [pl.BlockSpec((tm, D_FF), lambda i: (i, 0)),
                pl.BlockSpec((D_FF, D_MODEL), lambda i: (0, 0), pipeline_mode=pl.Buffered(1)),
                rows, vec, per_batch]
    out_specs = [rows]
    out_shape = [jax.ShapeDtypeStruct((M_ROWS, D_MODEL), F32)]
    args = [a, w_down, x, g, gate]
    if emit_h:
        in_specs += [vec, per_batch, per_batch]
        out_specs.append(rows)
        out_shape.append(jax.ShapeDtypeStruct((M_ROWS, D_MODEL), BF16))
        args += list(next_norm)
    return pl.pallas_call(
        functools.partial(_down_kernel, emit_h),
        grid=(M_ROWS // tm,),
        in_specs=in_specs,
        out_specs=out_specs,
        out_shape=out_shape,
        compiler_params=_params("arbitrary"),
        name="ffn_down",
    )(*args)


def kernel(x, c, g_pre_mix, g_post_mix, g_pre_ffn, g_post_ffn, w_ada, b_ada, w_in, b_in,
           sgu_norm_g, sgu_w, sgu_b, pool_w, pool_scale, w_sgu_out, w_moba_out, w_pool_out,
           w_out, w_up, w_conv, b_conv, w_down):
    c_pad = jnp.pad(c, ((0, 8 - BATCH), (0, 0)))
    mod = _modulation(c_pad, w_ada, b_ada)[:, :BATCH]
    mod = mod.reshape(DEPTH, BATCH, N_MOD, 1, D_MODEL)
    shift1, scale1, gate1, shift2, scale2, gate2 = [mod[:, :, n] for n in range(N_MOD)]
    row = lambda v, l: v[l].reshape(1, -1)
    qkv_cols = slice(2 * BRANCH_WIDTH, 6 * BRANCH_WIDTH)

    xf = x.reshape(M_ROWS, D_MODEL)
    h = _prenorm(xf, row(g_pre_mix, 0), scale1[0], shift1[0])
    for l in range(DEPTH):
        w_in_l = w_in[l].astype(BF16)
        w_uvg = jnp.concatenate([w_in_l[:, :2 * BRANCH_WIDTH], w_in_l[:, 6 * BRANCH_WIDTH:]], axis=1)
        b_uvg = jnp.concatenate([b_in[l, :2 * BRANCH_WIDTH], b_in[l, 6 * BRANCH_WIDTH:]])
        uvg = _uvg_proj(h, w_uvg, b_uvg.reshape(1, -1))
        qkvz, kmean, v_t = _qkv_proj(h, w_in_l[:, qkv_cols], b_in[l, qkv_cols].reshape(1, -1))
        kmean = kmean.reshape(BATCH, N_KV_BLOCKS, BRANCH_WIDTH)

        sgu_bias_full = jnp.repeat(sgu_b[l].T, SGU_GROUP_DIM, axis=1)
        ysgu, ypool = _branches(uvg, qkvz, row(sgu_norm_g, l), sgu_w[l], sgu_bias_full, pool_w[l],
                                row(pool_scale, l))
        ymoba = _attention(qkvz, kmean, v_t)
        xf, h = _merge(ysgu, ymoba, ypool, uvg, w_sgu_out[l].astype(BF16),
                       w_moba_out[l].astype(BF16), w_pool_out[l].astype(BF16),
                       w_out[l].astype(BF16), xf, row(g_post_mix, l), gate1[l],
                       row(g_pre_ffn, l), scale2[l], shift2[l])

        a = _ffn_up(h, w_up[l].astype(BF16), w_conv[l], row(b_conv, l))
        if l + 1 < DEPTH:
            xf, h = _ffn_down(a, w_down[l].astype(BF16), xf, row(g_post_ffn, l), gate2[l],
                              (row(g_pre_mix, l + 1), scale1[l + 1], shift1[l + 1]))
        else:
            (xf,) = _ffn_down(a, w_down[l].astype(BF16), xf, row(g_post_ffn, l), gate2[l])
    return xf.reshape(BATCH, SEQ, D_MODEL)
```

```python
import functools

import jax
import jax.numpy as jnp
from jax import lax
from jax.experimental import pallas as pl
from jax.experimental.pallas import tpu as pltpu

F32 = jnp.float32
BF16 = jnp.bfloat16

D_MODEL = 2048
BATCH = 4
SEQ = 8192
DEPTH = 2
EPS = 1e-6
BRANCH_WIDTH = 512
SGU_GROUPS = 8
SGU_GROUP_DIM = 64
SGU_CHUNK = 128
MOBA_HEADS = 8
MOBA_HEAD_DIM = 64
MOBA_BLOCK = 256
MOBA_TOPK = 3
POOL_WINDOWS = (2, 4, 8, 16)
POOL_GROUPS = 4
POOL_GROUP_DIM = 128
IN_WIDTH = 6 * BRANCH_WIDTH + 3 * D_MODEL
D_FF = 5632
N_MOD = 6
M_ROWS = BATCH * SEQ
N_KV_BLOCKS = SEQ // MOBA_BLOCK

VMEM_LIMIT_BYTES = 56 * 1024 * 1024
LANES = 128
NEG = -1e30
LOG2E = 1.4426950408889634
ONES_ROWS = 16

MOD_TN = 1024
PRENORM_TM = 512
QKV_TM = 1024
UVG_TM, UVG_TN = 2048, 1024
UVG_ROWS = 512
BRANCH_TR = 512
ATTN_TQ = MOBA_BLOCK
MERGE_TM = 512
MERGE_ROWS = 256
UP_TM, UP_TN = 1024, 512
CONV_HALO = 16
DOWN_TM = 256
DOWN_ROWS = 128
POOL_HALO = 16


GELU_C = 0.7978845608028654
GELU_A = 0.044715


def _sigmoid(x):
    return 1.0 / (1.0 + jnp.exp(-x))


def _sigmoid2(x):
    return 1.0 / (1.0 + jnp.exp2(-LOG2E * x))


def _gelu_tanh(x):
    k1 = -2.0 * GELU_C * LOG2E
    return x / (1.0 + jnp.exp2(x * (k1 + (k1 * GELU_A) * (x * x))))


def _rms_norm(x, g):
    ms = jnp.mean(x * x, axis=-1, keepdims=True)
    return (x * lax.rsqrt(ms + EPS)) * g


def _modulated_norm(x, g, scale, shift):
    return (_rms_norm(x, g * (1.0 + scale)) + shift).astype(BF16)


def _dot(a, b):
    return jnp.dot(a, b, preferred_element_type=F32)


def _params(*sem):
    return pltpu.CompilerParams(dimension_semantics=sem, vmem_limit_bytes=VMEM_LIMIT_BYTES)


def _mod_kernel(c_ref, w_ref, b_ref, o_ref):
    c = c_ref[...]
    cond = (c * _sigmoid(c)).astype(BF16)
    o_ref[0] = _dot(cond, w_ref[0].astype(BF16)) + b_ref[0]


def _modulation(c_pad, w_ada, b_ada):
    rows = c_pad.shape[0]
    n = N_MOD * D_MODEL
    return pl.pallas_call(
        _mod_kernel,
        grid=(DEPTH, n // MOD_TN),
        in_specs=[
            pl.BlockSpec((rows, D_MODEL), lambda l, j: (0, 0)),
            pl.BlockSpec((1, D_MODEL, MOD_TN), lambda l, j: (l, 0, j)),
            pl.BlockSpec((1, 1, MOD_TN), lambda l, j: (l, 0, j)),
        ],
        out_specs=pl.BlockSpec((1, rows, MOD_TN), lambda l, j: (l, 0, j)),
        out_shape=jax.ShapeDtypeStruct((DEPTH, rows, n), F32),
        compiler_params=_params("arbitrary", "arbitrary"),
        name="adaln_mod",
    )(c_pad, w_ada, b_ada.reshape(DEPTH, 1, n))


def _prenorm_kernel(x_ref, g_ref, sc_ref, sh_ref, h_ref):
    h_ref[...] = _modulated_norm(x_ref[...], g_ref[...], sc_ref[0], sh_ref[0])


def _prenorm(x, g, scale, shift):
    tm = PRENORM_TM
    tiles_per_seq = SEQ // tm
    return pl.pallas_call(
        _prenorm_kernel,
        grid=(M_ROWS // tm,),
        in_specs=[
            pl.BlockSpec((tm, D_MODEL), lambda i: (i, 0)),
            pl.BlockSpec((1, D_MODEL), lambda i: (0, 0)),
            pl.BlockSpec((1, 1, D_MODEL), lambda i: (i // tiles_per_seq, 0, 0)),
            pl.BlockSpec((1, 1, D_MODEL), lambda i: (i // tiles_per_seq, 0, 0)),
        ],
        out_specs=pl.BlockSpec((tm, D_MODEL), lambda i: (i, 0)),
        out_shape=jax.ShapeDtypeStruct((M_ROWS, D_MODEL), BF16),
        compiler_params=_params("arbitrary"),
        name="prenorm",
    )(x, g, scale, shift)


def _qkv_kernel(h_ref, w_ref, b_ref, o_ref, kmean_ref, vt_ref):
    acc = _dot(h_ref[...], w_ref[...]) + b_ref[...]
    o_ref[...] = acc.astype(BF16)
    for r in range(QKV_TM // MOBA_BLOCK):
        rows = slice(r * MOBA_BLOCK, (r + 1) * MOBA_BLOCK)
        kmean_ref[0, r:r + 1, :] = jnp.mean(acc[rows, BRANCH_WIDTH:2 * BRANCH_WIDTH],
                                            axis=0, keepdims=True)
        vt_ref[r] = jnp.transpose(acc[rows, 2 * BRANCH_WIDTH:3 * BRANCH_WIDTH]).astype(BF16)


def _qkv_proj(h, w, b):
    tm = QKV_TM
    n = 4 * BRANCH_WIDTH
    blocks_per_tile = tm // MOBA_BLOCK
    return pl.pallas_call(
        _qkv_kernel,
        grid=(M_ROWS // tm,),
        in_specs=[
            pl.BlockSpec((tm, D_MODEL), lambda i: (i, 0)),
            pl.BlockSpec((D_MODEL, n), lambda i: (0, 0), pipeline_mode=pl.Buffered(1)),
            pl.BlockSpec((1, n), lambda i: (0, 0)),
        ],
        out_specs=[
            pl.BlockSpec((tm, n), lambda i: (i, 0)),
            pl.BlockSpec((1, blocks_per_tile, BRANCH_WIDTH), lambda i: (i, 0, 0)),
            pl.BlockSpec((blocks_per_tile, BRANCH_WIDTH, MOBA_BLOCK), lambda i: (i, 0, 0)),
        ],
        out_shape=[
            jax.ShapeDtypeStruct((M_ROWS, n), BF16),
            jax.ShapeDtypeStruct((M_ROWS // tm, blocks_per_tile, BRANCH_WIDTH), F32),
            jax.ShapeDtypeStruct((M_ROWS // MOBA_BLOCK, BRANCH_WIDTH, MOBA_BLOCK), BF16),
        ],
        compiler_params=_params("arbitrary"),
        name="qkv_proj",
    )(h, w, b)


def _uvg_kernel(h_ref, w_ref, b_ref, o_ref):
    j = pl.program_id(1)

    def project(act):
        for r0 in range(0, UVG_TM, UVG_ROWS):
            rows = slice(r0, r0 + UVG_ROWS)
            o_ref[rows, :] = act(_dot(h_ref[rows, :], w_ref[...]) + b_ref[...]).astype(BF16)

    @pl.when(j == 0)
    def _():
        project(_gelu_tanh)

    @pl.when(j > 0)
    def _():
        project(_sigmoid2)


def _uvg_proj(h, w_in, b_in):
    tm, tn = UVG_TM, UVG_TN
    n = 2 * BRANCH_WIDTH + 3 * D_MODEL
    skipped = 4 * BRANCH_WIDTH // tn
    src = lambda j: jnp.where(j == 0, 0, j + skipped)
    return pl.pallas_call(
        _uvg_kernel,
        grid=(M_ROWS // tm, n // tn),
        in_specs=[
            pl.BlockSpec((tm, D_MODEL), lambda i, j: (i, 0)),
            pl.BlockSpec((D_MODEL, tn), lambda i, j: (0, src(j))),
            pl.BlockSpec((1, tn), lambda i, j: (0, src(j))),
        ],
        out_specs=pl.BlockSpec((tm, tn), lambda i, j: (i, j)),
        out_shape=jax.ShapeDtypeStruct((M_ROWS, n), BF16),
        compiler_params=_params("arbitrary", "arbitrary"),
        name="uvg_proj",
    )(h, w_in, b_in)


def _branch_kernel(u_ref, v_ref, z_ref, zprev_ref, ng_ref, sw_ref, sb_ref, pw_ref, ps_ref,
                   ysgu_ref, ypool_ref):
    i = pl.program_id(0)
    tr = BRANCH_TR
    first = (i % (SEQ // tr)) == 0

    v = v_ref[...].astype(F32)
    mu = jnp.mean(v, axis=-1, keepdims=True)
    vc = v - mu
    var = jnp.mean(vc * vc, axis=-1, keepdims=True)
    vn = (vc * lax.rsqrt(var + EPS) * ng_ref[...]).astype(BF16)

    row = lax.broadcasted_iota(jnp.int32, (SGU_CHUNK, SGU_CHUNK), 0)
    col = lax.broadcasted_iota(jnp.int32, (SGU_CHUNK, SGU_CHUNK), 1)
    causal = row >= col
    wm = [jnp.where(causal, sw_ref[g], 0.0).astype(BF16) for g in range(SGU_GROUPS)]
    lane = lax.broadcasted_iota(jnp.int32, (SGU_CHUNK, LANES), 1)
    low_half = lane < SGU_GROUP_DIM

    n_chunks = tr // SGU_CHUNK
    for pr in range(SGU_GROUPS // 2):
        cols = slice(pr * LANES, (pr + 1) * LANES)
        blk = jnp.concatenate([vn[c * SGU_CHUNK:(c + 1) * SGU_CHUNK, cols] for c in range(n_chunks)],
                              axis=1)
        low = jnp.concatenate([low_half] * n_chunks, axis=1)
        mixed = (_dot(wm[2 * pr], jnp.where(low, blk, jnp.zeros_like(blk)))
                 + _dot(wm[2 * pr + 1], jnp.where(low, jnp.zeros_like(blk), blk)))
        for c in range(n_chunks):
            rows = slice(c * SGU_CHUNK, (c + 1) * SGU_CHUNK)
            m_c = mixed[:, c * LANES:(c + 1) * LANES] + sb_ref[:, cols]
            ysgu_ref[rows, cols] = (u_ref[rows, cols].astype(F32) * m_c).astype(BF16)

    z = z_ref[...].astype(F32)
    zprev = jnp.where(first, 0.0, zprev_ref[...].astype(F32))
    pos = (i % (SEQ // tr)) * tr + lax.broadcasted_iota(jnp.int32, (tr, POOL_GROUP_DIM), 0)
    for gi, w in enumerate(POOL_WINDOWS):
        cols = slice(gi * POOL_GROUP_DIM, (gi + 1) * POOL_GROUP_DIM)
        zg = z[:, cols]
        cur = jnp.concatenate([zprev[:, cols], zg], axis=0)
        d = 1
        while d < w:
            cur = cur[d:] + cur[:-d]
            d *= 2
        win_sum = cur[cur.shape[0] - tr:]
        count = jnp.minimum(pos + 1, w).astype(F32)
        pooled = win_sum / count - zg
        y = _dot(pooled.astype(BF16), pw_ref[gi].astype(BF16)) * ps_ref[:, cols]
        ypool_ref[:, cols] = y.astype(BF16)


def _branches(uvg, qkvz, sgu_norm_g, sgu_w, sgu_bias_full, pool_w, pool_scale):
    tr = BRANCH_TR
    halo_blocks = tr // POOL_HALO
    return pl.pallas_call(
        _branch_kernel,
        grid=(M_ROWS // tr,),
        in_specs=[
            pl.BlockSpec((tr, BRANCH_WIDTH), lambda i: (i, 0)),
            pl.BlockSpec((tr, BRANCH_WIDTH), lambda i: (i, 1)),
            pl.BlockSpec((tr, BRANCH_WIDTH), lambda i: (i, 3)),
            pl.BlockSpec((POOL_HALO, BRANCH_WIDTH),
                         lambda i: (jnp.maximum(i * halo_blocks - 1, 0), 3)),
            pl.BlockSpec((1, BRANCH_WIDTH), lambda i: (0, 0)),
            pl.BlockSpec((SGU_GROUPS, SGU_CHUNK, SGU_CHUNK), lambda i: (0, 0, 0)),
            pl.BlockSpec((SGU_CHUNK, BRANCH_WIDTH), lambda i: (0, 0)),
            pl.BlockSpec((POOL_GROUPS, POOL_GROUP_DIM, POOL_GROUP_DIM), lambda i: (0, 0, 0)),
            pl.BlockSpec((1, BRANCH_WIDTH), lambda i: (0, 0)),
        ],
        out_specs=[
            pl.BlockSpec((tr, BRANCH_WIDTH), lambda i: (i, 0)),
            pl.BlockSpec((tr, BRANCH_WIDTH), lambda i: (i, 0)),
        ],
        out_shape=[
            jax.ShapeDtypeStruct((M_ROWS, BRANCH_WIDTH), BF16),
            jax.ShapeDtypeStruct((M_ROWS, BRANCH_WIDTH), BF16),
        ],
        compiler_params=_params("arbitrary"),
        name="sgu_pool",
    )(uvg, uvg, qkvz, qkvz, sgu_norm_g, sgu_w, sgu_bias_full, pool_w, pool_scale)


def _attn_kernel(q_ref, k_ref, vt_ref, km_ref, o_ref,
                 qs_ref, bias_ref, pick_ref, m_ref, acc_ref, s_ref, smax_ref):
    own = pl.program_id(1)
    tq, blk, hd = ATTN_TQ, MOBA_BLOCK, MOBA_HEAD_DIM
    own_f = own.astype(F32)

    key_row = lax.broadcasted_iota(jnp.int32, (blk, tq), 0)
    causal = jnp.where(key_row > lax.broadcasted_iota(jnp.int32, (blk, tq), 1), NEG, 0.0)
    key_row_f = key_row.astype(F32)
    blk_id = lax.broadcasted_iota(jnp.int32, (N_KV_BLOCKS, tq), 0)
    blk_id_f = blk_id.astype(F32)
    past = blk_id < own
    head_rows = lax.broadcasted_iota(jnp.int32, (LANES, tq), 0) < hd
    ones_rows = jnp.ones((ONES_ROWS, blk), BF16)

    q_t = jnp.transpose(q_ref[...].astype(F32))
    k_own = pl.multiple_of(own * blk, blk)

    def values(j, h):
        return jnp.concatenate([vt_ref[j, h * hd:(h + 1) * hd, :], ones_rows], axis=0)

    for pr in range(MOBA_HEADS // 2):
        cols = slice(pr * LANES, (pr + 1) * LANES)
        q_pair = q_t[pr * LANES:(pr + 1) * LANES]
        km = km_ref[0, :, cols]
        km_hi = km.astype(BF16)
        km_lo = (km - km_hi.astype(F32)).astype(BF16)
        for hh in range(2):
            h = 2 * pr + hh
            slope = 2.0 ** -(h + 1)
            mine = head_rows if hh == 0 else ~head_rows
            qh = jnp.where(mine, q_pair, 0.0).astype(BF16)

            sc = _dot(km_hi, qh) + _dot(km_lo, qh)
            sc = jnp.where(past, sc, -jnp.inf)
            for r in range(MOBA_TOPK):
                best = jnp.max(sc, axis=0, keepdims=True)
                idx = jnp.min(jnp.where(sc == best, blk_id_f, 1e9), axis=0, keepdims=True)
                pick_ref[h, r] = jnp.where(best > -jnp.inf, idx, -1.0)
                sc = jnp.where(blk_id_f == idx, -jnp.inf, sc)

            qs = jnp.where(mine, q_pair * (hd ** -0.5 * LOG2E), 0.0).astype(BF16)
            qs_ref[h] = qs
            bias = (slope * LOG2E) * key_row_f
            bias_ref[h] = bias

            s = _dot(k_ref[pl.ds(k_own, blk), cols], qs) + bias + causal
            m = jnp.max(s, axis=0, keepdims=True)
            m_ref[h] = m
            acc_ref[h] = _dot(values(own, h), jnp.exp2(s - m).astype(BF16))

    def scores(j, h, slot):
        start = pl.multiple_of(j * blk, blk)
        pr = h // 2
        kj = k_ref[pl.ds(start, blk), pr * LANES:(pr + 1) * LANES]
        s = _dot(kj, qs_ref[h]) + bias_ref[h]
        s_ref[slot, h] = s
        smax_ref[slot, h] = jnp.max(s, axis=0, keepdims=True)

    def absorb(j, h, slot):
        jf = j.astype(F32)
        slope = 2.0 ** -(h + 1)
        chosen = (pick_ref[h, 0] == jf) | (pick_ref[h, 1] == jf) | (pick_ref[h, 2] == jf)
        shift = jnp.where(chosen, (-slope * blk * LOG2E) * (own_f - jf), NEG)
        m = m_ref[h]
        m_new = jnp.maximum(m, smax_ref[slot, h] + shift)
        p = jnp.exp2(s_ref[slot, h] - (m_new - shift))
        m_ref[h] = m_new
        acc_ref[h] = jnp.exp2(m - m_new) * acc_ref[h] + _dot(values(j, h), p.astype(BF16))

    for h in range(MOBA_HEADS):
        scores(0, h, 0)

    def body(i, carry):
        a = 2 * i
        b = jnp.minimum(a + 1, own)
        c = jnp.minimum(a + 2, own)
        for h in range(MOBA_HEADS):
            scores(b, h, 1)
            absorb(a, h, 0)
        for h in range(MOBA_HEADS):
            scores(c, h, 0)
            absorb(b, h, 1)
        return carry

    lax.fori_loop(0, (own + 1) // 2, body, 0)

    o_t = jnp.concatenate([acc_ref[h, :hd] / acc_ref[h, hd:hd + 1] for h in range(MOBA_HEADS)],
                          axis=0)
    o_ref[...] = jnp.transpose(o_t).astype(BF16)


def _attention(qkvz, kmean, v_t):
    tq = ATTN_TQ
    q_tiles = SEQ // tq
    return pl.pallas_call(
        _attn_kernel,
        grid=(BATCH, q_tiles),
        in_specs=[
            pl.BlockSpec((tq, BRANCH_WIDTH), lambda b, i: (b * q_tiles + i, 0)),
            pl.BlockSpec((SEQ, BRANCH_WIDTH), lambda b, i: (b, 1)),
            pl.BlockSpec((N_KV_BLOCKS, BRANCH_WIDTH, MOBA_BLOCK), lambda b, i: (b, 0, 0)),
            pl.BlockSpec((1, N_KV_BLOCKS, BRANCH_WIDTH), lambda b, i: (b, 0, 0)),
        ],
        out_specs=pl.BlockSpec((tq, BRANCH_WIDTH), lambda b, i: (b * q_tiles + i, 0)),
        out_shape=jax.ShapeDtypeStruct((M_ROWS, BRANCH_WIDTH), BF16),
        scratch_shapes=[
            pltpu.VMEM((MOBA_HEADS, LANES, tq), BF16),
            pltpu.VMEM((MOBA_HEADS, MOBA_BLOCK, tq), F32),
            pltpu.VMEM((MOBA_HEADS, MOBA_TOPK, 1, tq), F32),
            pltpu.VMEM((MOBA_HEADS, 1, tq), F32),
            pltpu.VMEM((MOBA_HEADS, MOBA_HEAD_DIM + ONES_ROWS, tq), F32),
            pltpu.VMEM((2, MOBA_HEADS, MOBA_BLOCK, tq), F32),
            pltpu.VMEM((2, MOBA_HEADS, 1, tq), F32),
        ],
        compiler_params=_params("arbitrary", "arbitrary"),
        name="moba_attention",
    )(qkvz, qkvz, v_t, kmean)


def _merge_kernel(ys_ref, ym_ref, yp_ref, gs0, gs1, gm0, gm1, gp0, gp1,
                  ws_ref, wm_ref, wp_ref, wo_ref, x_ref, g_ref, gt_ref, g2_ref, sc2_ref, sh2_ref,
                  o_ref, h_ref):
    half_w = D_MODEL // 2
    for r0 in range(0, MERGE_TM, MERGE_ROWS):
        rows = slice(r0, r0 + MERGE_ROWS)
        ys, ym, yp = ys_ref[rows, :], ym_ref[rows, :], yp_ref[rows, :]
        halves = []
        for hf, (gs, gm, gp) in enumerate(((gs0, gm0, gp0), (gs1, gm1, gp1))):
            cols = slice(hf * half_w, (hf + 1) * half_w)
            merged = (gs[rows, :].astype(F32) * _dot(ys, ws_ref[:, cols])
                      + gm[rows, :].astype(F32) * _dot(ym, wm_ref[:, cols])
                      + gp[rows, :].astype(F32) * _dot(yp, wp_ref[:, cols]))
            halves.append(merged.astype(BF16))
        y = _dot(jnp.concatenate(halves, axis=1), wo_ref[...])
        x_new = x_ref[rows, :] + _rms_norm(y, g_ref[...] * gt_ref[0])
        o_ref[rows, :] = x_new
        h_ref[rows, :] = _modulated_norm(x_new, g2_ref[...], sc2_ref[0], sh2_ref[0])


def _merge(ysgu, ymoba, ypool, uvg, w_sgu_out, w_moba_out, w_pool_out, w_out, x, g, gate,
           g_next, scale_next, shift_next):
    tm = MERGE_TM
    tiles_per_seq = SEQ // tm
    half_w = D_MODEL // 2
    gate_col0 = 2 * BRANCH_WIDTH // half_w
    resident = functools.partial(pl.BlockSpec, pipeline_mode=pl.Buffered(1))
    branch = pl.BlockSpec((tm, BRANCH_WIDTH), lambda i: (i, 0))
    gate_specs = [pl.BlockSpec((tm, half_w), functools.partial(lambda i, c: (i, c), c=gate_col0 + c))
                  for c in range(6)]
    w_branch = resident((BRANCH_WIDTH, D_MODEL), lambda i: (0, 0))
    rows = pl.BlockSpec((tm, D_MODEL), lambda i: (i, 0))
    vec = pl.BlockSpec((1, D_MODEL), lambda i: (0, 0))
    per_batch = pl.BlockSpec((1, 1, D_MODEL), lambda i: (i // tiles_per_seq, 0, 0))
    return pl.pallas_call(
        _merge_kernel,
        grid=(M_ROWS // tm,),
        in_specs=[branch, branch, branch, *gate_specs, w_branch, w_branch, w_branch,
                  resident((D_MODEL, D_MODEL), lambda i: (0, 0)),
                  rows, vec, per_batch, vec, per_batch, per_batch],
        out_specs=[rows, rows],
        out_shape=[jax.ShapeDtypeStruct((M_ROWS, D_MODEL), F32),
                   jax.ShapeDtypeStruct((M_ROWS, D_MODEL), BF16)],
        compiler_params=_params("arbitrary"),
        name="merge_outproj",
    )(ysgu, ymoba, ypool, uvg, uvg, uvg, uvg, uvg, uvg, w_sgu_out, w_moba_out, w_pool_out, w_out,
      x, g, gate, g_next, scale_next, shift_next)


def _up_kernel(h_ref, hh_ref, wg_ref, wv_ref, cg_ref, cv_ref, bg_ref, bv_ref, a_ref,
               zg_ref, zv_ref):
    tm = UP_TM
    first_in_seq = (pl.program_id(0) % (SEQ // tm)) == 0
    h = h_ref[...]
    h_prev = jnp.where(first_in_seq, jnp.zeros_like(hh_ref), hh_ref[...])

    def conv(w_ref, z_ref, cw_ref, cb_ref):
        w = w_ref[...]
        z = _dot(h, w)
        z_prev = _dot(h_prev, w)[CONV_HALO - 8:]
        taps = []
        for s in range(UP_TN // LANES):
            cols = slice(s * LANES, (s + 1) * LANES)
            z_ref[s, 8:, :] = z[:, cols]
            z_ref[s, :8, :] = z_prev[:, cols]
            taps.append(cw_ref[0:1, cols] * z_ref[s, 6:6 + tm, :]
                        + cw_ref[1:2, cols] * z_ref[s, 7:7 + tm, :]
                        + cw_ref[2:3, cols] * z[:, cols] + cb_ref[:, cols])
        return jnp.concatenate(taps, axis=1)

    gate = conv(wg_ref, zg_ref, cg_ref, bg_ref)
    val = conv(wv_ref, zv_ref, cv_ref, bv_ref)
    a_ref[...] = (_gelu_tanh(gate) * val).astype(BF16)


def _ffn_up(h, w_up, w_conv, b_conv):
    tm, tn, halo = UP_TM, UP_TN, CONV_HALO
    nj = D_FF // tn
    halo_blocks = tm // halo
    return pl.pallas_call(
        _up_kernel,
        grid=(M_ROWS // tm, nj),
        in_specs=[
            pl.BlockSpec((tm, D_MODEL), lambda i, j: (i, 0)),
            pl.BlockSpec((halo, D_MODEL), lambda i, j: (jnp.maximum(i * halo_blocks - 1, 0), 0)),
            pl.BlockSpec((D_MODEL, tn), lambda i, j: (0, j)),
            pl.BlockSpec((D_MODEL, tn), lambda i, j: (0, nj + j)),
            pl.BlockSpec((3, tn), lambda i, j: (0, j)),
            pl.BlockSpec((3, tn), lambda i, j: (0, nj + j)),
            pl.BlockSpec((1, tn), lambda i, j: (0, j)),
            pl.BlockSpec((1, tn), lambda i, j: (0, nj + j)),
        ],
        out_specs=pl.BlockSpec((tm, tn), lambda i, j: (i, j)),
        out_shape=jax.ShapeDtypeStruct((M_ROWS, D_FF), BF16),
        scratch_shapes=[pltpu.VMEM((tn // LANES, tm + 8, LANES), F32) for _ in range(2)],
        compiler_params=_params("arbitrary", "arbitrary"),
        name="ffn_up",
    )(h, h, w_up, w_up, w_conv, w_conv, b_conv, b_conv)


def _down_kernel(emit_h, a_ref, w_ref, x_ref, g_ref, gt_ref, *rest):
    if emit_h:
        g2_ref, sc2_ref, sh2_ref, o_ref, h_ref = rest
    else:
        (o_ref,) = rest
    for r0 in range(0, DOWN_TM, DOWN_ROWS):
        rows = slice(r0, r0 + DOWN_ROWS)
        y = _dot(a_ref[rows, :], w_ref[...])
        x_new = x_ref[rows, :] + _rms_norm(y, g_ref[...] * gt_ref[0])
        o_ref[rows, :] = x_new
        if emit_h:
            h_ref[rows, :] = _modulated_norm(x_new, g2_ref[...], sc2_ref[0], sh2_ref[0])


def _ffn_down(a, w_down, x, g, gate, next_norm=None):
    tm = DOWN_TM
    tiles_per_seq = SEQ // tm
    rows = pl.BlockSpec((tm, D_MODEL), lambda i: (i, 0))
    vec = pl.BlockSpec((1, D_MODEL), lambda i: (0, 0))
    per_batch = pl.BlockSpec((1, 1, D_MODEL), lambda i: (i // tiles_per_seq, 0, 0))
    emit_h = next_norm is not None
    in_specs = [pl.BlockSpec((tm, D_FF), lambda i: (i, 0)),
                pl.BlockSpec((D_FF, D_MODEL), lambda i: (0, 0), pipeline_mode=pl.Buffered(1)),
                rows, vec, per_batch]
    out_specs = [rows]
    out_shape = [jax.ShapeDtypeStruct((M_ROWS, D_MODEL), F32)]
    args = [a, w_down, x, g, gate]
    if emit_h:
        in_specs += [vec, per_batch, per_batch]
        out_specs.append(rows)
        out_shape.append(jax.ShapeDtypeStruct((M_ROWS, D_MODEL), BF16))
        args += list(next_norm)
    return pl.pallas_call(
        functools.partial(_down_kernel, emit_h),
        grid=(M_ROWS // tm,),
        in_specs=in_specs,
        out_specs=out_specs,
        out_shape=out_shape,
        compiler_params=_params("arbitrary"),
        name="ffn_down",
    )(*args)


def kernel(x, c, g_pre_mix, g_post_mix, g_pre_ffn, g_post_ffn, w_ada, b_ada, w_in, b_in,
           sgu_norm_g, sgu_w, sgu_b, pool_w, pool_scale, w_sgu_out, w_moba_out, w_pool_out,
           w_out, w_up, w_conv, b_conv, w_down):
    c_pad = jnp.pad(c, ((0, 8 - BATCH), (0, 0)))
    mod = _modulation(c_pad, w_ada, b_ada)[:, :BATCH]
    mod = mod.reshape(DEPTH, BATCH, N_MOD, 1, D_MODEL)
    shift1, scale1, gate1, shift2, scale2, gate2 = [mod[:, :, n] for n in range(N_MOD)]
    row = lambda v, l: v[l].reshape(1, -1)
    qkv_cols = slice(2 * BRANCH_WIDTH, 6 * BRANCH_WIDTH)

    xf = x.reshape(M_ROWS, D_MODEL)
    h = _prenorm(xf, row(g_pre_mix, 0), scale1[0], shift1[0])
    for l in range(DEPTH):
        w_in_l = w_in[l].astype(BF16)
        uvg = _uvg_proj(h, w_in_l, row(b_in, l))
        qkvz, kmean, v_t = _qkv_proj(h, w_in_l[:, qkv_cols], b_in[l, qkv_cols].reshape(1, -1))
        kmean = kmean.reshape(BATCH, N_KV_BLOCKS, BRANCH_WIDTH)

        sgu_bias_full = jnp.repeat(sgu_b[l].T, SGU_GROUP_DIM, axis=1)
        ysgu, ypool = _branches(uvg, qkvz, row(sgu_norm_g, l), sgu_w[l], sgu_bias_full, pool_w[l],
                                row(pool_scale, l))
        ymoba = _attention(qkvz, kmean, v_t)
        xf, h = _merge(ysgu, ymoba, ypool, uvg, w_sgu_out[l].astype(BF16),
                       w_moba_out[l].astype(BF16), w_pool_out[l].astype(BF16),
                       w_out[l].astype(BF16), xf, row(g_post_mix, l), gate1[l],
                       row(g_pre_ffn, l), scale2[l], shift2[l])

        a = _ffn_up(h, w_up[l].astype(BF16), w_conv[l], row(b_conv, l))
        if l + 1 < DEPTH:
            xf, h = _ffn_down(a, w_down[l].astype(BF16), xf, row(g_post_ffn, l), gate2[l],
                              (row(g_pre_mix, l + 1), scale1[l + 1], shift1[l + 1]))
        else:
            (xf,) = _ffn_down(a, w_down[l].astype(BF16), xf, row(g_post_ffn, l), gate2[l])
    return xf.reshape(BATCH, SEQ, D_MODEL)
```

```python
import functools

import jax
import jax.numpy as jnp
import numpy as np
from jax import lax
from jax.experimental import pallas as pl
from jax.experimental.pallas import tpu as pltpu

F32 = jnp.float32
BF16 = jnp.bfloat16

D_MODEL = 2048
BATCH = 4
SEQ = 8192
DEPTH = 2
EPS = 1e-6
BRANCH_WIDTH = 512
SGU_GROUPS = 8
SGU_GROUP_DIM = 64
SGU_CHUNK = 128
MOBA_HEADS = 8
MOBA_HEAD_DIM = 64
MOBA_BLOCK = 256
MOBA_TOPK = 3
POOL_WINDOWS = (2, 4, 8, 16)
POOL_GROUPS = 4
POOL_GROUP_DIM = 128
IN_WIDTH = 6 * BRANCH_WIDTH + 3 * D_MODEL
D_FF = 5632
N_MOD = 6
M_ROWS = BATCH * SEQ
N_KV_BLOCKS = SEQ // MOBA_BLOCK

VMEM_LIMIT_BYTES = 56 * 1024 * 1024
LANES = 128
NEG = -1e30
LOG2E = 1.4426950408889634
ALIBI_PARTS = 3


def _bf16_parts(value, n):
    parts = []
    for _ in range(n):
        parts.append(float(np.asarray(value - sum(parts), dtype=jnp.bfloat16)))
    return tuple(parts)


LOG2E_PARTS = _bf16_parts(LOG2E, ALIBI_PARTS)
ONES_ROWS = 16

MOD_TN = 1024
PRENORM_TM = 512
QKV_TM = 1024
UVG_TM, UVG_TN = 2048, 1024
UVG_ROWS = 512
BRANCH_TR = 512
ATTN_TQ = MOBA_BLOCK
MERGE_TM = 512
MERGE_ROWS = 256
UP_TM, UP_TN = 1024, 512
CONV_HALO = 16
DOWN_TM = 256
DOWN_ROWS = 128
POOL_HALO = 16


GELU_C = 0.7978845608028654
GELU_A = 0.044715


def _sigmoid(x):
    return 1.0 / (1.0 + jnp.exp(-x))


def _sigmoid2(x):
    return 1.0 / (1.0 + jnp.exp2(-LOG2E * x))


def _gelu_tanh(x):
    k1 = -2.0 * GELU_C * LOG2E
    return x / (1.0 + jnp.exp2(x * (k1 + (k1 * GELU_A) * (x * x))))


def _rms_norm(x, g):
    ms = jnp.mean(x * x, axis=-1, keepdims=True)
    return (x * lax.rsqrt(ms + EPS)) * g


def _modulated_norm(x, g, scale, shift):
    return (_rms_norm(x, g * (1.0 + scale)) + shift).astype(BF16)


def _dot(a, b):
    return jnp.dot(a, b, preferred_element_type=F32)


def _params(*sem):
    return pltpu.CompilerParams(dimension_semantics=sem, vmem_limit_bytes=VMEM_LIMIT_BYTES)


def _mod_kernel(c_ref, w_ref, b_ref, o_ref):
    c = c_ref[...]
    cond = (c * _sigmoid(c)).astype(BF16)
    o_ref[0] = _dot(cond, w_ref[0].astype(BF16)) + b_ref[0]


def _modulation(c_pad, w_ada, b_ada):
    rows = c_pad.shape[0]
    n = N_MOD * D_MODEL
    return pl.pallas_call(
        _mod_kernel,
        grid=(DEPTH, n // MOD_TN),
        in_specs=[
            pl.BlockSpec((rows, D_MODEL), lambda l, j: (0, 0)),
            pl.BlockSpec((1, D_MODEL, MOD_TN), lambda l, j: (l, 0, j)),
            pl.BlockSpec((1, 1, MOD_TN), lambda l, j: (l, 0, j)),
        ],
        out_specs=pl.BlockSpec((1, rows, MOD_TN), lambda l, j: (l, 0, j)),
        out_shape=jax.ShapeDtypeStruct((DEPTH, rows, n), F32),
        compiler_params=_params("arbitrary", "arbitrary"),
        name="adaln_mod",
    )(c_pad, w_ada, b_ada.reshape(DEPTH, 1, n))


def _prenorm_kernel(x_ref, g_ref, sc_ref, sh_ref, h_ref):
    h_ref[...] = _modulated_norm(x_ref[...], g_ref[...], sc_ref[0], sh_ref[0])


def _prenorm(x, g, scale, shift):
    tm = PRENORM_TM
    tiles_per_seq = SEQ // tm
    return pl.pallas_call(
        _prenorm_kernel,
        grid=(M_ROWS // tm,),
        in_specs=[
            pl.BlockSpec((tm, D_MODEL), lambda i: (i, 0)),
            pl.BlockSpec((1, D_MODEL), lambda i: (0, 0)),
            pl.BlockSpec((1, 1, D_MODEL), lambda i: (i // tiles_per_seq, 0, 0)),
            pl.BlockSpec((1, 1, D_MODEL), lambda i: (i // tiles_per_seq, 0, 0)),
        ],
        out_specs=pl.BlockSpec((tm, D_MODEL), lambda i: (i, 0)),
        out_shape=jax.ShapeDtypeStruct((M_ROWS, D_MODEL), BF16),
        compiler_params=_params("arbitrary"),
        name="prenorm",
    )(x, g, scale, shift)


def _qkv_kernel(h_ref, w_ref, b_ref, o_ref, kmean_ref, vt_ref, kaug_ref):
    acc = _dot(h_ref[...], w_ref[...]) + b_ref[...]
    o_ref[...] = acc.astype(BF16)
    hd = MOBA_HEAD_DIM
    lane = lax.broadcasted_iota(jnp.int32, (QKV_TM, LANES - hd), 1)
    row_in_block = (lax.broadcasted_iota(jnp.int32, (QKV_TM, LANES - hd), 0)
                    % MOBA_BLOCK).astype(F32)
    groups = []
    for h in range(MOBA_HEADS):
        k_h = acc[:, BRANCH_WIDTH + h * hd:BRANCH_WIDTH + (h + 1) * hd]
        pos = jnp.where(lane < ALIBI_PARTS, (2.0 ** -(h + 1)) * row_in_block, 0.0)
        groups += [k_h, pos]
    kaug_ref[...] = jnp.concatenate(groups, axis=1).astype(BF16)
    for r in range(QKV_TM // MOBA_BLOCK):
        rows = slice(r * MOBA_BLOCK, (r + 1) * MOBA_BLOCK)
        kmean_ref[0, r:r + 1, :] = jnp.mean(acc[rows, BRANCH_WIDTH:2 * BRANCH_WIDTH],
                                            axis=0, keepdims=True)
        vt_ref[r] = jnp.transpose(acc[rows, 2 * BRANCH_WIDTH:3 * BRANCH_WIDTH]).astype(BF16)


def _qkv_proj(h, w, b):
    tm = QKV_TM
    n = 4 * BRANCH_WIDTH
    blocks_per_tile = tm // MOBA_BLOCK
    return pl.pallas_call(
        _qkv_kernel,
        grid=(M_ROWS // tm,),
        in_specs=[
            pl.BlockSpec((tm, D_MODEL), lambda i: (i, 0)),
            pl.BlockSpec((D_MODEL, n), lambda i: (0, 0), pipeline_mode=pl.Buffered(1)),
            pl.BlockSpec((1, n), lambda i: (0, 0)),
        ],
        out_specs=[
            pl.BlockSpec((tm, n), lambda i: (i, 0)),
            pl.BlockSpec((1, blocks_per_tile, BRANCH_WIDTH), lambda i: (i, 0, 0)),
            pl.BlockSpec((blocks_per_tile, BRANCH_WIDTH, MOBA_BLOCK), lambda i: (i, 0, 0)),
            pl.BlockSpec((tm, MOBA_HEADS * LANES), lambda i: (i, 0)),
        ],
        out_shape=[
            jax.ShapeDtypeStruct((M_ROWS, n), BF16),
            jax.ShapeDtypeStruct((M_ROWS // tm, blocks_per_tile, BRANCH_WIDTH), F32),
            jax.ShapeDtypeStruct((M_ROWS // MOBA_BLOCK, BRANCH_WIDTH, MOBA_BLOCK), BF16),
            jax.ShapeDtypeStruct((M_ROWS, MOBA_HEADS * LANES), BF16),
        ],
        compiler_params=_params("arbitrary"),
        name="qkv_proj",
    )(h, w, b)


def _uvg_kernel(h_ref, w_ref, b_ref, o_ref):
    j = pl.program_id(1)

    def project(act):
        for r0 in range(0, UVG_TM, UVG_ROWS):
            rows = slice(r0, r0 + UVG_ROWS)
            o_ref[rows, :] = act(_dot(h_ref[rows, :], w_ref[...]) + b_ref[...]).astype(BF16)

    @pl.when(j == 0)
    def _():
        project(_gelu_tanh)

    @pl.when(j > 0)
    def _():
        project(_sigmoid2)


def _uvg_proj(h, w_in, b_in):
    tm, tn = UVG_TM, UVG_TN
    n = 2 * BRANCH_WIDTH + 3 * D_MODEL
    skipped = 4 * BRANCH_WIDTH // tn
    src = lambda j: jnp.where(j == 0, 0, j + skipped)
    return pl.pallas_call(
        _uvg_kernel,
        grid=(M_ROWS // tm, n // tn),
        in_specs=[
            pl.BlockSpec((tm, D_MODEL), lambda i, j: (i, 0)),
            pl.BlockSpec((D_MODEL, tn), lambda i, j: (0, src(j))),
            pl.BlockSpec((1, tn), lambda i, j: (0, src(j))),
        ],
        out_specs=pl.BlockSpec((tm, tn), lambda i, j: (i, j)),
        out_shape=jax.ShapeDtypeStruct((M_ROWS, n), BF16),
        compiler_params=_params("arbitrary", "arbitrary"),
        name="uvg_proj",
    )(h, w_in, b_in)


def _branch_kernel(u_ref, v_ref, z_ref, zprev_ref, ng_ref, sw_ref, sb_ref, pw_ref, ps_ref,
                   ysgu_ref, ypool_ref):
    i = pl.program_id(0)
    tr = BRANCH_TR
    first = (i % (SEQ // tr)) == 0

    v = v_ref[...].astype(F32)
    mu = jnp.mean(v, axis=-1, keepdims=True)
    vc = v - mu
    var = jnp.mean(vc * vc, axis=-1, keepdims=True)
    vn = (vc * lax.rsqrt(var + EPS) * ng_ref[...]).astype(BF16)

    row = lax.broadcasted_iota(jnp.int32, (SGU_CHUNK, SGU_CHUNK), 0)
    col = lax.broadcasted_iota(jnp.int32, (SGU_CHUNK, SGU_CHUNK), 1)
    causal = row >= col
    wm = [jnp.where(causal, sw_ref[g], 0.0).astype(BF16) for g in range(SGU_GROUPS)]
    lane = lax.broadcasted_iota(jnp.int32, (SGU_CHUNK, LANES), 1)
    low_half = lane < SGU_GROUP_DIM

    n_chunks = tr // SGU_CHUNK
    for pr in range(SGU_GROUPS // 2):
        cols = slice(pr * LANES, (pr + 1) * LANES)
        blk = jnp.concatenate([vn[c * SGU_CHUNK:(c + 1) * SGU_CHUNK, cols] for c in range(n_chunks)],
                              axis=1)
        low = jnp.concatenate([low_half] * n_chunks, axis=1)
        mixed = (_dot(wm[2 * pr], jnp.where(low, blk, jnp.zeros_like(blk)))
                 + _dot(wm[2 * pr + 1], jnp.where(low, jnp.zeros_like(blk), blk)))
        for c in range(n_chunks):
            rows = slice(c * SGU_CHUNK, (c + 1) * SGU_CHUNK)
            m_c = mixed[:, c * LANES:(c + 1) * LANES] + sb_ref[:, cols]
            ysgu_ref[rows, cols] = (u_ref[rows, cols].astype(F32) * m_c).astype(BF16)

    z = z_ref[...].astype(F32)
    zprev = jnp.where(first, 0.0, zprev_ref[...].astype(F32))
    pos = (i % (SEQ // tr)) * tr + lax.broadcasted_iota(jnp.int32, (tr, POOL_GROUP_DIM), 0)
    for gi, w in enumerate(POOL_WINDOWS):
        cols = slice(gi * POOL_GROUP_DIM, (gi + 1) * POOL_GROUP_DIM)
        zg = z[:, cols]
        cur = jnp.concatenate([zprev[:, cols], zg], axis=0)
        d = 1
        while d < w:
            cur = cur[d:] + cur[:-d]
            d *= 2
        win_sum = cur[cur.shape[0] - tr:]
        count = jnp.minimum(pos + 1, w).astype(F32)
        pooled = win_sum / count - zg
        y = _dot(pooled.astype(BF16), pw_ref[gi].astype(BF16)) * ps_ref[:, cols]
        ypool_ref[:, cols] = y.astype(BF16)


def _branches(uvg, qkvz, sgu_norm_g, sgu_w, sgu_bias_full, pool_w, pool_scale):
    tr = BRANCH_TR
    halo_blocks = tr // POOL_HALO
    return pl.pallas_call(
        _branch_kernel,
        grid=(M_ROWS // tr,),
        in_specs=[
            pl.BlockSpec((tr, BRANCH_WIDTH), lambda i: (i, 0)),
            pl.BlockSpec((tr, BRANCH_WIDTH), lambda i: (i, 1)),
            pl.BlockSpec((tr, BRANCH_WIDTH), lambda i: (i, 3)),
            pl.BlockSpec((POOL_HALO, BRANCH_WIDTH),
                         lambda i: (jnp.maximum(i * halo_blocks - 1, 0), 3)),
            pl.BlockSpec((1, BRANCH_WIDTH), lambda i: (0, 0)),
            pl.BlockSpec((SGU_GROUPS, SGU_CHUNK, SGU_CHUNK), lambda i: (0, 0, 0)),
            pl.BlockSpec((SGU_CHUNK, BRANCH_WIDTH), lambda i: (0, 0)),
            pl.BlockSpec((POOL_GROUPS, POOL_GROUP_DIM, POOL_GROUP_DIM), lambda i: (0, 0, 0)),
            pl.BlockSpec((1, BRANCH_WIDTH), lambda i: (0, 0)),
        ],
        out_specs=[
            pl.BlockSpec((tr, BRANCH_WIDTH), lambda i: (i, 0)),
            pl.BlockSpec((tr, BRANCH_WIDTH), lambda i: (i, 0)),
        ],
        out_shape=[
            jax.ShapeDtypeStruct((M_ROWS, BRANCH_WIDTH), BF16),
            jax.ShapeDtypeStruct((M_ROWS, BRANCH_WIDTH), BF16),
        ],
        compiler_params=_params("arbitrary"),
        name="sgu_pool",
    )(uvg, uvg, qkvz, qkvz, sgu_norm_g, sgu_w, sgu_bias_full, pool_w, pool_scale)


def _attn_kernel(q_ref, k_ref, vt_ref, km_ref, o_ref,
                 qs_ref, pick_ref, m_ref, acc_ref, s_ref, smax_ref):
    own = pl.program_id(1)
    tq, blk, hd = ATTN_TQ, MOBA_BLOCK, MOBA_HEAD_DIM
    own_f = own.astype(F32)

    key_row = lax.broadcasted_iota(jnp.int32, (blk, tq), 0)
    causal = jnp.where(key_row > lax.broadcasted_iota(jnp.int32, (blk, tq), 1), NEG, 0.0)
    blk_id = lax.broadcasted_iota(jnp.int32, (N_KV_BLOCKS, tq), 0)
    blk_id_f = blk_id.astype(F32)
    past = blk_id < own
    ones_rows = jnp.ones((ONES_ROWS, blk), BF16)
    part_row = lax.broadcasted_iota(jnp.int32, (LANES - hd, tq), 0)
    q_alibi = jnp.zeros((LANES - hd, tq), F32)
    for t, part in enumerate(LOG2E_PARTS):
        q_alibi = jnp.where(part_row == t, part, q_alibi)

    q_t = jnp.transpose(q_ref[...].astype(F32))

    def raw_scores(j, h):
        start = pl.multiple_of(j * blk, blk)
        return _dot(k_ref[pl.ds(start, blk), h * LANES:(h + 1) * LANES], qs_ref[h])

    def values(j, h):
        return jnp.concatenate([vt_ref[j, h * hd:(h + 1) * hd, :], ones_rows], axis=0)

    gate_scores = []
    for h in range(MOBA_HEADS):
        qh = q_t[h * hd:(h + 1) * hd]
        qs_ref[h] = jnp.concatenate([qh * (hd ** -0.5 * LOG2E), q_alibi], axis=0).astype(BF16)
        km = km_ref[0, :, h * hd:(h + 1) * hd]
        km_hi = km.astype(BF16)
        km_lo = (km - km_hi.astype(F32)).astype(BF16)
        qg = qh.astype(BF16)
        sc = _dot(km_hi, qg) + _dot(km_lo, qg)
        gate_scores.append(jnp.where(past, sc, -jnp.inf))

    for h in range(MOBA_HEADS):
        s = raw_scores(own, h) + causal
        s_ref[1, h] = s
        m_ref[h] = jnp.max(s, axis=0, keepdims=True)

    for r in range(MOBA_TOPK):
        for h in range(MOBA_HEADS):
            sc = gate_scores[h]
            best = jnp.max(sc, axis=0, keepdims=True)
            idx = jnp.min(jnp.where(sc == best, blk_id_f, 1e9), axis=0, keepdims=True)
            pick_ref[h, r] = jnp.where(best > -jnp.inf, idx, -1.0)
            gate_scores[h] = jnp.where(blk_id_f == idx, -jnp.inf, sc)

    for h in range(MOBA_HEADS):
        acc_ref[h] = _dot(values(own, h), jnp.exp2(s_ref[1, h] - m_ref[h]).astype(BF16))

    def scores(j, h, slot):
        s = raw_scores(j, h)
        s_ref[slot, h] = s
        smax_ref[slot, h] = jnp.max(s, axis=0, keepdims=True)

    def absorb(j, h, slot):
        jf = j.astype(F32)
        slope = 2.0 ** -(h + 1)
        chosen = (pick_ref[h, 0] == jf) | (pick_ref[h, 1] == jf) | (pick_ref[h, 2] == jf)
        shift = jnp.where(chosen, (-slope * blk * LOG2E) * (own_f - jf), NEG)
        m = m_ref[h]
        m_new = jnp.maximum(m, smax_ref[slot, h] + shift)
        p = jnp.exp2(s_ref[slot, h] - (m_new - shift))
        m_ref[h] = m_new
        acc_ref[h] = jnp.exp2(m - m_new) * acc_ref[h] + _dot(values(j, h), p.astype(BF16))

    for h in range(MOBA_HEADS):
        scores(0, h, 0)

    def body(i, carry):
        a = 2 * i
        b = jnp.minimum(a + 1, own)
        c = jnp.minimum(a + 2, own)
        for h in range(MOBA_HEADS):
            scores(b, h, 1)
            absorb(a, h, 0)
        for h in range(MOBA_HEADS):
            scores(c, h, 0)
            absorb(b, h, 1)
        return carry

    lax.fori_loop(0, (own + 1) // 2, body, 0)

    o_t = jnp.concatenate([acc_ref[h, :hd] / acc_ref[h, hd:hd + 1] for h in range(MOBA_HEADS)],
                          axis=0)
    o_ref[...] = jnp.transpose(o_t).astype(BF16)


def _attention(qkvz, k_aug, kmean, v_t):
    tq = ATTN_TQ
    q_tiles = SEQ // tq
    return pl.pallas_call(
        _attn_kernel,
        grid=(BATCH, q_tiles),
        in_specs=[
            pl.BlockSpec((tq, BRANCH_WIDTH), lambda b, i: (b * q_tiles + i, 0)),
            pl.BlockSpec((SEQ, MOBA_HEADS * LANES), lambda b, i: (b, 0),
                         pipeline_mode=pl.Buffered(1)),
            pl.BlockSpec((N_KV_BLOCKS, BRANCH_WIDTH, MOBA_BLOCK), lambda b, i: (b, 0, 0)),
            pl.BlockSpec((1, N_KV_BLOCKS, BRANCH_WIDTH), lambda b, i: (b, 0, 0)),
        ],
        out_specs=pl.BlockSpec((tq, BRANCH_WIDTH), lambda b, i: (b * q_tiles + i, 0)),
        out_shape=jax.ShapeDtypeStruct((M_ROWS, BRANCH_WIDTH), BF16),
        scratch_shapes=[
            pltpu.VMEM((MOBA_HEADS, LANES, tq), BF16),
            pltpu.VMEM((MOBA_HEADS, MOBA_TOPK, 1, tq), F32),
            pltpu.VMEM((MOBA_HEADS, 1, tq), F32),
            pltpu.VMEM((MOBA_HEADS, MOBA_HEAD_DIM + ONES_ROWS, tq), F32),
            pltpu.VMEM((2, MOBA_HEADS, MOBA_BLOCK, tq), F32),
            pltpu.VMEM((2, MOBA_HEADS, 1, tq), F32),
        ],
        compiler_params=_params("arbitrary", "arbitrary"),
        name="moba_attention",
    )(qkvz, k_aug, v_t, kmean)


def _merge_kernel(ys_ref, ym_ref, yp_ref, gs0, gs1, gm0, gm1, gp0, gp1,
                  ws_ref, wm_ref, wp_ref, wo_ref, x_ref, g_ref, gt_ref, g2_ref, sc2_ref, sh2_ref,
                  o_ref, h_ref):
    half_w = D_MODEL // 2
    for r0 in range(0, MERGE_TM, MERGE_ROWS):
        rows = slice(r0, r0 + MERGE_ROWS)
        ys, ym, yp = ys_ref[rows, :], ym_ref[rows, :], yp_ref[rows, :]
        halves = []
        for hf, (gs, gm, gp) in enumerate(((gs0, gm0, gp0), (gs1, gm1, gp1))):
            cols = slice(hf * half_w, (hf + 1) * half_w)
            merged = (gs[rows, :].astype(F32) * _dot(ys, ws_ref[:, cols])
                      + gm[rows, :].astype(F32) * _dot(ym, wm_ref[:, cols])
                      + gp[rows, :].astype(F32) * _dot(yp, wp_ref[:, cols]))
            halves.append(merged.astype(BF16))
        y = _dot(jnp.concatenate(halves, axis=1), wo_ref[...])
        x_new = x_ref[rows, :] + _rms_norm(y, g_ref[...] * gt_ref[0])
        o_ref[rows, :] = x_new
        h_ref[rows, :] = _modulated_norm(x_new, g2_ref[...], sc2_ref[0], sh2_ref[0])


def _merge(ysgu, ymoba, ypool, uvg, w_sgu_out, w_moba_out, w_pool_out, w_out, x, g, gate,
           g_next, scale_next, shift_next):
    tm = MERGE_TM
    tiles_per_seq = SEQ // tm
    half_w = D_MODEL // 2
    gate_col0 = 2 * BRANCH_WIDTH // half_w
    resident = functools.partial(pl.BlockSpec, pipeline_mode=pl.Buffered(1))
    branch = pl.BlockSpec((tm, BRANCH_WIDTH), lambda i: (i, 0))
    gate_specs = [pl.BlockSpec((tm, half_w), functools.partial(lambda i, c: (i, c), c=gate_col0 + c))
                  for c in range(6)]
    w_branch = resident((BRANCH_WIDTH, D_MODEL), lambda i: (0, 0))
    rows = pl.BlockSpec((tm, D_MODEL), lambda i: (i, 0))
    vec = pl.BlockSpec((1, D_MODEL), lambda i: (0, 0))
    per_batch = pl.BlockSpec((1, 1, D_MODEL), lambda i: (i // tiles_per_seq, 0, 0))
    return pl.pallas_call(
        _merge_kernel,
        grid=(M_ROWS // tm,),
        in_specs=[branch, branch, branch, *gate_specs, w_branch, w_branch, w_branch,
                  resident((D_MODEL, D_MODEL), lambda i: (0, 0)),
                  rows, vec, per_batch, vec, per_batch, per_batch],
        out_specs=[rows, rows],
        out_shape=[jax.ShapeDtypeStruct((M_ROWS, D_MODEL), F32),
                   jax.ShapeDtypeStruct((M_ROWS, D_MODEL), BF16)],
        compiler_params=_params("arbitrary"),
        name="merge_outproj",
    )(ysgu, ymoba, ypool, uvg, uvg, uvg, uvg, uvg, uvg, w_sgu_out, w_moba_out, w_pool_out, w_out,
      x, g, gate, g_next, scale_next, shift_next)


def _up_kernel(h_ref, hh_ref, wg_ref, wv_ref, cg_ref, cv_ref, bg_ref, bv_ref, a_ref,
               zg_ref, zv_ref):
    tm = UP_TM
    first_in_seq = (pl.program_id(0) % (SEQ // tm)) == 0
    h = h_ref[...]
    h_prev = jnp.where(first_in_seq, jnp.zeros_like(hh_ref), hh_ref[...])

    def conv(w_ref, z_ref, cw_ref, cb_ref):
        w = w_ref[...]
        z = _dot(h, w)
        z_prev = _dot(h_prev, w)[CONV_HALO - 8:]
        taps = []
        for s in range(UP_TN // LANES):
            cols = slice(s * LANES, (s + 1) * LANES)
            z_ref[s, 8:, :] = z[:, cols]
            z_ref[s, :8, :] = z_prev[:, cols]
            taps.append(cw_ref[0:1, cols] * z_ref[s, 6:6 + tm, :]
                        + cw_ref[1:2, cols] * z_ref[s, 7:7 + tm, :]
                        + cw_ref[2:3, cols] * z[:, cols] + cb_ref[:, cols])
        return jnp.concatenate(taps, axis=1)

    gate = conv(wg_ref, zg_ref, cg_ref, bg_ref)
    val = conv(wv_ref, zv_ref, cv_ref, bv_ref)
    a_ref[...] = (_gelu_tanh(gate) * val).astype(BF16)


def _ffn_up(h, w_up, w_conv, b_conv):
    tm, tn, halo = UP_TM, UP_TN, CONV_HALO
    nj = D_FF // tn
    halo_blocks = tm // halo
    return pl.pallas_call(
        _up_kernel,
        grid=(M_ROWS // tm, nj),
        in_specs=[
            pl.BlockSpec((tm, D_MODEL), lambda i, j: (i, 0)),
            pl.BlockSpec((halo, D_MODEL), lambda i, j: (jnp.maximum(i * halo_blocks - 1, 0), 0)),
            pl.BlockSpec((D_MODEL, tn), lambda i, j: (0, j)),
            pl.BlockSpec((D_MODEL, tn), lambda i, j: (0, nj + j)),
            pl.BlockSpec((3, tn), lambda i, j: (0, j)),
            pl.BlockSpec((3, tn), lambda i, j: (0, nj + j)),
            pl.BlockSpec((1, tn), lambda i, j: (0, j)),
            pl.BlockSpec((1, tn), lambda i, j: (0, nj + j)),
        ],
        out_specs=pl.BlockSpec((tm, tn), lambda i, j: (i, j)),
        out_shape=jax.ShapeDtypeStruct((M_ROWS, D_FF), BF16),
        scratch_shapes=[pltpu.VMEM((tn // LANES, tm + 8, LANES), F32) for _ in range(2)],
        compiler_params=_params("arbitrary", "arbitrary"),
        name="ffn_up",
    )(h, h, w_up, w_up, w_conv, w_conv, b_conv, b_conv)


def _down_kernel(emit_h, a_ref, w_ref, x_ref, g_ref, gt_ref, *rest):
    if emit_h:
        g2_ref, sc2_ref, sh2_ref, o_ref, h_ref = rest
    else:
        (o_ref,) = rest
    for r0 in range(0, DOWN_TM, DOWN_ROWS):
        rows = slice(r0, r0 + DOWN_ROWS)
        y = _dot(a_ref[rows, :], w_ref[...])
        x_new = x_ref[rows, :] + _rms_norm(y, g_ref[...] * gt_ref[0])
        o_ref[rows, :] = x_new
        if emit_h:
            h_ref[rows, :] = _modulated_norm(x_new, g2_ref[...], sc2_ref[0], sh2_ref[0])


def _ffn_down(a, w_down, x, g, gate, next_norm=None):
    tm = DOWN_TM
    tiles_per_seq = SEQ // tm
    rows = pl.BlockSpec((tm, D_MODEL), lambda i: (i, 0))
    vec = pl.BlockSpec((1, D_MODEL), lambda i: (0, 0))
    per_batch = pl.BlockSpec((1, 1, D_MODEL), lambda i: (i // tiles_per_seq, 0, 0))
    emit_h = next_norm is not None
    in_specs = [pl.BlockSpec((tm, D_FF), lambda i: (i, 0)),
                pl.BlockSpec((D_FF, D_MODEL), lambda i: (0, 0), pipeline_mode=pl.Buffered(1)),
                rows, vec, per_batch]
    out_specs = [rows]
    out_shape = [jax.ShapeDtypeStruct((M_ROWS, D_MODEL), F32)]
    args = [a, w_down, x, g, gate]
    if emit_h:
        in_specs += [vec, per_batch, per_batch]
        out_specs.append(rows)
        out_shape.append(jax.ShapeDtypeStruct((M_ROWS, D_MODEL), BF16))
        args += list(next_norm)
    return pl.pallas_call(
        functools.partial(_down_kernel, emit_h),
        grid=(M_ROWS // tm,),
        in_specs=in_specs,
        out_specs=out_specs,
        out_shape=out_shape,
        compiler_params=_params("arbitrary"),
        name="ffn_down",
    )(*args)


def kernel(x, c, g_pre_mix, g_post_mix, g_pre_ffn, g_post_ffn, w_ada, b_ada, w_in, b_in,
           sgu_norm_g, sgu_w, sgu_b, pool_w, pool_scale, w_sgu_out, w_moba_out, w_pool_out,
           w_out, w_up, w_conv, b_conv, w_down):
    c_pad = jnp.pad(c, ((0, 8 - BATCH), (0, 0)))
    mod = _modulation(c_pad, w_ada, b_ada)[:, :BATCH]
    mod = mod.reshape(DEPTH, BATCH, N_MOD, 1, D_MODEL)
    shift1, scale1, gate1, shift2, scale2, gate2 = [mod[:, :, n] for n in range(N_MOD)]
    row = lambda v, l: v[l].reshape(1, -1)
    qkv_cols = slice(2 * BRANCH_WIDTH, 6 * BRANCH_WIDTH)

    xf = x.reshape(M_ROWS, D_MODEL)
    h = _prenorm(xf, row(g_pre_mix, 0), scale1[0], shift1[0])
    for l in range(DEPTH):
        w_in_l = w_in[l].astype(BF16)
        uvg = _uvg_proj(h, w_in_l, row(b_in, l))
        qkvz, kmean, v_t, k_aug = _qkv_proj(h, w_in_l[:, qkv_cols],
                                            b_in[l, qkv_cols].reshape(1, -1))
        kmean = kmean.reshape(BATCH, N_KV_BLOCKS, BRANCH_WIDTH)

        sgu_bias_full = jnp.repeat(sgu_b[l].T, SGU_GROUP_DIM, axis=1)
        ysgu, ypool = _branches(uvg, qkvz, row(sgu_norm_g, l), sgu_w[l], sgu_bias_full, pool_w[l],
                                row(pool_scale, l))
        ymoba = _attention(qkvz, k_aug, kmean, v_t)
        xf, h = _merge(ysgu, ymoba, ypool, uvg, w_sgu_out[l].astype(BF16),
                       w_moba_out[l].astype(BF16), w_pool_out[l].astype(BF16),
                       w_out[l].astype(BF16), xf, row(g_post_mix, l), gate1[l],
                       row(g_pre_ffn, l), scale2[l], shift2[l])

        a = _ffn_up(h, w_up[l].astype(BF16), w_conv[l], row(b_conv, l))
        if l + 1 < DEPTH:
            xf, h = _ffn_down(a, w_down[l].astype(BF16), xf, row(g_post_ffn, l), gate2[l],
                              (row(g_pre_mix, l + 1), scale1[l + 1], shift1[l + 1]))
        else:
            (xf,) = _ffn_down(a, w_down[l].astype(BF16), xf, row(g_post_ffn, l), gate2[l])
    return xf.reshape(BATCH, SEQ, D_MODEL)
```

```python
import functools

import jax
import jax.numpy as jnp
import numpy as np
from jax import lax
from jax.experimental import pallas as pl
from jax.experimental.pallas import tpu as pltpu

F32 = jnp.float32
BF16 = jnp.bfloat16

D_MODEL = 2048
BATCH = 4
SEQ = 8192
DEPTH = 2
EPS = 1e-6
BRANCH_WIDTH = 512
SGU_GROUPS = 8
SGU_GROUP_DIM = 64
SGU_CHUNK = 128
MOBA_HEADS = 8
MOBA_HEAD_DIM = 64
MOBA_BLOCK = 256
MOBA_TOPK = 3
POOL_WINDOWS = (2, 4, 8, 16)
POOL_GROUPS = 4
POOL_GROUP_DIM = 128
IN_WIDTH = 6 * BRANCH_WIDTH + 3 * D_MODEL
D_FF = 5632
N_MOD = 6
M_ROWS = BATCH * SEQ
N_KV_BLOCKS = SEQ // MOBA_BLOCK

VMEM_LIMIT_BYTES = 56 * 1024 * 1024
LANES = 128
SUBLANES = 8
NEG = -1e30
LOG2E = 1.4426950408889634
ALIBI_PARTS = 3


def _bf16_parts(value, n):
    parts = []
    for _ in range(n):
        parts.append(float(np.asarray(value - sum(parts), dtype=jnp.bfloat16)))
    return tuple(parts)


LOG2E_PARTS = _bf16_parts(LOG2E, ALIBI_PARTS)
ONES_ROWS = 16

MOD_TN = 1024
PRENORM_TM = 512
QKV_TM = 1024
UVG_TM, UVG_TN = 2048, 1024
UVG_ROWS = 512
BRANCH_TR = 512
ATTN_TQ = MOBA_BLOCK
MERGE_TM = 512
MERGE_ROWS = 256
UP_TM, UP_TN = 1024, 512
DOWN_TM = 256
DOWN_ROWS = 128
POOL_HALO = 16


GELU_C = 0.7978845608028654
GELU_A = 0.044715


def _sigmoid(x):
    return 1.0 / (1.0 + jnp.exp(-x))


def _sigmoid2(x):
    return 1.0 / (1.0 + jnp.exp2(-LOG2E * x))


def _gelu_tanh(x):
    k1 = -2.0 * GELU_C * LOG2E
    return x / (1.0 + jnp.exp2(x * (k1 + (k1 * GELU_A) * (x * x))))


def _rms_norm(x, g):
    ms = jnp.mean(x * x, axis=-1, keepdims=True)
    return (x * lax.rsqrt(ms + EPS)) * g


def _modulated_norm(x, g, scale, shift):
    return (_rms_norm(x, g * (1.0 + scale)) + shift).astype(BF16)


def _dot(a, b):
    return jnp.dot(a, b, preferred_element_type=F32)


def _params(*sem):
    return pltpu.CompilerParams(dimension_semantics=sem, vmem_limit_bytes=VMEM_LIMIT_BYTES)


def _mod_kernel(c_ref, w_ref, b_ref, o_ref):
    c = c_ref[...]
    cond = (c * _sigmoid(c)).astype(BF16)
    o_ref[0] = _dot(cond, w_ref[0].astype(BF16)) + b_ref[0]


def _modulation(c_pad, w_ada, b_ada):
    rows = c_pad.shape[0]
    n = N_MOD * D_MODEL
    return pl.pallas_call(
        _mod_kernel,
        grid=(DEPTH, n // MOD_TN),
        in_specs=[
            pl.BlockSpec((rows, D_MODEL), lambda l, j: (0, 0)),
            pl.BlockSpec((1, D_MODEL, MOD_TN), lambda l, j: (l, 0, j)),
            pl.BlockSpec((1, 1, MOD_TN), lambda l, j: (l, 0, j)),
        ],
        out_specs=pl.BlockSpec((1, rows, MOD_TN), lambda l, j: (l, 0, j)),
        out_shape=jax.ShapeDtypeStruct((DEPTH, rows, n), F32),
        compiler_params=_params("arbitrary", "arbitrary"),
        name="adaln_mod",
    )(c_pad, w_ada, b_ada.reshape(DEPTH, 1, n))


def _prenorm_kernel(x_ref, g_ref, sc_ref, sh_ref, h_ref):
    h_ref[...] = _modulated_norm(x_ref[...], g_ref[...], sc_ref[0], sh_ref[0])


def _prenorm(x, g, scale, shift):
    tm = PRENORM_TM
    tiles_per_seq = SEQ // tm
    return pl.pallas_call(
        _prenorm_kernel,
        grid=(M_ROWS // tm,),
        in_specs=[
            pl.BlockSpec((tm, D_MODEL), lambda i: (i, 0)),
            pl.BlockSpec((1, D_MODEL), lambda i: (0, 0)),
            pl.BlockSpec((1, 1, D_MODEL), lambda i: (i // tiles_per_seq, 0, 0)),
            pl.BlockSpec((1, 1, D_MODEL), lambda i: (i // tiles_per_seq, 0, 0)),
        ],
        out_specs=pl.BlockSpec((tm, D_MODEL), lambda i: (i, 0)),
        out_shape=jax.ShapeDtypeStruct((M_ROWS, D_MODEL), BF16),
        compiler_params=_params("arbitrary"),
        name="prenorm",
    )(x, g, scale, shift)


def _qkv_kernel(h_ref, w_ref, b_ref, o_ref, kmean_ref, vt_ref, kaug_ref):
    acc = _dot(h_ref[...], w_ref[...]) + b_ref[...]
    o_ref[...] = acc.astype(BF16)
    hd = MOBA_HEAD_DIM
    lane = lax.broadcasted_iota(jnp.int32, (QKV_TM, LANES - hd), 1)
    row_in_block = (lax.broadcasted_iota(jnp.int32, (QKV_TM, LANES - hd), 0)
                    % MOBA_BLOCK).astype(F32)
    groups = []
    for h in range(MOBA_HEADS):
        k_h = acc[:, BRANCH_WIDTH + h * hd:BRANCH_WIDTH + (h + 1) * hd]
        pos = jnp.where(lane < ALIBI_PARTS, (2.0 ** -(h + 1)) * row_in_block, 0.0)
        groups += [k_h, pos]
    kaug_ref[...] = jnp.concatenate(groups, axis=1).astype(BF16)
    for r in range(QKV_TM // MOBA_BLOCK):
        rows = slice(r * MOBA_BLOCK, (r + 1) * MOBA_BLOCK)
        kmean_ref[0, r:r + 1, :] = jnp.mean(acc[rows, BRANCH_WIDTH:2 * BRANCH_WIDTH],
                                            axis=0, keepdims=True)
        vt_ref[r] = jnp.transpose(acc[rows, 2 * BRANCH_WIDTH:3 * BRANCH_WIDTH]).astype(BF16)


def _qkv_proj(h, w, b):
    tm = QKV_TM
    n = 4 * BRANCH_WIDTH
    blocks_per_tile = tm // MOBA_BLOCK
    return pl.pallas_call(
        _qkv_kernel,
        grid=(M_ROWS // tm,),
        in_specs=[
            pl.BlockSpec((tm, D_MODEL), lambda i: (i, 0)),
            pl.BlockSpec((D_MODEL, n), lambda i: (0, 0), pipeline_mode=pl.Buffered(1)),
            pl.BlockSpec((1, n), lambda i: (0, 0)),
        ],
        out_specs=[
            pl.BlockSpec((tm, n), lambda i: (i, 0)),
            pl.BlockSpec((1, blocks_per_tile, BRANCH_WIDTH), lambda i: (i, 0, 0)),
            pl.BlockSpec((blocks_per_tile, BRANCH_WIDTH, MOBA_BLOCK), lambda i: (i, 0, 0)),
            pl.BlockSpec((tm, MOBA_HEADS * LANES), lambda i: (i, 0)),
        ],
        out_shape=[
            jax.ShapeDtypeStruct((M_ROWS, n), BF16),
            jax.ShapeDtypeStruct((M_ROWS // tm, blocks_per_tile, BRANCH_WIDTH), F32),
            jax.ShapeDtypeStruct((M_ROWS // MOBA_BLOCK, BRANCH_WIDTH, MOBA_BLOCK), BF16),
            jax.ShapeDtypeStruct((M_ROWS, MOBA_HEADS * LANES), BF16),
        ],
        compiler_params=_params("arbitrary"),
        name="qkv_proj",
    )(h, w, b)


def _uvg_kernel(h_ref, w_ref, b_ref, o_ref):
    j = pl.program_id(1)

    def project(act):
        for r0 in range(0, UVG_TM, UVG_ROWS):
            rows = slice(r0, r0 + UVG_ROWS)
            o_ref[rows, :] = act(_dot(h_ref[rows, :], w_ref[...]) + b_ref[...]).astype(BF16)

    @pl.when(j == 0)
    def _():
        project(_gelu_tanh)

    @pl.when(j > 0)
    def _():
        project(_sigmoid2)


def _uvg_proj(h, w_in, b_in):
    tm, tn = UVG_TM, UVG_TN
    n = 2 * BRANCH_WIDTH + 3 * D_MODEL
    skipped = 4 * BRANCH_WIDTH // tn
    src = lambda j: jnp.where(j == 0, 0, j + skipped)
    return pl.pallas_call(
        _uvg_kernel,
        grid=(M_ROWS // tm, n // tn),
        in_specs=[
            pl.BlockSpec((tm, D_MODEL), lambda i, j: (i, 0)),
            pl.BlockSpec((D_MODEL, tn), lambda i, j: (0, src(j))),
            pl.BlockSpec((1, tn), lambda i, j: (0, src(j))),
        ],
        out_specs=pl.BlockSpec((tm, tn), lambda i, j: (i, j)),
        out_shape=jax.ShapeDtypeStruct((M_ROWS, n), BF16),
        compiler_params=_params("arbitrary", "arbitrary"),
        name="uvg_proj",
    )(h, w_in, b_in)


def _branch_kernel(u_ref, v_ref, z_ref, zprev_ref, ng_ref, sw_ref, sb_ref, pw_ref, ps_ref,
                   ysgu_ref, ypool_ref):
    i = pl.program_id(0)
    tr = BRANCH_TR
    first = (i % (SEQ // tr)) == 0

    v = v_ref[...].astype(F32)
    mu = jnp.mean(v, axis=-1, keepdims=True)
    vc = v - mu
    var = jnp.mean(vc * vc, axis=-1, keepdims=True)
    vn = (vc * lax.rsqrt(var + EPS) * ng_ref[...]).astype(BF16)

    row = lax.broadcasted_iota(jnp.int32, (SGU_CHUNK, SGU_CHUNK), 0)
    col = lax.broadcasted_iota(jnp.int32, (SGU_CHUNK, SGU_CHUNK), 1)
    causal = row >= col
    wm = [jnp.where(causal, sw_ref[g], 0.0).astype(BF16) for g in range(SGU_GROUPS)]
    lane = lax.broadcasted_iota(jnp.int32, (SGU_CHUNK, LANES), 1)
    low_half = lane < SGU_GROUP_DIM

    n_chunks = tr // SGU_CHUNK
    for pr in range(SGU_GROUPS // 2):
        cols = slice(pr * LANES, (pr + 1) * LANES)
        blk = jnp.concatenate([vn[c * SGU_CHUNK:(c + 1) * SGU_CHUNK, cols] for c in range(n_chunks)],
                              axis=1)
        low = jnp.concatenate([low_half] * n_chunks, axis=1)
        mixed = (_dot(wm[2 * pr], jnp.where(low, blk, jnp.zeros_like(blk)))
                 + _dot(wm[2 * pr + 1], jnp.where(low, jnp.zeros_like(blk), blk)))
        for c in range(n_chunks):
            rows = slice(c * SGU_CHUNK, (c + 1) * SGU_CHUNK)
            m_c = mixed[:, c * LANES:(c + 1) * LANES] + sb_ref[:, cols]
            ysgu_ref[rows, cols] = (u_ref[rows, cols].astype(F32) * m_c).astype(BF16)

    z = z_ref[...].astype(F32)
    zprev = jnp.where(first, 0.0, zprev_ref[...].astype(F32))
    pos = (i % (SEQ // tr)) * tr + lax.broadcasted_iota(jnp.int32, (tr, POOL_GROUP_DIM), 0)
    for gi, w in enumerate(POOL_WINDOWS):
        cols = slice(gi * POOL_GROUP_DIM, (gi + 1) * POOL_GROUP_DIM)
        zg = z[:, cols]
        cur = jnp.concatenate([zprev[:, cols], zg], axis=0)
        d = 1
        while d < w:
            cur = cur[d:] + cur[:-d]
            d *= 2
        win_sum = cur[cur.shape[0] - tr:]
        count = jnp.minimum(pos + 1, w).astype(F32)
        pooled = win_sum / count - zg
        y = _dot(pooled.astype(BF16), pw_ref[gi].astype(BF16)) * ps_ref[:, cols]
        ypool_ref[:, cols] = y.astype(BF16)


def _branches(uvg, qkvz, sgu_norm_g, sgu_w, sgu_bias_full, pool_w, pool_scale):
    tr = BRANCH_TR
    halo_blocks = tr // POOL_HALO
    return pl.pallas_call(
        _branch_kernel,
        grid=(M_ROWS // tr,),
        in_specs=[
            pl.BlockSpec((tr, BRANCH_WIDTH), lambda i: (i, 0)),
            pl.BlockSpec((tr, BRANCH_WIDTH), lambda i: (i, 1)),
            pl.BlockSpec((tr, BRANCH_WIDTH), lambda i: (i, 3)),
            pl.BlockSpec((POOL_HALO, BRANCH_WIDTH),
                         lambda i: (jnp.maximum(i * halo_blocks - 1, 0), 3)),
            pl.BlockSpec((1, BRANCH_WIDTH), lambda i: (0, 0)),
            pl.BlockSpec((SGU_GROUPS, SGU_CHUNK, SGU_CHUNK), lambda i: (0, 0, 0)),
            pl.BlockSpec((SGU_CHUNK, BRANCH_WIDTH), lambda i: (0, 0)),
            pl.BlockSpec((POOL_GROUPS, POOL_GROUP_DIM, POOL_GROUP_DIM), lambda i: (0, 0, 0)),
            pl.BlockSpec((1, BRANCH_WIDTH), lambda i: (0, 0)),
        ],
        out_specs=[
            pl.BlockSpec((tr, BRANCH_WIDTH), lambda i: (i, 0)),
            pl.BlockSpec((tr, BRANCH_WIDTH), lambda i: (i, 0)),
        ],
        out_shape=[
            jax.ShapeDtypeStruct((M_ROWS, BRANCH_WIDTH), BF16),
            jax.ShapeDtypeStruct((M_ROWS, BRANCH_WIDTH), BF16),
        ],
        compiler_params=_params("arbitrary"),
        name="sgu_pool",
    )(uvg, uvg, qkvz, qkvz, sgu_norm_g, sgu_w, sgu_bias_full, pool_w, pool_scale)


def _attn_kernel(q_ref, k_ref, vt_ref, km_ref, o_ref,
                 qs_ref, pick_ref, m_ref, acc_ref, s_ref, smax_ref):
    own = pl.program_id(1)
    tq, blk, hd = ATTN_TQ, MOBA_BLOCK, MOBA_HEAD_DIM
    own_f = own.astype(F32)

    key_row = lax.broadcasted_iota(jnp.int32, (blk, tq), 0)
    causal = jnp.where(key_row > lax.broadcasted_iota(jnp.int32, (blk, tq), 1), NEG, 0.0)
    blk_id = lax.broadcasted_iota(jnp.int32, (N_KV_BLOCKS, tq), 0)
    blk_id_f = blk_id.astype(F32)
    past = blk_id < own
    ones_rows = jnp.ones((ONES_ROWS, blk), BF16)
    part_row = lax.broadcasted_iota(jnp.int32, (LANES - hd, tq), 0)
    q_alibi = jnp.zeros((LANES - hd, tq), F32)
    for t, part in enumerate(LOG2E_PARTS):
        q_alibi = jnp.where(part_row == t, part, q_alibi)

    q_t = jnp.transpose(q_ref[...].astype(F32))

    def raw_scores(j, h):
        start = pl.multiple_of(j * blk, blk)
        return _dot(k_ref[pl.ds(start, blk), h * LANES:(h + 1) * LANES], qs_ref[h])

    def values(j, h):
        return jnp.concatenate([vt_ref[j, h * hd:(h + 1) * hd, :], ones_rows], axis=0)

    gate_scores = []
    for h in range(MOBA_HEADS):
        qh = q_t[h * hd:(h + 1) * hd]
        qs_ref[h] = jnp.concatenate([qh * (hd ** -0.5 * LOG2E), q_alibi], axis=0).astype(BF16)
        km = km_ref[0, :, h * hd:(h + 1) * hd]
        km_hi = km.astype(BF16)
        km_lo = (km - km_hi.astype(F32)).astype(BF16)
        qg = qh.astype(BF16)
        sc = _dot(km_hi, qg) + _dot(km_lo, qg)
        gate_scores.append(jnp.where(past, sc, -jnp.inf))

    for h in range(MOBA_HEADS):
        s = raw_scores(own, h) + causal
        s_ref[1, h] = s
        m_ref[h] = jnp.max(s, axis=0, keepdims=True)

    for r in range(MOBA_TOPK):
        for h in range(MOBA_HEADS):
            sc = gate_scores[h]
            best = jnp.max(sc, axis=0, keepdims=True)
            idx = jnp.min(jnp.where(sc == best, blk_id_f, 1e9), axis=0, keepdims=True)
            pick_ref[h, r] = jnp.where(best > -jnp.inf, idx, -1.0)
            gate_scores[h] = jnp.where(blk_id_f == idx, -jnp.inf, sc)

    for h in range(MOBA_HEADS):
        acc_ref[h] = _dot(values(own, h), jnp.exp2(s_ref[1, h] - m_ref[h]).astype(BF16))

    def scores(j, h, slot):
        s = raw_scores(j, h)
        s_ref[slot, h] = s
        smax_ref[slot, h] = jnp.max(s, axis=0, keepdims=True)

    def absorb(j, h, slot):
        jf = j.astype(F32)
        slope = 2.0 ** -(h + 1)
        chosen = (pick_ref[h, 0] == jf) | (pick_ref[h, 1] == jf) | (pick_ref[h, 2] == jf)
        shift = jnp.where(chosen, (-slope * blk * LOG2E) * (own_f - jf), NEG)
        m = m_ref[h]
        m_new = jnp.maximum(m, smax_ref[slot, h] + shift)
        p = jnp.exp2(s_ref[slot, h] - (m_new - shift))
        m_ref[h] = m_new
        acc_ref[h] = jnp.exp2(m - m_new) * acc_ref[h] + _dot(values(j, h), p.astype(BF16))

    for h in range(MOBA_HEADS):
        scores(0, h, 0)

    def body(i, carry):
        a = 2 * i
        b = jnp.minimum(a + 1, own)
        c = jnp.minimum(a + 2, own)
        for h in range(MOBA_HEADS):
            scores(b, h, 1)
            absorb(a, h, 0)
        for h in range(MOBA_HEADS):
            scores(c, h, 0)
            absorb(b, h, 1)
        return carry

    lax.fori_loop(0, (own + 1) // 2, body, 0)

    o_t = jnp.concatenate([acc_ref[h, :hd] / acc_ref[h, hd:hd + 1] for h in range(MOBA_HEADS)],
                          axis=0)
    o_ref[...] = jnp.transpose(o_t).astype(BF16)


def _attention(qkvz, k_aug, kmean, v_t):
    tq = ATTN_TQ
    q_tiles = SEQ // tq
    return pl.pallas_call(
        _attn_kernel,
        grid=(BATCH, q_tiles),
        in_specs=[
            pl.BlockSpec((tq, BRANCH_WIDTH), lambda b, i: (b * q_tiles + i, 0)),
            pl.BlockSpec((SEQ, MOBA_HEADS * LANES), lambda b, i: (b, 0),
                         pipeline_mode=pl.Buffered(1)),
            pl.BlockSpec((N_KV_BLOCKS, BRANCH_WIDTH, MOBA_BLOCK), lambda b, i: (b, 0, 0)),
            pl.BlockSpec((1, N_KV_BLOCKS, BRANCH_WIDTH), lambda b, i: (b, 0, 0)),
        ],
        out_specs=pl.BlockSpec((tq, BRANCH_WIDTH), lambda b, i: (b * q_tiles + i, 0)),
        out_shape=jax.ShapeDtypeStruct((M_ROWS, BRANCH_WIDTH), BF16),
        scratch_shapes=[
            pltpu.VMEM((MOBA_HEADS, LANES, tq), BF16),
            pltpu.VMEM((MOBA_HEADS, MOBA_TOPK, 1, tq), F32),
            pltpu.VMEM((MOBA_HEADS, 1, tq), F32),
            pltpu.VMEM((MOBA_HEADS, MOBA_HEAD_DIM + ONES_ROWS, tq), F32),
            pltpu.VMEM((2, MOBA_HEADS, MOBA_BLOCK, tq), F32),
            pltpu.VMEM((2, MOBA_HEADS, 1, tq), F32),
        ],
        compiler_params=_params("arbitrary", "arbitrary"),
        name="moba_attention",
    )(qkvz, k_aug, v_t, kmean)


def _merge_kernel(ys_ref, ym_ref, yp_ref, gs0, gs1, gm0, gm1, gp0, gp1,
                  ws_ref, wm_ref, wp_ref, wo_ref, x_ref, g_ref, gt_ref, g2_ref, sc2_ref, sh2_ref,
                  o_ref, h_ref):
    half_w = D_MODEL // 2
    for r0 in range(0, MERGE_TM, MERGE_ROWS):
        rows = slice(r0, r0 + MERGE_ROWS)
        ys, ym, yp = ys_ref[rows, :], ym_ref[rows, :], yp_ref[rows, :]
        halves = []
        for hf, (gs, gm, gp) in enumerate(((gs0, gm0, gp0), (gs1, gm1, gp1))):
            cols = slice(hf * half_w, (hf + 1) * half_w)
            merged = (gs[rows, :].astype(F32) * _dot(ys, ws_ref[:, cols])
                      + gm[rows, :].astype(F32) * _dot(ym, wm_ref[:, cols])
                      + gp[rows, :].astype(F32) * _dot(yp, wp_ref[:, cols]))
            halves.append(merged.astype(BF16))
        y = _dot(jnp.concatenate(halves, axis=1), wo_ref[...])
        x_new = x_ref[rows, :] + _rms_norm(y, g_ref[...] * gt_ref[0])
        o_ref[rows, :] = x_new
        h_ref[rows, :] = _modulated_norm(x_new, g2_ref[...], sc2_ref[0], sh2_ref[0])


def _merge(ysgu, ymoba, ypool, uvg, w_sgu_out, w_moba_out, w_pool_out, w_out, x, g, gate,
           g_next, scale_next, shift_next):
    tm = MERGE_TM
    tiles_per_seq = SEQ // tm
    half_w = D_MODEL // 2
    gate_col0 = 2 * BRANCH_WIDTH // half_w
    resident = functools.partial(pl.BlockSpec, pipeline_mode=pl.Buffered(1))
    branch = pl.BlockSpec((tm, BRANCH_WIDTH), lambda i: (i, 0))
    gate_specs = [pl.BlockSpec((tm, half_w), functools.partial(lambda i, c: (i, c), c=gate_col0 + c))
                  for c in range(6)]
    w_branch = resident((BRANCH_WIDTH, D_MODEL), lambda i: (0, 0))
    rows = pl.BlockSpec((tm, D_MODEL), lambda i: (i, 0))
    vec = pl.BlockSpec((1, D_MODEL), lambda i: (0, 0))
    per_batch = pl.BlockSpec((1, 1, D_MODEL), lambda i: (i // tiles_per_seq, 0, 0))
    return pl.pallas_call(
        _merge_kernel,
        grid=(M_ROWS // tm,),
        in_specs=[branch, branch, branch, *gate_specs, w_branch, w_branch, w_branch,
                  resident((D_MODEL, D_MODEL), lambda i: (0, 0)),
                  rows, vec, per_batch, vec, per_batch, per_batch],
        out_specs=[rows, rows],
        out_shape=[jax.ShapeDtypeStruct((M_ROWS, D_MODEL), F32),
                   jax.ShapeDtypeStruct((M_ROWS, D_MODEL), BF16)],
        compiler_params=_params("arbitrary"),
        name="merge_outproj",
    )(ysgu, ymoba, ypool, uvg, uvg, uvg, uvg, uvg, uvg, w_sgu_out, w_moba_out, w_pool_out, w_out,
      x, g, gate, g_next, scale_next, shift_next)


def _up_kernel(h_ref, wg_ref, wv_ref, cg_ref, cv_ref, bg_ref, bv_ref, a_ref,
               zg_ref, zv_ref, tail_g_ref, tail_v_ref):
    tm, sub = UP_TM, SUBLANES
    j = pl.program_id(1)

    @pl.when((pl.program_id(0) % (SEQ // tm)) == 0)
    def _():
        tail_g_ref[j] = jnp.zeros(tail_g_ref.shape[1:], F32)
        tail_v_ref[j] = jnp.zeros(tail_v_ref.shape[1:], F32)

    h = h_ref[...]

    def conv(w_ref, z_ref, tail_ref, cw_ref, cb_ref):
        z = _dot(h, w_ref[...])
        taps = []
        for s in range(UP_TN // LANES):
            cols = slice(s * LANES, (s + 1) * LANES)
            z_ref[s, :sub, :] = tail_ref[j, s]
            z_ref[s, sub:, :] = z[:, cols]
            tail_ref[j, s] = z[tm - sub:, cols]
            taps.append(cw_ref[0:1, cols] * z_ref[s, sub - 2:sub - 2 + tm, :]
                        + cw_ref[1:2, cols] * z_ref[s, sub - 1:sub - 1 + tm, :]
                        + cw_ref[2:3, cols] * z[:, cols] + cb_ref[:, cols])
        return jnp.concatenate(taps, axis=1)

    gate = conv(wg_ref, zg_ref, tail_g_ref, cg_ref, bg_ref)
    val = conv(wv_ref, zv_ref, tail_v_ref, cv_ref, bv_ref)
    a_ref[...] = (_gelu_tanh(gate) * val).astype(BF16)


def _ffn_up(h, w_up, w_conv, b_conv):
    tm, tn = UP_TM, UP_TN
    nj = D_FF // tn
    slabs = tn // LANES
    return pl.pallas_call(
        _up_kernel,
        grid=(M_ROWS // tm, nj),
        in_specs=[
            pl.BlockSpec((tm, D_MODEL), lambda i, j: (i, 0)),
            pl.BlockSpec((D_MODEL, tn), lambda i, j: (0, j)),
            pl.BlockSpec((D_MODEL, tn), lambda i, j: (0, nj + j)),
            pl.BlockSpec((3, tn), lambda i, j: (0, j)),
            pl.BlockSpec((3, tn), lambda i, j: (0, nj + j)),
            pl.BlockSpec((1, tn), lambda i, j: (0, j)),
            pl.BlockSpec((1, tn), lambda i, j: (0, nj + j)),
        ],
        out_specs=pl.BlockSpec((tm, tn), lambda i, j: (i, j)),
        out_shape=jax.ShapeDtypeStruct((M_ROWS, D_FF), BF16),
        scratch_shapes=([pltpu.VMEM((slabs, tm + SUBLANES, LANES), F32) for _ in range(2)]
                        + [pltpu.VMEM((nj, slabs, SUBLANES, LANES), F32) for _ in range(2)]),
        compiler_params=_params("arbitrary", "arbitrary"),
        name="ffn_up",
    )(h, w_up, w_up, w_conv, w_conv, b_conv, b_conv)


def _down_kernel(emit_h, a_ref, w_ref, x_ref, g_ref, gt_ref, *rest):
    if emit_h:
        g2_ref, sc2_ref, sh2_ref, o_ref, h_ref = rest
    else:
        (o_ref,) = rest
    for r0 in range(0, DOWN_TM, DOWN_ROWS):
        rows = slice(r0, r0 + DOWN_ROWS)
        y = _dot(a_ref[rows, :], w_ref[...])
        x_new = x_ref[rows, :] + _rms_norm(y, g_ref[...] * gt_ref[0])
        o_ref[rows, :] = x_new
        if emit_h:
            h_ref[rows, :] = _modulated_norm(x_new, g2_ref[...], sc2_ref[0], sh2_ref[0])


def _ffn_down(a, w_down, x, g, gate, next_norm=None):
    tm = DOWN_TM
    tiles_per_seq = SEQ // tm
    rows = pl.BlockSpec((tm, D_MODEL), lambda i: (i, 0))
    vec = pl.BlockSpec((1, D_MODEL), lambda i: (0, 0))
    per_batch = pl.BlockSpec((1, 1, D_MODEL), lambda i: (i // tiles_per_seq, 0, 0))
    emit_h = next_norm is not None
    in_specs = [pl.BlockSpec((tm, D_FF), lambda i: (i, 0)),
                pl.BlockSpec((D_FF, D_MODEL), lambda i: (0, 0), pipeline_mode=pl.Buffered(1)),
                rows, vec, per_batch]
    out_specs = [rows]
    out_shape = [jax.ShapeDtypeStruct((M_ROWS, D_MODEL), F32)]
    args = [a, w_down, x, g, gate]
    if emit_h:
        in_specs += [vec, per_batch, per_batch]
        out_specs.append(rows)
        out_shape.append(jax.ShapeDtypeStruct((M_ROWS, D_MODEL), BF16))
        args += list(next_norm)
    return pl.pallas_call(
        functools.partial(_down_kernel, emit_h),
        grid=(M_ROWS // tm,),
        in_specs=in_specs,
        out_specs=out_specs,
        out_shape=out_shape,
        compiler_params=_params("arbitrary"),
        name="ffn_down",
    )(*args)


def kernel(x, c, g_pre_mix, g_post_mix, g_pre_ffn, g_post_ffn, w_ada, b_ada, w_in, b_in,
           sgu_norm_g, sgu_w, sgu_b, pool_w, pool_scale, w_sgu_out, w_moba_out, w_pool_out,
           w_out, w_up, w_conv, b_conv, w_down):
    c_pad = jnp.pad(c, ((0, 8 - BATCH), (0, 0)))
    mod = _modulation(c_pad, w_ada, b_ada)[:, :BATCH]
    mod = mod.reshape(DEPTH, BATCH, N_MOD, 1, D_MODEL)
    shift1, scale1, gate1, shift2, scale2, gate2 = [mod[:, :, n] for n in range(N_MOD)]
    row = lambda v, l: v[l].reshape(1, -1)
    qkv_cols = slice(2 * BRANCH_WIDTH, 6 * BRANCH_WIDTH)

    xf = x.reshape(M_ROWS, D_MODEL)
    h = _prenorm(xf, row(g_pre_mix, 0), scale1[0], shift1[0])
    for l in range(DEPTH):
        w_in_l = w_in[l].astype(BF16)
        uvg = _uvg_proj(h, w_in_l, row(b_in, l))
        qkvz, kmean, v_t, k_aug = _qkv_proj(h, w_in_l[:, qkv_cols],
                                            b_in[l, qkv_cols].reshape(1, -1))
        kmean = kmean.reshape(BATCH, N_KV_BLOCKS, BRANCH_WIDTH)

        sgu_bias_full = jnp.repeat(sgu_b[l].T, SGU_GROUP_DIM, axis=1)
        ysgu, ypool = _branches(uvg, qkvz, row(sgu_norm_g, l), sgu_w[l], sgu_bias_full, pool_w[l],
                                row(pool_scale, l))
        ymoba = _attention(qkvz, k_aug, kmean, v_t)
        xf, h = _merge(ysgu, ymoba, ypool, uvg, w_sgu_out[l].astype(BF16),
                       w_moba_out[l].astype(BF16), w_pool_out[l].astype(BF16),
                       w_out[l].astype(BF16), xf, row(g_post_mix, l), gate1[l],
                       row(g_pre_ffn, l), scale2[l], shift2[l])

        a = _ffn_up(h, w_up[l].astype(BF16), w_conv[l], row(b_conv, l))
        if l + 1 < DEPTH:
            xf, h = _ffn_down(a, w_down[l].astype(BF16), xf, row(g_post_ffn, l), gate2[l],
                              (row(g_pre_mix, l + 1), scale1[l + 1], shift1[l + 1]))
        else:
            (xf,) = _ffn_down(a, w_down[l].astype(BF16), xf, row(g_post_ffn, l), gate2[l])
    return xf.reshape(BATCH, SEQ, D_MODEL)
```

```python
import functools

import jax
import jax.numpy as jnp
import numpy as np
from jax import lax
from jax.experimental import pallas as pl
from jax.experimental.pallas import tpu as pltpu

F32 = jnp.float32
BF16 = jnp.bfloat16

D_MODEL = 2048
BATCH = 4
SEQ = 8192
DEPTH = 2
EPS = 1e-6
BRANCH_WIDTH = 512
SGU_GROUPS = 8
SGU_GROUP_DIM = 64
SGU_CHUNK = 128
MOBA_HEADS = 8
MOBA_HEAD_DIM = 64
MOBA_BLOCK = 256
MOBA_TOPK = 3
POOL_WINDOWS = (2, 4, 8, 16)
POOL_GROUPS = 4
POOL_GROUP_DIM = 128
IN_WIDTH = 6 * BRANCH_WIDTH + 3 * D_MODEL
D_FF = 5632
N_MOD = 6
M_ROWS = BATCH * SEQ
N_KV_BLOCKS = SEQ // MOBA_BLOCK

VMEM_LIMIT_BYTES = 56 * 1024 * 1024
LANES = 128
SUBLANES = 8
NEG = -1e30
LOG2E = 1.4426950408889634
ALIBI_PARTS = 3


def _bf16_parts(value, n):
    parts = []
    for _ in range(n):
        parts.append(float(np.asarray(value - sum(parts), dtype=jnp.bfloat16)))
    return tuple(parts)


LOG2E_PARTS = _bf16_parts(LOG2E, ALIBI_PARTS)
ONES_ROWS = 16

MOD_TN = 1024
PRENORM_TM = 512
QKV_TM = 1024
UVG_TM, UVG_TN = 256, 1024
UVG_WIDTH = 2 * BRANCH_WIDTH + 3 * D_MODEL
BRANCH_TR = 512
ATTN_TQ = MOBA_BLOCK
MERGE_TM = 512
MERGE_ROWS = 256
UP_TM, UP_TN = 1024, 512
DOWN_TM = 256
DOWN_ROWS = 128
POOL_HALO = 16


GELU_C = 0.7978845608028654
GELU_A = 0.044715


def _sigmoid(x):
    return 1.0 / (1.0 + jnp.exp(-x))


def _sigmoid2(x):
    return 1.0 / (1.0 + jnp.exp2(-LOG2E * x))


def _gelu_tanh(x):
    k1 = -2.0 * GELU_C * LOG2E
    return x / (1.0 + jnp.exp2(x * (k1 + (k1 * GELU_A) * (x * x))))


def _rms_norm(x, g):
    ms = jnp.mean(x * x, axis=-1, keepdims=True)
    return (x * lax.rsqrt(ms + EPS)) * g


def _modulated_norm(x, g, scale, shift):
    return (_rms_norm(x, g * (1.0 + scale)) + shift).astype(BF16)


def _dot(a, b):
    return jnp.dot(a, b, preferred_element_type=F32)


def _params(*sem):
    return pltpu.CompilerParams(dimension_semantics=sem, vmem_limit_bytes=VMEM_LIMIT_BYTES)


def _mod_kernel(c_ref, w_ref, b_ref, o_ref):
    c = c_ref[...]
    cond = (c * _sigmoid(c)).astype(BF16)
    o_ref[0] = _dot(cond, w_ref[0].astype(BF16)) + b_ref[0]


def _modulation(c_pad, w_ada, b_ada):
    rows = c_pad.shape[0]
    n = N_MOD * D_MODEL
    return pl.pallas_call(
        _mod_kernel,
        grid=(DEPTH, n // MOD_TN),
        in_specs=[
            pl.BlockSpec((rows, D_MODEL), lambda l, j: (0, 0)),
            pl.BlockSpec((1, D_MODEL, MOD_TN), lambda l, j: (l, 0, j)),
            pl.BlockSpec((1, 1, MOD_TN), lambda l, j: (l, 0, j)),
        ],
        out_specs=pl.BlockSpec((1, rows, MOD_TN), lambda l, j: (l, 0, j)),
        out_shape=jax.ShapeDtypeStruct((DEPTH, rows, n), F32),
        compiler_params=_params("arbitrary", "arbitrary"),
        name="adaln_mod",
    )(c_pad, w_ada, b_ada.reshape(DEPTH, 1, n))


def _prenorm_kernel(x_ref, g_ref, sc_ref, sh_ref, h_ref):
    h_ref[...] = _modulated_norm(x_ref[...], g_ref[...], sc_ref[0], sh_ref[0])


def _prenorm(x, g, scale, shift):
    tm = PRENORM_TM
    tiles_per_seq = SEQ // tm
    return pl.pallas_call(
        _prenorm_kernel,
        grid=(M_ROWS // tm,),
        in_specs=[
            pl.BlockSpec((tm, D_MODEL), lambda i: (i, 0)),
            pl.BlockSpec((1, D_MODEL), lambda i: (0, 0)),
            pl.BlockSpec((1, 1, D_MODEL), lambda i: (i // tiles_per_seq, 0, 0)),
            pl.BlockSpec((1, 1, D_MODEL), lambda i: (i // tiles_per_seq, 0, 0)),
        ],
        out_specs=pl.BlockSpec((tm, D_MODEL), lambda i: (i, 0)),
        out_shape=jax.ShapeDtypeStruct((M_ROWS, D_MODEL), BF16),
        compiler_params=_params("arbitrary"),
        name="prenorm",
    )(x, g, scale, shift)


def _qkv_kernel(h_ref, w_ref, b_ref, o_ref, kmean_ref, vt_ref, kaug_ref):
    acc = _dot(h_ref[...], w_ref[...]) + b_ref[...]
    o_ref[...] = acc.astype(BF16)
    hd = MOBA_HEAD_DIM
    lane = lax.broadcasted_iota(jnp.int32, (QKV_TM, LANES - hd), 1)
    row_in_block = (lax.broadcasted_iota(jnp.int32, (QKV_TM, LANES - hd), 0)
                    % MOBA_BLOCK).astype(F32)
    groups = []
    for h in range(MOBA_HEADS):
        k_h = acc[:, BRANCH_WIDTH + h * hd:BRANCH_WIDTH + (h + 1) * hd]
        pos = jnp.where(lane < ALIBI_PARTS, (2.0 ** -(h + 1)) * row_in_block, 0.0)
        groups += [k_h, pos]
    kaug_ref[...] = jnp.concatenate(groups, axis=1).astype(BF16)
    for r in range(QKV_TM // MOBA_BLOCK):
        rows = slice(r * MOBA_BLOCK, (r + 1) * MOBA_BLOCK)
        kmean_ref[0, r:r + 1, :] = jnp.mean(acc[rows, BRANCH_WIDTH:2 * BRANCH_WIDTH],
                                            axis=0, keepdims=True)
        vt_ref[r] = jnp.transpose(acc[rows, 2 * BRANCH_WIDTH:3 * BRANCH_WIDTH]).astype(BF16)


def _qkv_proj(h, w, b):
    tm = QKV_TM
    n = 4 * BRANCH_WIDTH
    blocks_per_tile = tm // MOBA_BLOCK
    return pl.pallas_call(
        _qkv_kernel,
        grid=(M_ROWS // tm,),
        in_specs=[
            pl.BlockSpec((tm, D_MODEL), lambda i: (i, 0)),
            pl.BlockSpec((D_MODEL, n), lambda i: (0, 0), pipeline_mode=pl.Buffered(1)),
            pl.BlockSpec((1, n), lambda i: (0, 0)),
        ],
        out_specs=[
            pl.BlockSpec((tm, n), lambda i: (i, 0)),
            pl.BlockSpec((1, blocks_per_tile, BRANCH_WIDTH), lambda i: (i, 0, 0)),
            pl.BlockSpec((blocks_per_tile, BRANCH_WIDTH, MOBA_BLOCK), lambda i: (i, 0, 0)),
            pl.BlockSpec((tm, MOBA_HEADS * LANES), lambda i: (i, 0)),
        ],
        out_shape=[
            jax.ShapeDtypeStruct((M_ROWS, n), BF16),
            jax.ShapeDtypeStruct((M_ROWS // tm, blocks_per_tile, BRANCH_WIDTH), F32),
            jax.ShapeDtypeStruct((M_ROWS // MOBA_BLOCK, BRANCH_WIDTH, MOBA_BLOCK), BF16),
            jax.ShapeDtypeStruct((M_ROWS, MOBA_HEADS * LANES), BF16),
        ],
        compiler_params=_params("arbitrary"),
        name="qkv_proj",
    )(h, w, b)


def _uvg_kernel(h_ref, w_ref, b_ref, o_ref):
    h = h_ref[...]
    skipped = 4 * BRANCH_WIDTH
    for t in range(UVG_WIDTH // UVG_TN):
        src = t * UVG_TN if t == 0 else t * UVG_TN + skipped
        act = _gelu_tanh if t == 0 else _sigmoid2
        y = _dot(h, w_ref[:, src:src + UVG_TN]) + b_ref[:, src:src + UVG_TN]
        o_ref[:, t * UVG_TN:(t + 1) * UVG_TN] = act(y).astype(BF16)


def _uvg_proj(h, w_in, b_in):
    tm = UVG_TM
    return pl.pallas_call(
        _uvg_kernel,
        grid=(M_ROWS // tm,),
        in_specs=[
            pl.BlockSpec((tm, D_MODEL), lambda i: (i, 0)),
            pl.BlockSpec((D_MODEL, IN_WIDTH), lambda i: (0, 0), pipeline_mode=pl.Buffered(1)),
            pl.BlockSpec((1, IN_WIDTH), lambda i: (0, 0)),
        ],
        out_specs=pl.BlockSpec((tm, UVG_WIDTH), lambda i: (i, 0)),
        out_shape=jax.ShapeDtypeStruct((M_ROWS, UVG_WIDTH), BF16),
        compiler_params=_params("arbitrary"),
        name="uvg_proj",
    )(h, w_in, b_in)


def _branch_kernel(u_ref, v_ref, z_ref, zprev_ref, ng_ref, sw_ref, sb_ref, pw_ref, ps_ref,
                   ysgu_ref, ypool_ref):
    i = pl.program_id(0)
    tr = BRANCH_TR
    first = (i % (SEQ // tr)) == 0

    v = v_ref[...].astype(F32)
    mu = jnp.mean(v, axis=-1, keepdims=True)
    vc = v - mu
    var = jnp.mean(vc * vc, axis=-1, keepdims=True)
    vn = (vc * lax.rsqrt(var + EPS) * ng_ref[...]).astype(BF16)

    row = lax.broadcasted_iota(jnp.int32, (SGU_CHUNK, SGU_CHUNK), 0)
    col = lax.broadcasted_iota(jnp.int32, (SGU_CHUNK, SGU_CHUNK), 1)
    causal = row >= col
    wm = [jnp.where(causal, sw_ref[g], 0.0).astype(BF16) for g in range(SGU_GROUPS)]
    lane = lax.broadcasted_iota(jnp.int32, (SGU_CHUNK, LANES), 1)
    low_half = lane < SGU_GROUP_DIM

    n_chunks = tr // SGU_CHUNK
    for pr in range(SGU_GROUPS // 2):
        cols = slice(pr * LANES, (pr + 1) * LANES)
        blk = jnp.concatenate([vn[c * SGU_CHUNK:(c + 1) * SGU_CHUNK, cols] for c in range(n_chunks)],
                              axis=1)
        low = jnp.concatenate([low_half] * n_chunks, axis=1)
        mixed = (_dot(wm[2 * pr], jnp.where(low, blk, jnp.zeros_like(blk)))
                 + _dot(wm[2 * pr + 1], jnp.where(low, jnp.zeros_like(blk), blk)))
        for c in range(n_chunks):
            rows = slice(c * SGU_CHUNK, (c + 1) * SGU_CHUNK)
            m_c = mixed[:, c * LANES:(c + 1) * LANES] + sb_ref[:, cols]
            ysgu_ref[rows, cols] = (u_ref[rows, cols].astype(F32) * m_c).astype(BF16)

    z = z_ref[...].astype(F32)
    zprev = jnp.where(first, 0.0, zprev_ref[...].astype(F32))
    pos = (i % (SEQ // tr)) * tr + lax.broadcasted_iota(jnp.int32, (tr, POOL_GROUP_DIM), 0)
    for gi, w in enumerate(POOL_WINDOWS):
        cols = slice(gi * POOL_GROUP_DIM, (gi + 1) * POOL_GROUP_DIM)
        zg = z[:, cols]
        cur = jnp.concatenate([zprev[:, cols], zg], axis=0)
        d = 1
        while d < w:
            cur = cur[d:] + cur[:-d]
            d *= 2
        win_sum = cur[cur.shape[0] - tr:]
        count = jnp.minimum(pos + 1, w).astype(F32)
        pooled = win_sum / count - zg
        y = _dot(pooled.astype(BF16), pw_ref[gi].astype(BF16)) * ps_ref[:, cols]
        ypool_ref[:, cols] = y.astype(BF16)


def _branches(uvg, qkvz, sgu_norm_g, sgu_w, sgu_bias_full, pool_w, pool_scale):
    tr = BRANCH_TR
    halo_blocks = tr // POOL_HALO
    return pl.pallas_call(
        _branch_kernel,
        grid=(M_ROWS // tr,),
        in_specs=[
            pl.BlockSpec((tr, BRANCH_WIDTH), lambda i: (i, 0)),
            pl.BlockSpec((tr, BRANCH_WIDTH), lambda i: (i, 1)),
            pl.BlockSpec((tr, BRANCH_WIDTH), lambda i: (i, 3)),
            pl.BlockSpec((POOL_HALO, BRANCH_WIDTH),
                         lambda i: (jnp.maximum(i * halo_blocks - 1, 0), 3)),
            pl.BlockSpec((1, BRANCH_WIDTH), lambda i: (0, 0)),
            pl.BlockSpec((SGU_GROUPS, SGU_CHUNK, SGU_CHUNK), lambda i: (0, 0, 0)),
            pl.BlockSpec((SGU_CHUNK, BRANCH_WIDTH), lambda i: (0, 0)),
            pl.BlockSpec((POOL_GROUPS, POOL_GROUP_DIM, POOL_GROUP_DIM), lambda i: (0, 0, 0)),
            pl.BlockSpec((1, BRANCH_WIDTH), lambda i: (0, 0)),
        ],
        out_specs=[
            pl.BlockSpec((tr, BRANCH_WIDTH), lambda i: (i, 0)),
            pl.BlockSpec((tr, BRANCH_WIDTH), lambda i: (i, 0)),
        ],
        out_shape=[
            jax.ShapeDtypeStruct((M_ROWS, BRANCH_WIDTH), BF16),
            jax.ShapeDtypeStruct((M_ROWS, BRANCH_WIDTH), BF16),
        ],
        compiler_params=_params("arbitrary"),
        name="sgu_pool",
    )(uvg, uvg, qkvz, qkvz, sgu_norm_g, sgu_w, sgu_bias_full, pool_w, pool_scale)


def _attn_kernel(q_ref, k_ref, vt_ref, km_ref, o_ref,
                 qs_ref, pick_ref, m_ref, acc_ref, s_ref, smax_ref):
    own = pl.program_id(1)
    tq, blk, hd = ATTN_TQ, MOBA_BLOCK, MOBA_HEAD_DIM
    own_f = own.astype(F32)

    key_row = lax.broadcasted_iota(jnp.int32, (blk, tq), 0)
    causal = jnp.where(key_row > lax.broadcasted_iota(jnp.int32, (blk, tq), 1), NEG, 0.0)
    blk_id = lax.broadcasted_iota(jnp.int32, (N_KV_BLOCKS, tq), 0)
    blk_id_f = blk_id.astype(F32)
    past = blk_id < own
    ones_rows = jnp.ones((ONES_ROWS, blk), BF16)
    part_row = lax.broadcasted_iota(jnp.int32, (LANES - hd, tq), 0)
    q_alibi = jnp.zeros((LANES - hd, tq), F32)
    for t, part in enumerate(LOG2E_PARTS):
        q_alibi = jnp.where(part_row == t, part, q_alibi)

    q_t = jnp.transpose(q_ref[...].astype(F32))

    def raw_scores(j, h):
        start = pl.multiple_of(j * blk, blk)
        return _dot(k_ref[pl.ds(start, blk), h * LANES:(h + 1) * LANES], qs_ref[h])

    def values(j, h):
        return jnp.concatenate([vt_ref[j, h * hd:(h + 1) * hd, :], ones_rows], axis=0)

    gate_scores = []
    for h in range(MOBA_HEADS):
        qh = q_t[h * hd:(h + 1) * hd]
        qs_ref[h] = jnp.concatenate([qh * (hd ** -0.5 * LOG2E), q_alibi], axis=0).astype(BF16)
        km = km_ref[0, :, h * hd:(h + 1) * hd]
        km_hi = km.astype(BF16)
        km_lo = (km - km_hi.astype(F32)).astype(BF16)
        qg = qh.astype(BF16)
        sc = _dot(km_hi, qg) + _dot(km_lo, qg)
        gate_scores.append(jnp.where(past, sc, -jnp.inf))

    for h in range(MOBA_HEADS):
        s = raw_scores(own, h) + causal
        s_ref[1, h] = s
        m_ref[h] = jnp.max(s, axis=0, keepdims=True)

    for r in range(MOBA_TOPK):
        for h in range(MOBA_HEADS):
            sc = gate_scores[h]
            best = jnp.max(sc, axis=0, keepdims=True)
            idx = jnp.min(jnp.where(sc == best, blk_id_f, 1e9), axis=0, keepdims=True)
            pick_ref[h, r] = jnp.where(best > -jnp.inf, idx, -1.0)
            gate_scores[h] = jnp.where(blk_id_f == idx, -jnp.inf, sc)

    for h in range(MOBA_HEADS):
        acc_ref[h] = _dot(values(own, h), jnp.exp2(s_ref[1, h] - m_ref[h]).astype(BF16))

    def scores(j, h, slot):
        s = raw_scores(j, h)
        s_ref[slot, h] = s
        smax_ref[slot, h] = jnp.max(s, axis=0, keepdims=True)

    def absorb(j, h, slot):
        jf = j.astype(F32)
        slope = 2.0 ** -(h + 1)
        chosen = (pick_ref[h, 0] == jf) | (pick_ref[h, 1] == jf) | (pick_ref[h, 2] == jf)
        shift = jnp.where(chosen, (-slope * blk * LOG2E) * (own_f - jf), NEG)
        m = m_ref[h]
        m_new = jnp.maximum(m, smax_ref[slot, h] + shift)
        p = jnp.exp2(s_ref[slot, h] - (m_new - shift))
        m_ref[h] = m_new
        acc_ref[h] = jnp.exp2(m - m_new) * acc_ref[h] + _dot(values(j, h), p.astype(BF16))

    for h in range(MOBA_HEADS):
        scores(0, h, 0)

    def body(i, carry):
        a = 2 * i
        b = jnp.minimum(a + 1, own)
        c = jnp.minimum(a + 2, own)
        for h in range(MOBA_HEADS):
            scores(b, h, 1)
            absorb(a, h, 0)
        for h in range(MOBA_HEADS):
            scores(c, h, 0)
            absorb(b, h, 1)
        return carry

    lax.fori_loop(0, (own + 1) // 2, body, 0)

    o_t = jnp.concatenate([acc_ref[h, :hd] / acc_ref[h, hd:hd + 1] for h in range(MOBA_HEADS)],
                          axis=0)
    o_ref[...] = jnp.transpose(o_t).astype(BF16)


def _attention(qkvz, k_aug, kmean, v_t):
    tq = ATTN_TQ
    q_tiles = SEQ // tq
    return pl.pallas_call(
        _attn_kernel,
        grid=(BATCH, q_tiles),
        in_specs=[
            pl.BlockSpec((tq, BRANCH_WIDTH), lambda b, i: (b * q_tiles + i, 0)),
            pl.BlockSpec((SEQ, MOBA_HEADS * LANES), lambda b, i: (b, 0),
                         pipeline_mode=pl.Buffered(1)),
            pl.BlockSpec((N_KV_BLOCKS, BRANCH_WIDTH, MOBA_BLOCK), lambda b, i: (b, 0, 0)),
            pl.BlockSpec((1, N_KV_BLOCKS, BRANCH_WIDTH), lambda b, i: (b, 0, 0)),
        ],
        out_specs=pl.BlockSpec((tq, BRANCH_WIDTH), lambda b, i: (b * q_tiles + i, 0)),
        out_shape=jax.ShapeDtypeStruct((M_ROWS, BRANCH_WIDTH), BF16),
        scratch_shapes=[
            pltpu.VMEM((MOBA_HEADS, LANES, tq), BF16),
            pltpu.VMEM((MOBA_HEADS, MOBA_TOPK, 1, tq), F32),
            pltpu.VMEM((MOBA_HEADS, 1, tq), F32),
            pltpu.VMEM((MOBA_HEADS, MOBA_HEAD_DIM + ONES_ROWS, tq), F32),
            pltpu.VMEM((2, MOBA_HEADS, MOBA_BLOCK, tq), F32),
            pltpu.VMEM((2, MOBA_HEADS, 1, tq), F32),
        ],
        compiler_params=_params("arbitrary", "arbitrary"),
        name="moba_attention",
    )(qkvz, k_aug, v_t, kmean)


def _merge_kernel(ys_ref, ym_ref, yp_ref, gs0, gs1, gm0, gm1, gp0, gp1,
                  ws_ref, wm_ref, wp_ref, wo_ref, x_ref, g_ref, gt_ref, g2_ref, sc2_ref, sh2_ref,
                  o_ref, h_ref):
    half_w = D_MODEL // 2
    for r0 in range(0, MERGE_TM, MERGE_ROWS):
        rows = slice(r0, r0 + MERGE_ROWS)
        ys, ym, yp = ys_ref[rows, :], ym_ref[rows, :], yp_ref[rows, :]
        halves = []
        for hf, (gs, gm, gp) in enumerate(((gs0, gm0, gp0), (gs1, gm1, gp1))):
            cols = slice(hf * half_w, (hf + 1) * half_w)
            merged = (gs[rows, :].astype(F32) * _dot(ys, ws_ref[:, cols])
                      + gm[rows, :].astype(F32) * _dot(ym, wm_ref[:, cols])
                      + gp[rows, :].astype(F32) * _dot(yp, wp_ref[:, cols]))
            halves.append(merged.astype(BF16))
        y = _dot(jnp.concatenate(halves, axis=1), wo_ref[...])
        x_new = x_ref[rows, :] + _rms_norm(y, g_ref[...] * gt_ref[0])
        o_ref[rows, :] = x_new
        h_ref[rows, :] = _modulated_norm(x_new, g2_ref[...], sc2_ref[0], sh2_ref[0])


def _merge(ysgu, ymoba, ypool, uvg, w_sgu_out, w_moba_out, w_pool_out, w_out, x, g, gate,
           g_next, scale_next, shift_next):
    tm = MERGE_TM
    tiles_per_seq = SEQ // tm
    half_w = D_MODEL // 2
    gate_col0 = 2 * BRANCH_WIDTH // half_w
    resident = functools.partial(pl.BlockSpec, pipeline_mode=pl.Buffered(1))
    branch = pl.BlockSpec((tm, BRANCH_WIDTH), lambda i: (i, 0))
    gate_specs = [pl.BlockSpec((tm, half_w), functools.partial(lambda i, c: (i, c), c=gate_col0 + c))
                  for c in range(6)]
    w_branch = resident((BRANCH_WIDTH, D_MODEL), lambda i: (0, 0))
    rows = pl.BlockSpec((tm, D_MODEL), lambda i: (i, 0))
    vec = pl.BlockSpec((1, D_MODEL), lambda i: (0, 0))
    per_batch = pl.BlockSpec((1, 1, D_MODEL), lambda i: (i // tiles_per_seq, 0, 0))
    return pl.pallas_call(
        _merge_kernel,
        grid=(M_ROWS // tm,),
        in_specs=[branch, branch, branch, *gate_specs, w_branch, w_branch, w_branch,
                  resident((D_MODEL, D_MODEL), lambda i: (0, 0)),
                  rows, vec, per_batch, vec, per_batch, per_batch],
        out_specs=[rows, rows],
        out_shape=[jax.ShapeDtypeStruct((M_ROWS, D_MODEL), F32),
                   jax.ShapeDtypeStruct((M_ROWS, D_MODEL), BF16)],
        compiler_params=_params("arbitrary"),
        name="merge_outproj",
    )(ysgu, ymoba, ypool, uvg, uvg, uvg, uvg, uvg, uvg, w_sgu_out, w_moba_out, w_pool_out, w_out,
      x, g, gate, g_next, scale_next, shift_next)


def _up_kernel(h_ref, wg_ref, wv_ref, cg_ref, cv_ref, bg_ref, bv_ref, a_ref,
               zg_ref, zv_ref, tail_g_ref, tail_v_ref):
    tm, sub = UP_TM, SUBLANES
    j = pl.program_id(1)

    @pl.when((pl.program_id(0) % (SEQ // tm)) == 0)
    def _():
        tail_g_ref[j] = jnp.zeros(tail_g_ref.shape[1:], F32)
        tail_v_ref[j] = jnp.zeros(tail_v_ref.shape[1:], F32)

    h = h_ref[...]

    def conv(w_ref, z_ref, tail_ref, cw_ref, cb_ref):
        z = _dot(h, w_ref[...])
        taps = []
        for s in range(UP_TN // LANES):
            cols = slice(s * LANES, (s + 1) * LANES)
            z_ref[s, :sub, :] = tail_ref[j, s]
            z_ref[s, sub:, :] = z[:, cols]
            tail_ref[j, s] = z[tm - sub:, cols]
            taps.append(cw_ref[0:1, cols] * z_ref[s, sub - 2:sub - 2 + tm, :]
                        + cw_ref[1:2, cols] * z_ref[s, sub - 1:sub - 1 + tm, :]
                        + cw_ref[2:3, cols] * z[:, cols] + cb_ref[:, cols])
        return jnp.concatenate(taps, axis=1)

    gate = conv(wg_ref, zg_ref, tail_g_ref, cg_ref, bg_ref)
    val = conv(wv_ref, zv_ref, tail_v_ref, cv_ref, bv_ref)
    a_ref[...] = (_gelu_tanh(gate) * val).astype(BF16)


def _ffn_up(h, w_up, w_conv, b_conv):
    tm, tn = UP_TM, UP_TN
    nj = D_FF // tn
    slabs = tn // LANES
    return pl.pallas_call(
        _up_kernel,
        grid=(M_ROWS // tm, nj),
        in_specs=[
            pl.BlockSpec((tm, D_MODEL), lambda i, j: (i, 0)),
            pl.BlockSpec((D_MODEL, tn), lambda i, j: (0, j)),
            pl.BlockSpec((D_MODEL, tn), lambda i, j: (0, nj + j)),
            pl.BlockSpec((3, tn), lambda i, j: (0, j)),
            pl.BlockSpec((3, tn), lambda i, j: (0, nj + j)),
            pl.BlockSpec((1, tn), lambda i, j: (0, j)),
            pl.BlockSpec((1, tn), lambda i, j: (0, nj + j)),
        ],
        out_specs=pl.BlockSpec((tm, tn), lambda i, j: (i, j)),
        out_shape=jax.ShapeDtypeStruct((M_ROWS, D_FF), BF16),
        scratch_shapes=([pltpu.VMEM((slabs, tm + SUBLANES, LANES), F32) for _ in range(2)]
                        + [pltpu.VMEM((nj, slabs, SUBLANES, LANES), F32) for _ in range(2)]),
        compiler_params=_params("arbitrary", "arbitrary"),
        name="ffn_up",
    )(h, w_up, w_up, w_conv, w_conv, b_conv, b_conv)


def _down_kernel(emit_h, a_ref, w_ref, x_ref, g_ref, gt_ref, *rest):
    if emit_h:
        g2_ref, sc2_ref, sh2_ref, o_ref, h_ref = rest
    else:
        (o_ref,) = rest
    for r0 in range(0, DOWN_TM, DOWN_ROWS):
        rows = slice(r0, r0 + DOWN_ROWS)
        y = _dot(a_ref[rows, :], w_ref[...])
        x_new = x_ref[rows, :] + _rms_norm(y, g_ref[...] * gt_ref[0])
        o_ref[rows, :] = x_new
        if emit_h:
            h_ref[rows, :] = _modulated_norm(x_new, g2_ref[...], sc2_ref[0], sh2_ref[0])


def _ffn_down(a, w_down, x, g, gate, next_norm=None):
    tm = DOWN_TM
    tiles_per_seq = SEQ // tm
    rows = pl.BlockSpec((tm, D_MODEL), lambda i: (i, 0))
    vec = pl.BlockSpec((1, D_MODEL), lambda i: (0, 0))
    per_batch = pl.BlockSpec((1, 1, D_MODEL), lambda i: (i // tiles_per_seq, 0, 0))
    emit_h = next_norm is not None
    in_specs = [pl.BlockSpec((tm, D_FF), lambda i: (i, 0)),
                pl.BlockSpec((D_FF, D_MODEL), lambda i: (0, 0), pipeline_mode=pl.Buffered(1)),
                rows, vec, per_batch]
    out_specs = [rows]
    out_shape = [jax.ShapeDtypeStruct((M_ROWS, D_MODEL), F32)]
    args = [a, w_down, x, g, gate]
    if emit_h:
        in_specs += [vec, per_batch, per_batch]
        out_specs.append(rows)
        out_shape.append(jax.ShapeDtypeStruct((M_ROWS, D_MODEL), BF16))
        args += list(next_norm)
    return pl.pallas_call(
        functools.partial(_down_kernel, emit_h),
        grid=(M_ROWS // tm,),
        in_specs=in_specs,
        out_specs=out_specs,
        out_shape=out_shape,
        compiler_params=_params("arbitrary"),
        name="ffn_down",
    )(*args)


def kernel(x, c, g_pre_mix, g_post_mix, g_pre_ffn, g_post_ffn, w_ada, b_ada, w_in, b_in,
           sgu_norm_g, sgu_w, sgu_b, pool_w, pool_scale, w_sgu_out, w_moba_out, w_pool_out,
           w_out, w_up, w_conv, b_conv, w_down):
    c_pad = jnp.pad(c, ((0, 8 - BATCH), (0, 0)))
    mod = _modulation(c_pad, w_ada, b_ada)[:, :BATCH]
    mod = mod.reshape(DEPTH, BATCH, N_MOD, 1, D_MODEL)
    shift1, scale1, gate1, shift2, scale2, gate2 = [mod[:, :, n] for n in range(N_MOD)]
    row = lambda v, l: v[l].reshape(1, -1)
    qkv_cols = slice(2 * BRANCH_WIDTH, 6 * BRANCH_WIDTH)

    xf = x.reshape(M_ROWS, D_MODEL)
    h = _prenorm(xf, row(g_pre_mix, 0), scale1[0], shift1[0])
    for l in range(DEPTH):
        w_in_l = w_in[l].astype(BF16)
        uvg = _uvg_proj(h, w_in_l, row(b_in, l))
        qkvz, kmean, v_t, k_aug = _qkv_proj(h, w_in_l[:, qkv_cols],
                                            b_in[l, qkv_cols].reshape(1, -1))
        kmean = kmean.reshape(BATCH, N_KV_BLOCKS, BRANCH_WIDTH)

        sgu_bias_full = jnp.repeat(sgu_b[l].T, SGU_GROUP_DIM, axis=1)
        ysgu, ypool = _branches(uvg, qkvz, row(sgu_norm_g, l), sgu_w[l], sgu_bias_full, pool_w[l],
                                row(pool_scale, l))
        ymoba = _attention(qkvz, k_aug, kmean, v_t)
        xf, h = _merge(ysgu, ymoba, ypool, uvg, w_sgu_out[l].astype(BF16),
                       w_moba_out[l].astype(BF16), w_pool_out[l].astype(BF16),
                       w_out[l].astype(BF16), xf, row(g_post_mix, l), gate1[l],
                       row(g_pre_ffn, l), scale2[l], shift2[l])

        a = _ffn_up(h, w_up[l].astype(BF16), w_conv[l], row(b_conv, l))
        if l + 1 < DEPTH:
            xf, h = _ffn_down(a, w_down[l].astype(BF16), xf, row(g_post_ffn, l), gate2[l],
                              (row(g_pre_mix, l + 1), scale1[l + 1], shift1[l + 1]))
        else:
            (xf,) = _ffn_down(a, w_down[l].astype(BF16), xf, row(g_post_ffn, l), gate2[l])
    return xf.reshape(BATCH, SEQ, D_MODEL)
```

```python
import functools

import jax
import jax.numpy as jnp
import numpy as np
from jax import lax
from jax.experimental import pallas as pl
from jax.experimental.pallas import tpu as pltpu

F32 = jnp.float32
BF16 = jnp.bfloat16

D_MODEL = 2048
BATCH = 4
SEQ = 8192
DEPTH = 2
EPS = 1e-6
BRANCH_WIDTH = 512
SGU_GROUPS = 8
SGU_GROUP_DIM = 64
SGU_CHUNK = 128
MOBA_HEADS = 8
MOBA_HEAD_DIM = 64
MOBA_BLOCK = 256
MOBA_TOPK = 3
POOL_WINDOWS = (2, 4, 8, 16)
POOL_GROUPS = 4
POOL_GROUP_DIM = 128
IN_WIDTH = 6 * BRANCH_WIDTH + 3 * D_MODEL
D_FF = 5632
N_MOD = 6
M_ROWS = BATCH * SEQ
N_KV_BLOCKS = SEQ // MOBA_BLOCK

VMEM_LIMIT_BYTES = 56 * 1024 * 1024
LANES = 128
SUBLANES = 8
NEG = -1e30
LOG2E = 1.4426950408889634
ALIBI_PARTS = 3


def _bf16_parts(value, n):
    parts = []
    for _ in range(n):
        parts.append(float(np.asarray(value - sum(parts), dtype=jnp.bfloat16)))
    return tuple(parts)


LOG2E_PARTS = _bf16_parts(LOG2E, ALIBI_PARTS)
ONES_ROWS = 16

MOD_TN = 1024
PRENORM_TM = 512
QKV_TM = 1024
UVG_TM, UVG_TN = 256, 1024
UVG_WIDTH = 2 * BRANCH_WIDTH + 3 * D_MODEL
BRANCH_TR = 512
ATTN_TQ = MOBA_BLOCK
MERGE_TM = 512
MERGE_ROWS = 256
UP_TM, UP_TN = 1024, 512
DOWN_TM = 256
DOWN_ROWS = 128
POOL_HALO = 16


GELU_C = 0.7978845608028654
GELU_A = 0.044715


def _sigmoid(x):
    return 1.0 / (1.0 + jnp.exp(-x))


def _sigmoid2(x):
    return 1.0 / (1.0 + jnp.exp2(-LOG2E * x))


def _gelu_tanh(x):
    k1 = -2.0 * GELU_C * LOG2E
    return x / (1.0 + jnp.exp2(x * (k1 + (k1 * GELU_A) * (x * x))))


def _rms_norm(x, g):
    ms = jnp.mean(x * x, axis=-1, keepdims=True)
    return (x * lax.rsqrt(ms + EPS)) * g


def _modulated_norm(x, g, scale, shift):
    return (_rms_norm(x, g * (1.0 + scale)) + shift).astype(BF16)


def _dot(a, b):
    return jnp.dot(a, b, preferred_element_type=F32)


def _params(*sem):
    return pltpu.CompilerParams(dimension_semantics=sem, vmem_limit_bytes=VMEM_LIMIT_BYTES)


def _mod_kernel(c_ref, w_ref, b_ref, o_ref):
    c = c_ref[...]
    cond = (c * _sigmoid(c)).astype(BF16)
    o_ref[0] = _dot(cond, w_ref[0].astype(BF16)) + b_ref[0]


def _modulation(c_pad, w_ada, b_ada):
    rows = c_pad.shape[0]
    n = N_MOD * D_MODEL
    return pl.pallas_call(
        _mod_kernel,
        grid=(DEPTH, n // MOD_TN),
        in_specs=[
            pl.BlockSpec((rows, D_MODEL), lambda l, j: (0, 0)),
            pl.BlockSpec((1, D_MODEL, MOD_TN), lambda l, j: (l, 0, j)),
            pl.BlockSpec((1, 1, MOD_TN), lambda l, j: (l, 0, j)),
        ],
        out_specs=pl.BlockSpec((1, rows, MOD_TN), lambda l, j: (l, 0, j)),
        out_shape=jax.ShapeDtypeStruct((DEPTH, rows, n), F32),
        compiler_params=_params("arbitrary", "arbitrary"),
        name="adaln_mod",
    )(c_pad, w_ada, b_ada.reshape(DEPTH, 1, n))


def _prenorm_kernel(x_ref, g_ref, sc_ref, sh_ref, h_ref):
    h_ref[...] = _modulated_norm(x_ref[...], g_ref[...], sc_ref[0], sh_ref[0])


def _prenorm(x, g, scale, shift):
    tm = PRENORM_TM
    tiles_per_seq = SEQ // tm
    return pl.pallas_call(
        _prenorm_kernel,
        grid=(M_ROWS // tm,),
        in_specs=[
            pl.BlockSpec((tm, D_MODEL), lambda i: (i, 0)),
            pl.BlockSpec((1, D_MODEL), lambda i: (0, 0)),
            pl.BlockSpec((1, 1, D_MODEL), lambda i: (i // tiles_per_seq, 0, 0)),
            pl.BlockSpec((1, 1, D_MODEL), lambda i: (i // tiles_per_seq, 0, 0)),
        ],
        out_specs=pl.BlockSpec((tm, D_MODEL), lambda i: (i, 0)),
        out_shape=jax.ShapeDtypeStruct((M_ROWS, D_MODEL), BF16),
        compiler_params=_params("arbitrary"),
        name="prenorm",
    )(x, g, scale, shift)


def _qkv_kernel(h_ref, w_ref, b_ref, qz_ref, kmean_ref, vt_ref, kaug_ref):
    acc = _dot(h_ref[...], w_ref[...]) + b_ref[...]
    qz_ref[:, :BRANCH_WIDTH] = acc[:, :BRANCH_WIDTH].astype(BF16)
    qz_ref[:, BRANCH_WIDTH:] = acc[:, 3 * BRANCH_WIDTH:].astype(BF16)
    hd = MOBA_HEAD_DIM
    lane = lax.broadcasted_iota(jnp.int32, (QKV_TM, LANES - hd), 1)
    row_in_block = (lax.broadcasted_iota(jnp.int32, (QKV_TM, LANES - hd), 0)
                    % MOBA_BLOCK).astype(F32)
    groups = []
    for h in range(MOBA_HEADS):
        k_h = acc[:, BRANCH_WIDTH + h * hd:BRANCH_WIDTH + (h + 1) * hd]
        pos = jnp.where(lane < ALIBI_PARTS, (2.0 ** -(h + 1)) * row_in_block, 0.0)
        groups += [k_h, pos]
    kaug_ref[...] = jnp.concatenate(groups, axis=1).astype(BF16)
    for r in range(QKV_TM // MOBA_BLOCK):
        rows = slice(r * MOBA_BLOCK, (r + 1) * MOBA_BLOCK)
        kmean_ref[0, r:r + 1, :] = jnp.mean(acc[rows, BRANCH_WIDTH:2 * BRANCH_WIDTH],
                                            axis=0, keepdims=True)
        vt_ref[r] = jnp.transpose(acc[rows, 2 * BRANCH_WIDTH:3 * BRANCH_WIDTH]).astype(BF16)


def _qkv_proj(h, w, b):
    tm = QKV_TM
    n = 4 * BRANCH_WIDTH
    blocks_per_tile = tm // MOBA_BLOCK
    return pl.pallas_call(
        _qkv_kernel,
        grid=(M_ROWS // tm,),
        in_specs=[
            pl.BlockSpec((tm, D_MODEL), lambda i: (i, 0)),
            pl.BlockSpec((D_MODEL, n), lambda i: (0, 0), pipeline_mode=pl.Buffered(1)),
            pl.BlockSpec((1, n), lambda i: (0, 0)),
        ],
        out_specs=[
            pl.BlockSpec((tm, 2 * BRANCH_WIDTH), lambda i: (i, 0)),
            pl.BlockSpec((1, blocks_per_tile, BRANCH_WIDTH), lambda i: (i, 0, 0)),
            pl.BlockSpec((blocks_per_tile, BRANCH_WIDTH, MOBA_BLOCK), lambda i: (i, 0, 0)),
            pl.BlockSpec((tm, MOBA_HEADS * LANES), lambda i: (i, 0)),
        ],
        out_shape=[
            jax.ShapeDtypeStruct((M_ROWS, 2 * BRANCH_WIDTH), BF16),
            jax.ShapeDtypeStruct((M_ROWS // tm, blocks_per_tile, BRANCH_WIDTH), F32),
            jax.ShapeDtypeStruct((M_ROWS // MOBA_BLOCK, BRANCH_WIDTH, MOBA_BLOCK), BF16),
            jax.ShapeDtypeStruct((M_ROWS, MOBA_HEADS * LANES), BF16),
        ],
        compiler_params=_params("arbitrary"),
        name="qkv_proj",
    )(h, w, b)


def _uvg_kernel(h_ref, w_ref, b_ref, o_ref):
    h = h_ref[...]
    skipped = 4 * BRANCH_WIDTH
    for t in range(UVG_WIDTH // UVG_TN):
        src = t * UVG_TN if t == 0 else t * UVG_TN + skipped
        act = _gelu_tanh if t == 0 else _sigmoid2
        y = _dot(h, w_ref[:, src:src + UVG_TN]) + b_ref[:, src:src + UVG_TN]
        o_ref[:, t * UVG_TN:(t + 1) * UVG_TN] = act(y).astype(BF16)


def _uvg_proj(h, w_in, b_in):
    tm = UVG_TM
    return pl.pallas_call(
        _uvg_kernel,
        grid=(M_ROWS // tm,),
        in_specs=[
            pl.BlockSpec((tm, D_MODEL), lambda i: (i, 0)),
            pl.BlockSpec((D_MODEL, IN_WIDTH), lambda i: (0, 0), pipeline_mode=pl.Buffered(1)),
            pl.BlockSpec((1, IN_WIDTH), lambda i: (0, 0)),
        ],
        out_specs=pl.BlockSpec((tm, UVG_WIDTH), lambda i: (i, 0)),
        out_shape=jax.ShapeDtypeStruct((M_ROWS, UVG_WIDTH), BF16),
        compiler_params=_params("arbitrary"),
        name="uvg_proj",
    )(h, w_in, b_in)


def _branch_kernel(u_ref, v_ref, z_ref, zprev_ref, ng_ref, sw_ref, sb_ref, pw_ref, ps_ref,
                   ysgu_ref, ypool_ref):
    i = pl.program_id(0)
    tr = BRANCH_TR
    first = (i % (SEQ // tr)) == 0

    v = v_ref[...].astype(F32)
    mu = jnp.mean(v, axis=-1, keepdims=True)
    vc = v - mu
    var = jnp.mean(vc * vc, axis=-1, keepdims=True)
    vn = (vc * lax.rsqrt(var + EPS) * ng_ref[...]).astype(BF16)

    row = lax.broadcasted_iota(jnp.int32, (SGU_CHUNK, SGU_CHUNK), 0)
    col = lax.broadcasted_iota(jnp.int32, (SGU_CHUNK, SGU_CHUNK), 1)
    causal = row >= col
    wm = [jnp.where(causal, sw_ref[g], 0.0).astype(BF16) for g in range(SGU_GROUPS)]
    lane = lax.broadcasted_iota(jnp.int32, (SGU_CHUNK, LANES), 1)
    low_half = lane < SGU_GROUP_DIM

    n_chunks = tr // SGU_CHUNK
    for pr in range(SGU_GROUPS // 2):
        cols = slice(pr * LANES, (pr + 1) * LANES)
        blk = jnp.concatenate([vn[c * SGU_CHUNK:(c + 1) * SGU_CHUNK, cols] for c in range(n_chunks)],
                              axis=1)
        low = jnp.concatenate([low_half] * n_chunks, axis=1)
        mixed = (_dot(wm[2 * pr], jnp.where(low, blk, jnp.zeros_like(blk)))
                 + _dot(wm[2 * pr + 1], jnp.where(low, jnp.zeros_like(blk), blk)))
        for c in range(n_chunks):
            rows = slice(c * SGU_CHUNK, (c + 1) * SGU_CHUNK)
            m_c = mixed[:, c * LANES:(c + 1) * LANES] + sb_ref[:, cols]
            ysgu_ref[rows, cols] = (u_ref[rows, cols].astype(F32) * m_c).astype(BF16)

    z = z_ref[...].astype(F32)
    zprev = jnp.where(first, 0.0, zprev_ref[...].astype(F32))
    pos = (i % (SEQ // tr)) * tr + lax.broadcasted_iota(jnp.int32, (tr, POOL_GROUP_DIM), 0)
    for gi, w in enumerate(POOL_WINDOWS):
        cols = slice(gi * POOL_GROUP_DIM, (gi + 1) * POOL_GROUP_DIM)
        zg = z[:, cols]
        cur = jnp.concatenate([zprev[:, cols], zg], axis=0)
        d = 1
        while d < w:
            cur = cur[d:] + cur[:-d]
            d *= 2
        win_sum = cur[cur.shape[0] - tr:]
        count = jnp.minimum(pos + 1, w).astype(F32)
        pooled = win_sum / count - zg
        y = _dot(pooled.astype(BF16), pw_ref[gi].astype(BF16)) * ps_ref[:, cols]
        ypool_ref[:, cols] = y.astype(BF16)


def _branches(uvg, qkvz, sgu_norm_g, sgu_w, sgu_bias_full, pool_w, pool_scale):
    tr = BRANCH_TR
    halo_blocks = tr // POOL_HALO
    return pl.pallas_call(
        _branch_kernel,
        grid=(M_ROWS // tr,),
        in_specs=[
            pl.BlockSpec((tr, BRANCH_WIDTH), lambda i: (i, 0)),
            pl.BlockSpec((tr, BRANCH_WIDTH), lambda i: (i, 1)),
            pl.BlockSpec((tr, BRANCH_WIDTH), lambda i: (i, 1)),
            pl.BlockSpec((POOL_HALO, BRANCH_WIDTH),
                         lambda i: (jnp.maximum(i * halo_blocks - 1, 0), 1)),
            pl.BlockSpec((1, BRANCH_WIDTH), lambda i: (0, 0)),
            pl.BlockSpec((SGU_GROUPS, SGU_CHUNK, SGU_CHUNK), lambda i: (0, 0, 0)),
            pl.BlockSpec((SGU_CHUNK, BRANCH_WIDTH), lambda i: (0, 0)),
            pl.BlockSpec((POOL_GROUPS, POOL_GROUP_DIM, POOL_GROUP_DIM), lambda i: (0, 0, 0)),
            pl.BlockSpec((1, BRANCH_WIDTH), lambda i: (0, 0)),
        ],
        out_specs=[
            pl.BlockSpec((tr, BRANCH_WIDTH), lambda i: (i, 0)),
            pl.BlockSpec((tr, BRANCH_WIDTH), lambda i: (i, 0)),
        ],
        out_shape=[
            jax.ShapeDtypeStruct((M_ROWS, BRANCH_WIDTH), BF16),
            jax.ShapeDtypeStruct((M_ROWS, BRANCH_WIDTH), BF16),
        ],
        compiler_params=_params("arbitrary"),
        name="sgu_pool",
    )(uvg, uvg, qkvz, qkvz, sgu_norm_g, sgu_w, sgu_bias_full, pool_w, pool_scale)


def _attn_kernel(q_ref, k_ref, vt_ref, km_ref, o_ref,
                 qs_ref, pick_ref, m_ref, acc_ref, s_ref, smax_ref):
    own = pl.program_id(1)
    tq, blk, hd = ATTN_TQ, MOBA_BLOCK, MOBA_HEAD_DIM
    own_f = own.astype(F32)

    key_row = lax.broadcasted_iota(jnp.int32, (blk, tq), 0)
    causal = jnp.where(key_row > lax.broadcasted_iota(jnp.int32, (blk, tq), 1), NEG, 0.0)
    blk_id = lax.broadcasted_iota(jnp.int32, (N_KV_BLOCKS, tq), 0)
    blk_id_f = blk_id.astype(F32)
    past = blk_id < own
    ones_rows = jnp.ones((ONES_ROWS, blk), BF16)
    part_row = lax.broadcasted_iota(jnp.int32, (LANES - hd, tq), 0)
    q_alibi = jnp.zeros((LANES - hd, tq), F32)
    for t, part in enumerate(LOG2E_PARTS):
        q_alibi = jnp.where(part_row == t, part, q_alibi)

    q_t = jnp.transpose(q_ref[...].astype(F32))

    def raw_scores(j, h):
        start = pl.multiple_of(j * blk, blk)
        return _dot(k_ref[pl.ds(start, blk), h * LANES:(h + 1) * LANES], qs_ref[h])

    def values(j, h):
        return jnp.concatenate([vt_ref[j, h * hd:(h + 1) * hd, :], ones_rows], axis=0)

    gate_scores = []
    for h in range(MOBA_HEADS):
        qh = q_t[h * hd:(h + 1) * hd]
        qs_ref[h] = jnp.concatenate([qh * (hd ** -0.5 * LOG2E), q_alibi], axis=0).astype(BF16)
        km = km_ref[0, :, h * hd:(h + 1) * hd]
        km_hi = km.astype(BF16)
        km_lo = (km - km_hi.astype(F32)).astype(BF16)
        qg = qh.astype(BF16)
        sc = _dot(km_hi, qg) + _dot(km_lo, qg)
        gate_scores.append(jnp.where(past, sc, -jnp.inf))

    for h in range(MOBA_HEADS):
        s = raw_scores(own, h) + causal
        s_ref[1, h] = s
        m_ref[h] = jnp.max(s, axis=0, keepdims=True)

    for r in range(MOBA_TOPK):
        for h in range(MOBA_HEADS):
            sc = gate_scores[h]
            best = jnp.max(sc, axis=0, keepdims=True)
            idx = jnp.min(jnp.where(sc == best, blk_id_f, 1e9), axis=0, keepdims=True)
            pick_ref[h, r] = jnp.where(best > -jnp.inf, idx, -1.0)
            gate_scores[h] = jnp.where(blk_id_f == idx, -jnp.inf, sc)

    for h in range(MOBA_HEADS):
        acc_ref[h] = _dot(values(own, h), jnp.exp2(s_ref[1, h] - m_ref[h]).astype(BF16))

    def scores(j, h, slot):
        s = raw_scores(j, h)
        s_ref[slot, h] = s
        smax_ref[slot, h] = jnp.max(s, axis=0, keepdims=True)

    def absorb(j, h, slot):
        jf = j.astype(F32)
        slope = 2.0 ** -(h + 1)
        chosen = (pick_ref[h, 0] == jf) | (pick_ref[h, 1] == jf) | (pick_ref[h, 2] == jf)
        shift = jnp.where(chosen, (-slope * blk * LOG2E) * (own_f - jf), NEG)
        m = m_ref[h]
        m_new = jnp.maximum(m, smax_ref[slot, h] + shift)
        p = jnp.exp2(s_ref[slot, h] - (m_new - shift))
        m_ref[h] = m_new
        acc_ref[h] = jnp.exp2(m - m_new) * acc_ref[h] + _dot(values(j, h), p.astype(BF16))

    for h in range(MOBA_HEADS):
        scores(0, h, 0)

    def body(i, carry):
        a = 2 * i
        b = jnp.minimum(a + 1, own)
        c = jnp.minimum(a + 2, own)
        for h in range(MOBA_HEADS):
            scores(b, h, 1)
            absorb(a, h, 0)
        for h in range(MOBA_HEADS):
            scores(c, h, 0)
            absorb(b, h, 1)
        return carry

    lax.fori_loop(0, (own + 1) // 2, body, 0)

    o_t = jnp.concatenate([acc_ref[h, :hd] / acc_ref[h, hd:hd + 1] for h in range(MOBA_HEADS)],
                          axis=0)
    o_ref[...] = jnp.transpose(o_t).astype(BF16)


def _attention(qkvz, k_aug, kmean, v_t):
    tq = ATTN_TQ
    q_tiles = SEQ // tq
    return pl.pallas_call(
        _attn_kernel,
        grid=(BATCH, q_tiles),
        in_specs=[
            pl.BlockSpec((tq, BRANCH_WIDTH), lambda b, i: (b * q_tiles + i, 0)),
            pl.BlockSpec((SEQ, MOBA_HEADS * LANES), lambda b, i: (b, 0),
                         pipeline_mode=pl.Buffered(1)),
            pl.BlockSpec((N_KV_BLOCKS, BRANCH_WIDTH, MOBA_BLOCK), lambda b, i: (b, 0, 0)),
            pl.BlockSpec((1, N_KV_BLOCKS, BRANCH_WIDTH), lambda b, i: (b, 0, 0)),
        ],
        out_specs=pl.BlockSpec((tq, BRANCH_WIDTH), lambda b, i: (b * q_tiles + i, 0)),
        out_shape=jax.ShapeDtypeStruct((M_ROWS, BRANCH_WIDTH), BF16),
        scratch_shapes=[
            pltpu.VMEM((MOBA_HEADS, LANES, tq), BF16),
            pltpu.VMEM((MOBA_HEADS, MOBA_TOPK, 1, tq), F32),
            pltpu.VMEM((MOBA_HEADS, 1, tq), F32),
            pltpu.VMEM((MOBA_HEADS, MOBA_HEAD_DIM + ONES_ROWS, tq), F32),
            pltpu.VMEM((2, MOBA_HEADS, MOBA_BLOCK, tq), F32),
            pltpu.VMEM((2, MOBA_HEADS, 1, tq), F32),
        ],
        compiler_params=_params("arbitrary", "arbitrary"),
        name="moba_attention",
    )(qkvz, k_aug, v_t, kmean)


def _merge_kernel(ys_ref, ym_ref, yp_ref, gs0, gs1, gm0, gm1, gp0, gp1,
                  ws_ref, wm_ref, wp_ref, wo_ref, x_ref, g_ref, gt_ref, g2_ref, sc2_ref, sh2_ref,
                  o_ref, h_ref):
    half_w = D_MODEL // 2
    for r0 in range(0, MERGE_TM, MERGE_ROWS):
        rows = slice(r0, r0 + MERGE_ROWS)
        ys, ym, yp = ys_ref[rows, :], ym_ref[rows, :], yp_ref[rows, :]
        halves = []
        for hf, (gs, gm, gp) in enumerate(((gs0, gm0, gp0), (gs1, gm1, gp1))):
            cols = slice(hf * half_w, (hf + 1) * half_w)
            merged = (gs[rows, :].astype(F32) * _dot(ys, ws_ref[:, cols])
                      + gm[rows, :].astype(F32) * _dot(ym, wm_ref[:, cols])
                      + gp[rows, :].astype(F32) * _dot(yp, wp_ref[:, cols]))
            halves.append(merged.astype(BF16))
        y = _dot(jnp.concatenate(halves, axis=1), wo_ref[...])
        x_new = x_ref[rows, :] + _rms_norm(y, g_ref[...] * gt_ref[0])
        o_ref[rows, :] = x_new
        h_ref[rows, :] = _modulated_norm(x_new, g2_ref[...], sc2_ref[0], sh2_ref[0])


def _merge(ysgu, ymoba, ypool, uvg, w_sgu_out, w_moba_out, w_pool_out, w_out, x, g, gate,
           g_next, scale_next, shift_next):
    tm = MERGE_TM
    tiles_per_seq = SEQ // tm
    half_w = D_MODEL // 2
    gate_col0 = 2 * BRANCH_WIDTH // half_w
    resident = functools.partial(pl.BlockSpec, pipeline_mode=pl.Buffered(1))
    branch = pl.BlockSpec((tm, BRANCH_WIDTH), lambda i: (i, 0))
    gate_specs = [pl.BlockSpec((tm, half_w), functools.partial(lambda i, c: (i, c), c=gate_col0 + c))
                  for c in range(6)]
    w_branch = resident((BRANCH_WIDTH, D_MODEL), lambda i: (0, 0))
    rows = pl.BlockSpec((tm, D_MODEL), lambda i: (i, 0))
    vec = pl.BlockSpec((1, D_MODEL), lambda i: (0, 0))
    per_batch = pl.BlockSpec((1, 1, D_MODEL), lambda i: (i // tiles_per_seq, 0, 0))
    return pl.pallas_call(
        _merge_kernel,
        grid=(M_ROWS // tm,),
        in_specs=[branch, branch, branch, *gate_specs, w_branch, w_branch, w_branch,
                  resident((D_MODEL, D_MODEL), lambda i: (0, 0)),
                  rows, vec, per_batch, vec, per_batch, per_batch],
        out_specs=[rows, rows],
        out_shape=[jax.ShapeDtypeStruct((M_ROWS, D_MODEL), F32),
                   jax.ShapeDtypeStruct((M_ROWS, D_MODEL), BF16)],
        compiler_params=_params("arbitrary"),
        name="merge_outproj",
    )(ysgu, ymoba, ypool, uvg, uvg, uvg, uvg, uvg, uvg, w_sgu_out, w_moba_out, w_pool_out, w_out,
      x, g, gate, g_next, scale_next, shift_next)


def _up_kernel(h_ref, wg_ref, wv_ref, cg_ref, cv_ref, bg_ref, bv_ref, a_ref,
               zg_ref, zv_ref, tail_g_ref, tail_v_ref):
    tm, sub = UP_TM, SUBLANES
    j = pl.program_id(1)

    @pl.when((pl.program_id(0) % (SEQ // tm)) == 0)
    def _():
        tail_g_ref[j] = jnp.zeros(tail_g_ref.shape[1:], F32)
        tail_v_ref[j] = jnp.zeros(tail_v_ref.shape[1:], F32)

    h = h_ref[...]

    def conv(w_ref, z_ref, tail_ref, cw_ref, cb_ref):
        z = _dot(h, w_ref[...])
        taps = []
        for s in range(UP_TN // LANES):
            cols = slice(s * LANES, (s + 1) * LANES)
            z_ref[s, :sub, :] = tail_ref[j, s]
            z_ref[s, sub:, :] = z[:, cols]
            tail_ref[j, s] = z[tm - sub:, cols]
            taps.append(cw_ref[0:1, cols] * z_ref[s, sub - 2:sub - 2 + tm, :]
                        + cw_ref[1:2, cols] * z_ref[s, sub - 1:sub - 1 + tm, :]
                        + cw_ref[2:3, cols] * z[:, cols] + cb_ref[:, cols])
        return jnp.concatenate(taps, axis=1)

    gate = conv(wg_ref, zg_ref, tail_g_ref, cg_ref, bg_ref)
    val = conv(wv_ref, zv_ref, tail_v_ref, cv_ref, bv_ref)
    a_ref[...] = (_gelu_tanh(gate) * val).astype(BF16)


def _ffn_up(h, w_up, w_conv, b_conv):
    tm, tn = UP_TM, UP_TN
    nj = D_FF // tn
    slabs = tn // LANES
    return pl.pallas_call(
        _up_kernel,
        grid=(M_ROWS // tm, nj),
        in_specs=[
            pl.BlockSpec((tm, D_MODEL), lambda i, j: (i, 0)),
            pl.BlockSpec((D_MODEL, tn), lambda i, j: (0, j)),
            pl.BlockSpec((D_MODEL, tn), lambda i, j: (0, nj + j)),
            pl.BlockSpec((3, tn), lambda i, j: (0, j)),
            pl.BlockSpec((3, tn), lambda i, j: (0, nj + j)),
            pl.BlockSpec((1, tn), lambda i, j: (0, j)),
            pl.BlockSpec((1, tn), lambda i, j: (0, nj + j)),
        ],
        out_specs=pl.BlockSpec((tm, tn), lambda i, j: (i, j)),
        out_shape=jax.ShapeDtypeStruct((M_ROWS, D_FF), BF16),
        scratch_shapes=([pltpu.VMEM((slabs, tm + SUBLANES, LANES), F32) for _ in range(2)]
                        + [pltpu.VMEM((nj, slabs, SUBLANES, LANES), F32) for _ in range(2)]),
        compiler_params=_params("arbitrary", "arbitrary"),
        name="ffn_up",
    )(h, w_up, w_up, w_conv, w_conv, b_conv, b_conv)


def _down_kernel(emit_h, a_ref, w_ref, x_ref, g_ref, gt_ref, *rest):
    if emit_h:
        g2_ref, sc2_ref, sh2_ref, o_ref, h_ref = rest
    else:
        (o_ref,) = rest
    for r0 in range(0, DOWN_TM, DOWN_ROWS):
        rows = slice(r0, r0 + DOWN_ROWS)
        y = _dot(a_ref[rows, :], w_ref[...])
        x_new = x_ref[rows, :] + _rms_norm(y, g_ref[...] * gt_ref[0])
        o_ref[rows, :] = x_new
        if emit_h:
            h_ref[rows, :] = _modulated_norm(x_new, g2_ref[...], sc2_ref[0], sh2_ref[0])


def _ffn_down(a, w_down, x, g, gate, next_norm=None):
    tm = DOWN_TM
    tiles_per_seq = SEQ // tm
    rows = pl.BlockSpec((tm, D_MODEL), lambda i: (i, 0))
    vec = pl.BlockSpec((1, D_MODEL), lambda i: (0, 0))
    per_batch = pl.BlockSpec((1, 1, D_MODEL), lambda i: (i // tiles_per_seq, 0, 0))
    emit_h = next_norm is not None
    in_specs = [pl.BlockSpec((tm, D_FF), lambda i: (i, 0)),
                pl.BlockSpec((D_FF, D_MODEL), lambda i: (0, 0), pipeline_mode=pl.Buffered(1)),
                rows, vec, per_batch]
    out_specs = [rows]
    out_shape = [jax.ShapeDtypeStruct((M_ROWS, D_MODEL), F32)]
    args = [a, w_down, x, g, gate]
    if emit_h:
        in_specs += [vec, per_batch, per_batch]
        out_specs.append(rows)
        out_shape.append(jax.ShapeDtypeStruct((M_ROWS, D_MODEL), BF16))
        args += list(next_norm)
    return pl.pallas_call(
        functools.partial(_down_kernel, emit_h),
        grid=(M_ROWS // tm,),
        in_specs=in_specs,
        out_specs=out_specs,
        out_shape=out_shape,
        compiler_params=_params("arbitrary"),
        name="ffn_down",
    )(*args)


def kernel(x, c, g_pre_mix, g_post_mix, g_pre_ffn, g_post_ffn, w_ada, b_ada, w_in, b_in,
           sgu_norm_g, sgu_w, sgu_b, pool_w, pool_scale, w_sgu_out, w_moba_out, w_pool_out,
           w_out, w_up, w_conv, b_conv, w_down):
    c_pad = jnp.pad(c, ((0, SUBLANES - BATCH), (0, 0)))
    mod = _modulation(c_pad, w_ada, b_ada)[:, :BATCH]
    mod = mod.reshape(DEPTH, BATCH, N_MOD, 1, D_MODEL)
    shift1, scale1, gate1, shift2, scale2, gate2 = [mod[:, :, n] for n in range(N_MOD)]
    row = lambda v, l: v[l].reshape(1, -1)
    qkv_cols = slice(2 * BRANCH_WIDTH, 6 * BRANCH_WIDTH)

    xf = x.reshape(M_ROWS, D_MODEL)
    h = _prenorm(xf, row(g_pre_mix, 0), scale1[0], shift1[0])
    for l in range(DEPTH):
        w_in_l = w_in[l].astype(BF16)
        uvg = _uvg_proj(h, w_in_l, row(b_in, l))
        qkvz, kmean, v_t, k_aug = _qkv_proj(h, w_in_l[:, qkv_cols],
                                            b_in[l, qkv_cols].reshape(1, -1))
        kmean = kmean.reshape(BATCH, N_KV_BLOCKS, BRANCH_WIDTH)

        sgu_bias_full = jnp.repeat(sgu_b[l].T, SGU_GROUP_DIM, axis=1)
        ysgu, ypool = _branches(uvg, qkvz, row(sgu_norm_g, l), sgu_w[l], sgu_bias_full, pool_w[l],
                                row(pool_scale, l))
        ymoba = _attention(qkvz, k_aug, kmean, v_t)
        xf, h = _merge(ysgu, ymoba, ypool, uvg, w_sgu_out[l].astype(BF16),
                       w_moba_out[l].astype(BF16), w_pool_out[l].astype(BF16),
                       w_out[l].astype(BF16), xf, row(g_post_mix, l), gate1[l],
                       row(g_pre_ffn, l), scale2[l], shift2[l])

        a = _ffn_up(h, w_up[l].astype(BF16), w_conv[l], row(b_conv, l))
        if l + 1 < DEPTH:
            xf, h = _ffn_down(a, w_down[l].astype(BF16), xf, row(g_post_ffn, l), gate2[l],
                              (row(g_pre_mix, l + 1), scale1[l + 1], shift1[l + 1]))
        else:
            (xf,) = _ffn_down(a, w_down[l].astype(BF16), xf, row(g_post_ffn, l), gate2[l])
    return xf.reshape(BATCH, SEQ, D_MODEL)
```

```python
import functools

import jax
import jax.numpy as jnp
import numpy as np
from jax import lax
from jax.experimental import pallas as pl
from jax.experimental.pallas import tpu as pltpu

F32 = jnp.float32
BF16 = jnp.bfloat16

D_MODEL = 2048
BATCH = 4
SEQ = 8192
DEPTH = 2
EPS = 1e-6
BRANCH_WIDTH = 512
SGU_GROUPS = 8
SGU_GROUP_DIM = 64
SGU_CHUNK = 128
MOBA_HEADS = 8
MOBA_HEAD_DIM = 64
MOBA_BLOCK = 256
MOBA_TOPK = 3
POOL_WINDOWS = (2, 4, 8, 16)
POOL_GROUPS = 4
POOL_GROUP_DIM = 128
IN_WIDTH = 6 * BRANCH_WIDTH + 3 * D_MODEL
D_FF = 5632
N_MOD = 6
M_ROWS = BATCH * SEQ
N_KV_BLOCKS = SEQ // MOBA_BLOCK

VMEM_LIMIT_BYTES = 56 * 1024 * 1024
LANES = 128
SUBLANES = 8
NEG = -1e30
LOG2E = 1.4426950408889634
ALIBI_PARTS = 3


def _bf16_parts(value, n):
    parts = []
    for _ in range(n):
        parts.append(float(np.asarray(value - sum(parts), dtype=jnp.bfloat16)))
    return tuple(parts)


LOG2E_PARTS = _bf16_parts(LOG2E, ALIBI_PARTS)
ONES_ROWS = 16

MOD_TN = 1024
PRENORM_TM = 512
QKV_TM = 1024
UVG_TM, UVG_TN = 256, 1024
UVG_WIDTH = 2 * BRANCH_WIDTH + 3 * D_MODEL
BRANCH_TR = 512
ATTN_TQ = MOBA_BLOCK
MERGE_TM = 512
MERGE_ROWS = 256
UP_TM, UP_TN = 1024, 512
DOWN_TM = 256
DOWN_ROWS = 128
POOL_HALO = 16


GELU_C = 0.7978845608028654
GELU_A = 0.044715


def _sigmoid(x):
    return 1.0 / (1.0 + jnp.exp(-x))


def _sigmoid2(x):
    return 1.0 / (1.0 + jnp.exp2(-LOG2E * x))


def _gelu_tanh(x):
    k1 = -2.0 * GELU_C * LOG2E
    return x / (1.0 + jnp.exp2(x * (k1 + (k1 * GELU_A) * (x * x))))


def _rms_norm(x, g):
    ms = jnp.mean(x * x, axis=-1, keepdims=True)
    return (x * lax.rsqrt(ms + EPS)) * g


def _modulated_norm(x, g, scale, shift):
    return (_rms_norm(x, g * (1.0 + scale)) + shift).astype(BF16)


def _dot(a, b):
    return jnp.dot(a, b, preferred_element_type=F32)


def _params(*sem):
    return pltpu.CompilerParams(dimension_semantics=sem, vmem_limit_bytes=VMEM_LIMIT_BYTES)


def _mod_kernel(c_ref, w_ref, b_ref, o_ref):
    c = c_ref[...]
    cond = (c * _sigmoid(c)).astype(BF16)
    o_ref[0] = _dot(cond, w_ref[0].astype(BF16)) + b_ref[0]


def _modulation(c_pad, w_ada, b_ada):
    rows = c_pad.shape[0]
    n = N_MOD * D_MODEL
    return pl.pallas_call(
        _mod_kernel,
        grid=(DEPTH, n // MOD_TN),
        in_specs=[
            pl.BlockSpec((rows, D_MODEL), lambda l, j: (0, 0)),
            pl.BlockSpec((1, D_MODEL, MOD_TN), lambda l, j: (l, 0, j)),
            pl.BlockSpec((1, 1, MOD_TN), lambda l, j: (l, 0, j)),
        ],
        out_specs=pl.BlockSpec((1, rows, MOD_TN), lambda l, j: (l, 0, j)),
        out_shape=jax.ShapeDtypeStruct((DEPTH, rows, n), F32),
        compiler_params=_params("arbitrary", "arbitrary"),
        name="adaln_mod",
    )(c_pad, w_ada, b_ada.reshape(DEPTH, 1, n))


def _prenorm_kernel(x_ref, g_ref, sc_ref, sh_ref, h_ref):
    h_ref[...] = _modulated_norm(x_ref[...], g_ref[...], sc_ref[0], sh_ref[0])


def _prenorm(x, g, scale, shift):
    tm = PRENORM_TM
    tiles_per_seq = SEQ // tm
    return pl.pallas_call(
        _prenorm_kernel,
        grid=(M_ROWS // tm,),
        in_specs=[
            pl.BlockSpec((tm, D_MODEL), lambda i: (i, 0)),
            pl.BlockSpec((1, D_MODEL), lambda i: (0, 0)),
            pl.BlockSpec((1, 1, D_MODEL), lambda i: (i // tiles_per_seq, 0, 0)),
            pl.BlockSpec((1, 1, D_MODEL), lambda i: (i // tiles_per_seq, 0, 0)),
        ],
        out_specs=pl.BlockSpec((tm, D_MODEL), lambda i: (i, 0)),
        out_shape=jax.ShapeDtypeStruct((M_ROWS, D_MODEL), BF16),
        compiler_params=_params("arbitrary"),
        name="prenorm",
    )(x, g, scale, shift)


def _qkv_kernel(h_ref, w_ref, b_ref, qz_ref, kmean_ref, vt_ref, kaug_ref):
    acc = _dot(h_ref[...], w_ref[...]) + b_ref[...]
    qz_ref[:, :BRANCH_WIDTH] = acc[:, :BRANCH_WIDTH].astype(BF16)
    qz_ref[:, BRANCH_WIDTH:] = acc[:, 3 * BRANCH_WIDTH:].astype(BF16)
    hd = MOBA_HEAD_DIM
    lane = lax.broadcasted_iota(jnp.int32, (QKV_TM, LANES - hd), 1)
    row_in_block = (lax.broadcasted_iota(jnp.int32, (QKV_TM, LANES - hd), 0)
                    % MOBA_BLOCK).astype(F32)
    groups = []
    for h in range(MOBA_HEADS):
        k_h = acc[:, BRANCH_WIDTH + h * hd:BRANCH_WIDTH + (h + 1) * hd]
        pos = jnp.where(lane < ALIBI_PARTS, (2.0 ** -(h + 1)) * row_in_block, 0.0)
        groups += [k_h, pos]
    kaug_ref[...] = jnp.concatenate(groups, axis=1).astype(BF16)
    for r in range(QKV_TM // MOBA_BLOCK):
        rows = slice(r * MOBA_BLOCK, (r + 1) * MOBA_BLOCK)
        kmean_ref[0, r:r + 1, :] = jnp.mean(acc[rows, BRANCH_WIDTH:2 * BRANCH_WIDTH],
                                            axis=0, keepdims=True)
        vt_ref[r] = jnp.transpose(acc[rows, 2 * BRANCH_WIDTH:3 * BRANCH_WIDTH]).astype(BF16)


def _qkv_proj(h, w, b):
    tm = QKV_TM
    n = 4 * BRANCH_WIDTH
    blocks_per_tile = tm // MOBA_BLOCK
    return pl.pallas_call(
        _qkv_kernel,
        grid=(M_ROWS // tm,),
        in_specs=[
            pl.BlockSpec((tm, D_MODEL), lambda i: (i, 0)),
            pl.BlockSpec((D_MODEL, n), lambda i: (0, 0), pipeline_mode=pl.Buffered(1)),
            pl.BlockSpec((1, n), lambda i: (0, 0)),
        ],
        out_specs=[
            pl.BlockSpec((tm, 2 * BRANCH_WIDTH), lambda i: (i, 0)),
            pl.BlockSpec((1, blocks_per_tile, BRANCH_WIDTH), lambda i: (i, 0, 0)),
            pl.BlockSpec((blocks_per_tile, BRANCH_WIDTH, MOBA_BLOCK), lambda i: (i, 0, 0)),
            pl.BlockSpec((tm, MOBA_HEADS * LANES), lambda i: (i, 0)),
        ],
        out_shape=[
            jax.ShapeDtypeStruct((M_ROWS, 2 * BRANCH_WIDTH), BF16),
            jax.ShapeDtypeStruct((M_ROWS // tm, blocks_per_tile, BRANCH_WIDTH), F32),
            jax.ShapeDtypeStruct((M_ROWS // MOBA_BLOCK, BRANCH_WIDTH, MOBA_BLOCK), BF16),
            jax.ShapeDtypeStruct((M_ROWS, MOBA_HEADS * LANES), BF16),
        ],
        compiler_params=_params("arbitrary"),
        name="qkv_proj",
    )(h, w, b)


def _uvg_kernel(h_ref, w_ref, b_ref, o_ref):
    h = h_ref[...]
    skipped = 4 * BRANCH_WIDTH
    for t in range(UVG_WIDTH // UVG_TN):
        src = t * UVG_TN if t == 0 else t * UVG_TN + skipped
        act = _gelu_tanh if t == 0 else _sigmoid2
        y = _dot(h, w_ref[:, src:src + UVG_TN]) + b_ref[:, src:src + UVG_TN]
        o_ref[:, t * UVG_TN:(t + 1) * UVG_TN] = act(y).astype(BF16)


def _uvg_proj(h, w_in, b_in):
    tm = UVG_TM
    return pl.pallas_call(
        _uvg_kernel,
        grid=(M_ROWS // tm,),
        in_specs=[
            pl.BlockSpec((tm, D_MODEL), lambda i: (i, 0)),
            pl.BlockSpec((D_MODEL, IN_WIDTH), lambda i: (0, 0), pipeline_mode=pl.Buffered(1)),
            pl.BlockSpec((1, IN_WIDTH), lambda i: (0, 0)),
        ],
        out_specs=pl.BlockSpec((tm, UVG_WIDTH), lambda i: (i, 0)),
        out_shape=jax.ShapeDtypeStruct((M_ROWS, UVG_WIDTH), BF16),
        compiler_params=_params("arbitrary"),
        name="uvg_proj",
    )(h, w_in, b_in)


def _branch_kernel(u_ref, v_ref, z_ref, zprev_ref, ng_ref, sw_ref, sb_ref, pw_ref, ps_ref,
                   ysgu_ref, ypool_ref):
    i = pl.program_id(0)
    tr = BRANCH_TR
    first = (i % (SEQ // tr)) == 0

    v = v_ref[...].astype(F32)
    mu = jnp.mean(v, axis=-1, keepdims=True)
    vc = v - mu
    var = jnp.mean(vc * vc, axis=-1, keepdims=True)
    vn = (vc * lax.rsqrt(var + EPS) * ng_ref[...]).astype(BF16)

    row = lax.broadcasted_iota(jnp.int32, (SGU_CHUNK, SGU_CHUNK), 0)
    col = lax.broadcasted_iota(jnp.int32, (SGU_CHUNK, SGU_CHUNK), 1)
    causal = row >= col
    wm = [jnp.where(causal, sw_ref[g], 0.0).astype(BF16) for g in range(SGU_GROUPS)]
    lane = lax.broadcasted_iota(jnp.int32, (SGU_CHUNK, LANES), 1)
    low_half = lane < SGU_GROUP_DIM

    n_chunks = tr // SGU_CHUNK
    for pr in range(SGU_GROUPS // 2):
        cols = slice(pr * LANES, (pr + 1) * LANES)
        blk = jnp.concatenate([vn[c * SGU_CHUNK:(c + 1) * SGU_CHUNK, cols] for c in range(n_chunks)],
                              axis=1)
        low = jnp.concatenate([low_half] * n_chunks, axis=1)
        mixed = (_dot(wm[2 * pr], jnp.where(low, blk, jnp.zeros_like(blk)))
                 + _dot(wm[2 * pr + 1], jnp.where(low, jnp.zeros_like(blk), blk)))
        for c in range(n_chunks):
            rows = slice(c * SGU_CHUNK, (c + 1) * SGU_CHUNK)
            m_c = mixed[:, c * LANES:(c + 1) * LANES] + sb_ref[:, cols]
            ysgu_ref[rows, cols] = (u_ref[rows, cols].astype(F32) * m_c).astype(BF16)

    z = z_ref[...].astype(F32)
    zprev = jnp.where(first, 0.0, zprev_ref[...].astype(F32))
    pos = (i % (SEQ // tr)) * tr + lax.broadcasted_iota(jnp.int32, (tr, POOL_GROUP_DIM), 0)
    for gi, w in enumerate(POOL_WINDOWS):
        cols = slice(gi * POOL_GROUP_DIM, (gi + 1) * POOL_GROUP_DIM)
        zg = z[:, cols]
        cur = jnp.concatenate([zprev[:, cols], zg], axis=0)
        d = 1
        while d < w:
            cur = cur[d:] + cur[:-d]
            d *= 2
        win_sum = cur[cur.shape[0] - tr:]
        count = jnp.minimum(pos + 1, w).astype(F32)
        pooled = win_sum / count - zg
        y = _dot(pooled.astype(BF16), pw_ref[gi].astype(BF16)) * ps_ref[:, cols]
        ypool_ref[:, cols] = y.astype(BF16)


def _branches(uvg, qkvz, sgu_norm_g, sgu_w, sgu_bias_full, pool_w, pool_scale):
    tr = BRANCH_TR
    halo_blocks = tr // POOL_HALO
    return pl.pallas_call(
        _branch_kernel,
        grid=(M_ROWS // tr,),
        in_specs=[
            pl.BlockSpec((tr, BRANCH_WIDTH), lambda i: (i, 0)),
            pl.BlockSpec((tr, BRANCH_WIDTH), lambda i: (i, 1)),
            pl.BlockSpec((tr, BRANCH_WIDTH), lambda i: (i, 1)),
            pl.BlockSpec((POOL_HALO, BRANCH_WIDTH),
                         lambda i: (jnp.maximum(i * halo_blocks - 1, 0), 1)),
            pl.BlockSpec((1, BRANCH_WIDTH), lambda i: (0, 0)),
            pl.BlockSpec((SGU_GROUPS, SGU_CHUNK, SGU_CHUNK), lambda i: (0, 0, 0)),
            pl.BlockSpec((SGU_CHUNK, BRANCH_WIDTH), lambda i: (0, 0)),
            pl.BlockSpec((POOL_GROUPS, POOL_GROUP_DIM, POOL_GROUP_DIM), lambda i: (0, 0, 0)),
            pl.BlockSpec((1, BRANCH_WIDTH), lambda i: (0, 0)),
        ],
        out_specs=[
            pl.BlockSpec((tr, BRANCH_WIDTH), lambda i: (i, 0)),
            pl.BlockSpec((tr, BRANCH_WIDTH), lambda i: (i, 0)),
        ],
        out_shape=[
            jax.ShapeDtypeStruct((M_ROWS, BRANCH_WIDTH), BF16),
            jax.ShapeDtypeStruct((M_ROWS, BRANCH_WIDTH), BF16),
        ],
        compiler_params=_params("arbitrary"),
        name="sgu_pool",
    )(uvg, uvg, qkvz, qkvz, sgu_norm_g, sgu_w, sgu_bias_full, pool_w, pool_scale)


def _attn_kernel(q_ref, k_ref, vt_ref, km_ref, o_ref,
                 qs_ref, pick_ref, m_ref, acc_ref, s_ref, smax_ref):
    pair = pl.program_id(1)
    tq, blk, hd = ATTN_TQ, MOBA_BLOCK, MOBA_HEAD_DIM
    own = [2 * pair, 2 * pair + 1]
    own_f = [o.astype(F32) for o in own]
    tiles, heads = range(2), range(MOBA_HEADS)

    key_row = lax.broadcasted_iota(jnp.int32, (blk, tq), 0)
    causal = jnp.where(key_row > lax.broadcasted_iota(jnp.int32, (blk, tq), 1), NEG, 0.0)
    blk_id = lax.broadcasted_iota(jnp.int32, (N_KV_BLOCKS, tq), 0)
    blk_id_f = blk_id.astype(F32)
    ones_rows = jnp.ones((ONES_ROWS, blk), BF16)
    part_row = lax.broadcasted_iota(jnp.int32, (LANES - hd, tq), 0)
    q_alibi = jnp.zeros((LANES - hd, tq), F32)
    for t, part in enumerate(LOG2E_PARTS):
        q_alibi = jnp.where(part_row == t, part, q_alibi)

    def raw_scores(j, x, h):
        start = pl.multiple_of(j * blk, blk)
        return _dot(k_ref[pl.ds(start, blk), h * LANES:(h + 1) * LANES], qs_ref[x, h])

    def values(j, h):
        return jnp.concatenate([vt_ref[j, h * hd:(h + 1) * hd, :], ones_rows], axis=0)

    def scores(j, x, h, slot):
        s = raw_scores(j, x, h)
        s_ref[slot, x, h] = s
        smax_ref[slot, x, h] = jnp.max(s, axis=0, keepdims=True)

    def absorb(j, x, h, slot):
        jf = j.astype(F32)
        slope = 2.0 ** -(h + 1)
        picks = pick_ref[x, h]
        chosen = (picks[0] == jf) | (picks[1] == jf) | (picks[2] == jf)
        shift = jnp.where(chosen, (-slope * blk * LOG2E) * (own_f[x] - jf), NEG)
        m = m_ref[x, h]
        m_new = jnp.maximum(m, smax_ref[slot, x, h] + shift)
        p = jnp.exp2(s_ref[slot, x, h] - (m_new - shift))
        m_ref[x, h] = m_new
        acc_ref[x, h] = (jnp.exp2(m - m_new) * acc_ref[x, h]
                         + _dot(values(j, h), p.astype(BF16)))

    gate_scores = {}
    for x in tiles:
        q_t = jnp.transpose(q_ref[x * tq:(x + 1) * tq, :].astype(F32))
        past = blk_id < own[x]
        for h in heads:
            qh = q_t[h * hd:(h + 1) * hd]
            qs_ref[x, h] = jnp.concatenate([qh * (hd ** -0.5 * LOG2E), q_alibi],
                                           axis=0).astype(BF16)
            km = km_ref[0, :, h * hd:(h + 1) * hd]
            km_hi = km.astype(BF16)
            km_lo = (km - km_hi.astype(F32)).astype(BF16)
            qg = qh.astype(BF16)
            sc = _dot(km_hi, qg) + _dot(km_lo, qg)
            gate_scores[x, h] = jnp.where(past, sc, -jnp.inf)

    for x in tiles:
        for h in heads:
            s = raw_scores(own[x], x, h) + causal
            s_ref[1, x, h] = s
            m_ref[x, h] = jnp.max(s, axis=0, keepdims=True)

    for r in range(MOBA_TOPK):
        for x in tiles:
            for h in heads:
                sc = gate_scores[x, h]
                best = jnp.max(sc, axis=0, keepdims=True)
                idx = jnp.min(jnp.where(sc == best, blk_id_f, 1e9), axis=0, keepdims=True)
                pick_ref[x, h, r] = jnp.where(best > -jnp.inf, idx, -1.0)
                gate_scores[x, h] = jnp.where(blk_id_f == idx, -jnp.inf, sc)

    for x in tiles:
        for h in heads:
            acc_ref[x, h] = _dot(values(own[x], h),
                                 jnp.exp2(s_ref[1, x, h] - m_ref[x, h]).astype(BF16))

    for h in heads:
        scores(own[0], 1, h, 0)
    for h in heads:
        absorb(own[0], 1, h, 0)

    for x in tiles:
        for h in heads:
            scores(0, x, h, 0)

    def body(t, carry):
        a = 2 * t
        for x in tiles:
            for h in heads:
                scores(a + 1, x, h, 1)
                absorb(a, x, h, 0)
        for x in tiles:
            for h in heads:
                scores(a + 2, x, h, 0)
                absorb(a + 1, x, h, 1)
        return carry

    lax.fori_loop(0, pair, body, 0)

    for x in tiles:
        o_t = jnp.concatenate([acc_ref[x, h, :hd] / acc_ref[x, h, hd:hd + 1] for h in heads],
                              axis=0)
        o_ref[x * tq:(x + 1) * tq, :] = jnp.transpose(o_t).astype(BF16)


def _attention(qkvz, k_aug, kmean, v_t):
    tq = 2 * ATTN_TQ
    q_tiles = SEQ // tq
    return pl.pallas_call(
        _attn_kernel,
        grid=(BATCH, q_tiles),
        in_specs=[
            pl.BlockSpec((tq, BRANCH_WIDTH), lambda b, i: (b * q_tiles + i, 0)),
            pl.BlockSpec((SEQ, MOBA_HEADS * LANES), lambda b, i: (b, 0),
                         pipeline_mode=pl.Buffered(1)),
            pl.BlockSpec((N_KV_BLOCKS, BRANCH_WIDTH, MOBA_BLOCK), lambda b, i: (b, 0, 0)),
            pl.BlockSpec((1, N_KV_BLOCKS, BRANCH_WIDTH), lambda b, i: (b, 0, 0)),
        ],
        out_specs=pl.BlockSpec((tq, BRANCH_WIDTH), lambda b, i: (b * q_tiles + i, 0)),
        out_shape=jax.ShapeDtypeStruct((M_ROWS, BRANCH_WIDTH), BF16),
        scratch_shapes=[
            pltpu.VMEM((2, MOBA_HEADS, LANES, ATTN_TQ), BF16),
            pltpu.VMEM((2, MOBA_HEADS, MOBA_TOPK, 1, ATTN_TQ), F32),
            pltpu.VMEM((2, MOBA_HEADS, 1, ATTN_TQ), F32),
            pltpu.VMEM((2, MOBA_HEADS, MOBA_HEAD_DIM + ONES_ROWS, ATTN_TQ), F32),
            pltpu.VMEM((2, 2, MOBA_HEADS, MOBA_BLOCK, ATTN_TQ), F32),
            pltpu.VMEM((2, 2, MOBA_HEADS, 1, ATTN_TQ), F32),
        ],
        compiler_params=_params("arbitrary", "arbitrary"),
        name="moba_attention",
    )(qkvz, k_aug, v_t, kmean)


def _merge_kernel(ys_ref, ym_ref, yp_ref, gs0, gs1, gm0, gm1, gp0, gp1,
                  ws_ref, wm_ref, wp_ref, wo_ref, x_ref, g_ref, gt_ref, g2_ref, sc2_ref, sh2_ref,
                  o_ref, h_ref):
    half_w = D_MODEL // 2
    for r0 in range(0, MERGE_TM, MERGE_ROWS):
        rows = slice(r0, r0 + MERGE_ROWS)
        ys, ym, yp = ys_ref[rows, :], ym_ref[rows, :], yp_ref[rows, :]
        halves = []
        for hf, (gs, gm, gp) in enumerate(((gs0, gm0, gp0), (gs1, gm1, gp1))):
            cols = slice(hf * half_w, (hf + 1) * half_w)
            merged = (gs[rows, :].astype(F32) * _dot(ys, ws_ref[:, cols])
                      + gm[rows, :].astype(F32) * _dot(ym, wm_ref[:, cols])
                      + gp[rows, :].astype(F32) * _dot(yp, wp_ref[:, cols]))
            halves.append(merged.astype(BF16))
        y = _dot(jnp.concatenate(halves, axis=1), wo_ref[...])
        x_new = x_ref[rows, :] + _rms_norm(y, g_ref[...] * gt_ref[0])
        o_ref[rows, :] = x_new
        h_ref[rows, :] = _modulated_norm(x_new, g2_ref[...], sc2_ref[0], sh2_ref[0])


def _merge(ysgu, ymoba, ypool, uvg, w_sgu_out, w_moba_out, w_pool_out, w_out, x, g, gate,
           g_next, scale_next, shift_next):
    tm = MERGE_TM
    tiles_per_seq = SEQ // tm
    half_w = D_MODEL // 2
    gate_col0 = 2 * BRANCH_WIDTH // half_w
    resident = functools.partial(pl.BlockSpec, pipeline_mode=pl.Buffered(1))
    branch = pl.BlockSpec((tm, BRANCH_WIDTH), lambda i: (i, 0))
    gate_specs = [pl.BlockSpec((tm, half_w), functools.partial(lambda i, c: (i, c), c=gate_col0 + c))
                  for c in range(6)]
    w_branch = resident((BRANCH_WIDTH, D_MODEL), lambda i: (0, 0))
    rows = pl.BlockSpec((tm, D_MODEL), lambda i: (i, 0))
    vec = pl.BlockSpec((1, D_MODEL), lambda i: (0, 0))
    per_batch = pl.BlockSpec((1, 1, D_MODEL), lambda i: (i // tiles_per_seq, 0, 0))
    return pl.pallas_call(
        _merge_kernel,
        grid=(M_ROWS // tm,),
        in_specs=[branch, branch, branch, *gate_specs, w_branch, w_branch, w_branch,
                  resident((D_MODEL, D_MODEL), lambda i: (0, 0)),
                  rows, vec, per_batch, vec, per_batch, per_batch],
        out_specs=[rows, rows],
        out_shape=[jax.ShapeDtypeStruct((M_ROWS, D_MODEL), F32),
                   jax.ShapeDtypeStruct((M_ROWS, D_MODEL), BF16)],
        compiler_params=_params("arbitrary"),
        name="merge_outproj",
    )(ysgu, ymoba, ypool, uvg, uvg, uvg, uvg, uvg, uvg, w_sgu_out, w_moba_out, w_pool_out, w_out,
      x, g, gate, g_next, scale_next, shift_next)


def _up_kernel(h_ref, wg_ref, wv_ref, cg_ref, cv_ref, bg_ref, bv_ref, a_ref,
               zg_ref, zv_ref, tail_g_ref, tail_v_ref):
    tm, sub = UP_TM, SUBLANES
    j = pl.program_id(1)

    @pl.when((pl.program_id(0) % (SEQ // tm)) == 0)
    def _():
        tail_g_ref[j] = jnp.zeros(tail_g_ref.shape[1:], F32)
        tail_v_ref[j] = jnp.zeros(tail_v_ref.shape[1:], F32)

    h = h_ref[...]

    def conv(w_ref, z_ref, tail_ref, cw_ref, cb_ref):
        z = _dot(h, w_ref[...])
        taps = []
        for s in range(UP_TN // LANES):
            cols = slice(s * LANES, (s + 1) * LANES)
            z_ref[s, :sub, :] = tail_ref[j, s]
            z_ref[s, sub:, :] = z[:, cols]
            tail_ref[j, s] = z[tm - sub:, cols]
            taps.append(cw_ref[0:1, cols] * z_ref[s, sub - 2:sub - 2 + tm, :]
                        + cw_ref[1:2, cols] * z_ref[s, sub - 1:sub - 1 + tm, :]
                        + cw_ref[2:3, cols] * z[:, cols] + cb_ref[:, cols])
        return jnp.concatenate(taps, axis=1)

    gate = conv(wg_ref, zg_ref, tail_g_ref, cg_ref, bg_ref)
    val = conv(wv_ref, zv_ref, tail_v_ref, cv_ref, bv_ref)
    a_ref[...] = (_gelu_tanh(gate) * val).astype(BF16)


def _ffn_up(h, w_up, w_conv, b_conv):
    tm, tn = UP_TM, UP_TN
    nj = D_FF // tn
    slabs = tn // LANES
    return pl.pallas_call(
        _up_kernel,
        grid=(M_ROWS // tm, nj),
        in_specs=[
            pl.BlockSpec((tm, D_MODEL), lambda i, j: (i, 0)),
            pl.BlockSpec((D_MODEL, tn), lambda i, j: (0, j)),
            pl.BlockSpec((D_MODEL, tn), lambda i, j: (0, nj + j)),
            pl.BlockSpec((3, tn), lambda i, j: (0, j)),
            pl.BlockSpec((3, tn), lambda i, j: (0, nj + j)),
            pl.BlockSpec((1, tn), lambda i, j: (0, j)),
            pl.BlockSpec((1, tn), lambda i, j: (0, nj + j)),
        ],
        out_specs=pl.BlockSpec((tm, tn), lambda i, j: (i, j)),
        out_shape=jax.ShapeDtypeStruct((M_ROWS, D_FF), BF16),
        scratch_shapes=([pltpu.VMEM((slabs, tm + SUBLANES, LANES), F32) for _ in range(2)]
                        + [pltpu.VMEM((nj, slabs, SUBLANES, LANES), F32) for _ in range(2)]),
        compiler_params=_params("arbitrary", "arbitrary"),
        name="ffn_up",
    )(h, w_up, w_up, w_conv, w_conv, b_conv, b_conv)


def _down_kernel(emit_h, a_ref, w_ref, x_ref, g_ref, gt_ref, *rest):
    if emit_h:
        g2_ref, sc2_ref, sh2_ref, o_ref, h_ref = rest
    else:
        (o_ref,) = rest
    for r0 in range(0, DOWN_TM, DOWN_ROWS):
        rows = slice(r0, r0 + DOWN_ROWS)
        y = _dot(a_ref[rows, :], w_ref[...])
        x_new = x_ref[rows, :] + _rms_norm(y, g_ref[...] * gt_ref[0])
        o_ref[rows, :] = x_new
        if emit_h:
            h_ref[rows, :] = _modulated_norm(x_new, g2_ref[...], sc2_ref[0], sh2_ref[0])


def _ffn_down(a, w_down, x, g, gate, next_norm=None):
    tm = DOWN_TM
    tiles_per_seq = SEQ // tm
    rows = pl.BlockSpec((tm, D_MODEL), lambda i: (i, 0))
    vec = pl.BlockSpec((1, D_MODEL), lambda i: (0, 0))
    per_batch = pl.BlockSpec((1, 1, D_MODEL), lambda i: (i // tiles_per_seq, 0, 0))
    emit_h = next_norm is not None
    in_specs = [pl.BlockSpec((tm, D_FF), lambda i: (i, 0)),
                pl.BlockSpec((D_FF, D_MODEL), lambda i: (0, 0), pipeline_mode=pl.Buffered(1)),
                rows, vec, per_batch]
    out_specs = [rows]
    out_shape = [jax.ShapeDtypeStruct((M_ROWS, D_MODEL), F32)]
    args = [a, w_down, x, g, gate]
    if emit_h:
        in_specs += [vec, per_batch, per_batch]
        out_specs.append(rows)
        out_shape.append(jax.ShapeDtypeStruct((M_ROWS, D_MODEL), BF16))
        args += list(next_norm)
    return pl.pallas_call(
        functools.partial(_down_kernel, emit_h),
        grid=(M_ROWS // tm,),
        in_specs=in_specs,
        out_specs=out_specs,
        out_shape=out_shape,
        compiler_params=_params("arbitrary"),
        name="ffn_down",
    )(*args)


def kernel(x, c, g_pre_mix, g_post_mix, g_pre_ffn, g_post_ffn, w_ada, b_ada, w_in, b_in,
           sgu_norm_g, sgu_w, sgu_b, pool_w, pool_scale, w_sgu_out, w_moba_out, w_pool_out,
           w_out, w_up, w_conv, b_conv, w_down):
    c_pad = jnp.pad(c, ((0, SUBLANES - BATCH), (0, 0)))
    mod = _modulation(c_pad, w_ada, b_ada)[:, :BATCH]
    mod = mod.reshape(DEPTH, BATCH, N_MOD, 1, D_MODEL)
    shift1, scale1, gate1, shift2, scale2, gate2 = [mod[:, :, n] for n in range(N_MOD)]
    row = lambda v, l: v[l].reshape(1, -1)
    qkv_cols = slice(2 * BRANCH_WIDTH, 6 * BRANCH_WIDTH)

    xf = x.reshape(M_ROWS, D_MODEL)
    h = _prenorm(xf, row(g_pre_mix, 0), scale1[0], shift1[0])
    for l in range(DEPTH):
        w_in_l = w_in[l].astype(BF16)
        uvg = _uvg_proj(h, w_in_l, row(b_in, l))
        qkvz, kmean, v_t, k_aug = _qkv_proj(h, w_in_l[:, qkv_cols],
                                            b_in[l, qkv_cols].reshape(1, -1))
        kmean = kmean.reshape(BATCH, N_KV_BLOCKS, BRANCH_WIDTH)

        sgu_bias_full = jnp.repeat(sgu_b[l].T, SGU_GROUP_DIM, axis=1)
        ysgu, ypool = _branches(uvg, qkvz, row(sgu_norm_g, l), sgu_w[l], sgu_bias_full, pool_w[l],
                                row(pool_scale, l))
        ymoba = _attention(qkvz, k_aug, kmean, v_t)
        xf, h = _merge(ysgu, ymoba, ypool, uvg, w_sgu_out[l].astype(BF16),
                       w_moba_out[l].astype(BF16), w_pool_out[l].astype(BF16),
                       w_out[l].astype(BF16), xf, row(g_post_mix, l), gate1[l],
                       row(g_pre_ffn, l), scale2[l], shift2[l])

        a = _ffn_up(h, w_up[l].astype(BF16), w_conv[l], row(b_conv, l))
        if l + 1 < DEPTH:
            xf, h = _ffn_down(a, w_down[l].astype(BF16), xf, row(g_post_ffn, l), gate2[l],
                              (row(g_pre_mix, l + 1), scale1[l + 1], shift1[l + 1]))
        else:
            (xf,) = _ffn_down(a, w_down[l].astype(BF16), xf, row(g_post_ffn, l), gate2[l])
    return xf.reshape(BATCH, SEQ, D_MODEL)
```

```python
import functools

import jax
import jax.numpy as jnp
import numpy as np
from jax import lax
from jax.experimental import pallas as pl
from jax.experimental.pallas import tpu as pltpu

F32 = jnp.float32
BF16 = jnp.bfloat16

D_MODEL = 2048
BATCH = 4
SEQ = 8192
DEPTH = 2
EPS = 1e-6
BRANCH_WIDTH = 512
SGU_GROUPS = 8
SGU_GROUP_DIM = 64
SGU_CHUNK = 128
MOBA_HEADS = 8
MOBA_HEAD_DIM = 64
MOBA_BLOCK = 256
MOBA_TOPK = 3
POOL_WINDOWS = (2, 4, 8, 16)
POOL_GROUPS = 4
POOL_GROUP_DIM = 128
IN_WIDTH = 6 * BRANCH_WIDTH + 3 * D_MODEL
D_FF = 5632
N_MOD = 6
M_ROWS = BATCH * SEQ
N_KV_BLOCKS = SEQ // MOBA_BLOCK

VMEM_LIMIT_BYTES = 56 * 1024 * 1024
LANES = 128
SUBLANES = 8
NEG = -1e30
LOG2E = 1.4426950408889634
ALIBI_PARTS = 3


def _bf16_parts(value, n):
    parts = []
    for _ in range(n):
        parts.append(float(np.asarray(value - sum(parts), dtype=jnp.bfloat16)))
    return tuple(parts)


LOG2E_PARTS = _bf16_parts(LOG2E, ALIBI_PARTS)
ONES_ROWS = 16

MOD_TN = 1024
PRENORM_TM = 512
QKV_TM = 1024
UVG_TM, UVG_TN = 256, 1024
UVG_WIDTH = 2 * BRANCH_WIDTH + 3 * D_MODEL
BRANCH_TR = 512
ATTN_TQ = MOBA_BLOCK
MERGE_TM = 512
MERGE_ROWS = 256
UP_TM, UP_TN = 1024, 512
DOWN_TM = 512
DOWN_ROWS = 128
POOL_HALO = 16


GELU_C = 0.7978845608028654
GELU_A = 0.044715


def _sigmoid(x):
    return 1.0 / (1.0 + jnp.exp(-x))


def _sigmoid2(x):
    return 1.0 / (1.0 + jnp.exp2(-LOG2E * x))


def _gelu_tanh(x):
    k1 = -2.0 * GELU_C * LOG2E
    return x / (1.0 + jnp.exp2(x * (k1 + (k1 * GELU_A) * (x * x))))


def _rms_norm(x, g):
    ms = jnp.mean(x * x, axis=-1, keepdims=True)
    return (x * lax.rsqrt(ms + EPS)) * g


def _modulated_norm(x, g, scale, shift):
    return (_rms_norm(x, g * (1.0 + scale)) + shift).astype(BF16)


def _dot(a, b):
    return jnp.dot(a, b, preferred_element_type=F32)


def _params(*sem):
    return pltpu.CompilerParams(dimension_semantics=sem, vmem_limit_bytes=VMEM_LIMIT_BYTES)


def _mod_kernel(c_ref, w_ref, b_ref, o_ref):
    c = c_ref[...]
    cond = (c * _sigmoid(c)).astype(BF16)
    o_ref[0] = _dot(cond, w_ref[0].astype(BF16)) + b_ref[0]


def _modulation(c_pad, w_ada, b_ada):
    rows = c_pad.shape[0]
    n = N_MOD * D_MODEL
    return pl.pallas_call(
        _mod_kernel,
        grid=(DEPTH, n // MOD_TN),
        in_specs=[
            pl.BlockSpec((rows, D_MODEL), lambda l, j: (0, 0)),
            pl.BlockSpec((1, D_MODEL, MOD_TN), lambda l, j: (l, 0, j)),
            pl.BlockSpec((1, 1, MOD_TN), lambda l, j: (l, 0, j)),
        ],
        out_specs=pl.BlockSpec((1, rows, MOD_TN), lambda l, j: (l, 0, j)),
        out_shape=jax.ShapeDtypeStruct((DEPTH, rows, n), F32),
        compiler_params=_params("arbitrary", "arbitrary"),
        name="adaln_mod",
    )(c_pad, w_ada, b_ada.reshape(DEPTH, 1, n))


def _prenorm_kernel(x_ref, g_ref, sc_ref, sh_ref, h_ref):
    h_ref[...] = _modulated_norm(x_ref[...], g_ref[...], sc_ref[0], sh_ref[0])


def _prenorm(x, g, scale, shift):
    tm = PRENORM_TM
    tiles_per_seq = SEQ // tm
    return pl.pallas_call(
        _prenorm_kernel,
        grid=(M_ROWS // tm,),
        in_specs=[
            pl.BlockSpec((tm, D_MODEL), lambda i: (i, 0)),
            pl.BlockSpec((1, D_MODEL), lambda i: (0, 0)),
            pl.BlockSpec((1, 1, D_MODEL), lambda i: (i // tiles_per_seq, 0, 0)),
            pl.BlockSpec((1, 1, D_MODEL), lambda i: (i // tiles_per_seq, 0, 0)),
        ],
        out_specs=pl.BlockSpec((tm, D_MODEL), lambda i: (i, 0)),
        out_shape=jax.ShapeDtypeStruct((M_ROWS, D_MODEL), BF16),
        compiler_params=_params("arbitrary"),
        name="prenorm",
    )(x, g, scale, shift)


def _qkv_kernel(h_ref, w_ref, b_ref, qz_ref, kmean_ref, vt_ref, kaug_ref):
    acc = _dot(h_ref[...], w_ref[...]) + b_ref[...]
    qz_ref[:, :BRANCH_WIDTH] = acc[:, :BRANCH_WIDTH].astype(BF16)
    qz_ref[:, BRANCH_WIDTH:] = acc[:, 3 * BRANCH_WIDTH:].astype(BF16)
    hd = MOBA_HEAD_DIM
    lane = lax.broadcasted_iota(jnp.int32, (QKV_TM, LANES - hd), 1)
    row_in_block = (lax.broadcasted_iota(jnp.int32, (QKV_TM, LANES - hd), 0)
                    % MOBA_BLOCK).astype(F32)
    groups = []
    for h in range(MOBA_HEADS):
        k_h = acc[:, BRANCH_WIDTH + h * hd:BRANCH_WIDTH + (h + 1) * hd]
        pos = jnp.where(lane < ALIBI_PARTS, (2.0 ** -(h + 1)) * row_in_block, 0.0)
        groups += [k_h, pos]
    kaug_ref[...] = jnp.concatenate(groups, axis=1).astype(BF16)
    for r in range(QKV_TM // MOBA_BLOCK):
        rows = slice(r * MOBA_BLOCK, (r + 1) * MOBA_BLOCK)
        kmean_ref[0, r:r + 1, :] = jnp.mean(acc[rows, BRANCH_WIDTH:2 * BRANCH_WIDTH],
                                            axis=0, keepdims=True)
        vt_ref[r] = jnp.transpose(acc[rows, 2 * BRANCH_WIDTH:3 * BRANCH_WIDTH]).astype(BF16)


def _qkv_proj(h, w, b):
    tm = QKV_TM
    n = 4 * BRANCH_WIDTH
    blocks_per_tile = tm // MOBA_BLOCK
    return pl.pallas_call(
        _qkv_kernel,
        grid=(M_ROWS // tm,),
        in_specs=[
            pl.BlockSpec((tm, D_MODEL), lambda i: (i, 0)),
            pl.BlockSpec((D_MODEL, n), lambda i: (0, 0), pipeline_mode=pl.Buffered(1)),
            pl.BlockSpec((1, n), lambda i: (0, 0)),
        ],
        out_specs=[
            pl.BlockSpec((tm, 2 * BRANCH_WIDTH), lambda i: (i, 0)),
            pl.BlockSpec((1, blocks_per_tile, BRANCH_WIDTH), lambda i: (i, 0, 0)),
            pl.BlockSpec((blocks_per_tile, BRANCH_WIDTH, MOBA_BLOCK), lambda i: (i, 0, 0)),
            pl.BlockSpec((tm, MOBA_HEADS * LANES), lambda i: (i, 0)),
        ],
        out_shape=[
            jax.ShapeDtypeStruct((M_ROWS, 2 * BRANCH_WIDTH), BF16),
            jax.ShapeDtypeStruct((M_ROWS // tm, blocks_per_tile, BRANCH_WIDTH), F32),
            jax.ShapeDtypeStruct((M_ROWS // MOBA_BLOCK, BRANCH_WIDTH, MOBA_BLOCK), BF16),
            jax.ShapeDtypeStruct((M_ROWS, MOBA_HEADS * LANES), BF16),
        ],
        compiler_params=_params("arbitrary"),
        name="qkv_proj",
    )(h, w, b)


def _uvg_kernel(h_ref, w_ref, b_ref, o_ref):
    h = h_ref[...]
    skipped = 4 * BRANCH_WIDTH
    for t in range(UVG_WIDTH // UVG_TN):
        src = t * UVG_TN if t == 0 else t * UVG_TN + skipped
        act = _gelu_tanh if t == 0 else _sigmoid2
        y = _dot(h, w_ref[:, src:src + UVG_TN]) + b_ref[:, src:src + UVG_TN]
        o_ref[:, t * UVG_TN:(t + 1) * UVG_TN] = act(y).astype(BF16)


def _uvg_proj(h, w_in, b_in):
    tm = UVG_TM
    return pl.pallas_call(
        _uvg_kernel,
        grid=(M_ROWS // tm,),
        in_specs=[
            pl.BlockSpec((tm, D_MODEL), lambda i: (i, 0)),
            pl.BlockSpec((D_MODEL, IN_WIDTH), lambda i: (0, 0), pipeline_mode=pl.Buffered(1)),
            pl.BlockSpec((1, IN_WIDTH), lambda i: (0, 0)),
        ],
        out_specs=pl.BlockSpec((tm, UVG_WIDTH), lambda i: (i, 0)),
        out_shape=jax.ShapeDtypeStruct((M_ROWS, UVG_WIDTH), BF16),
        compiler_params=_params("arbitrary"),
        name="uvg_proj",
    )(h, w_in, b_in)


def _branch_kernel(u_ref, v_ref, z_ref, zprev_ref, ng_ref, sw_ref, sb_ref, pw_ref, ps_ref,
                   ysgu_ref, ypool_ref):
    i = pl.program_id(0)
    tr = BRANCH_TR
    first = (i % (SEQ // tr)) == 0

    v = v_ref[...].astype(F32)
    mu = jnp.mean(v, axis=-1, keepdims=True)
    vc = v - mu
    var = jnp.mean(vc * vc, axis=-1, keepdims=True)
    vn = (vc * lax.rsqrt(var + EPS) * ng_ref[...]).astype(BF16)

    row = lax.broadcasted_iota(jnp.int32, (SGU_CHUNK, SGU_CHUNK), 0)
    col = lax.broadcasted_iota(jnp.int32, (SGU_CHUNK, SGU_CHUNK), 1)
    causal = row >= col
    wm = [jnp.where(causal, sw_ref[g], 0.0).astype(BF16) for g in range(SGU_GROUPS)]
    lane = lax.broadcasted_iota(jnp.int32, (SGU_CHUNK, LANES), 1)
    low_half = lane < SGU_GROUP_DIM

    n_chunks = tr // SGU_CHUNK
    for pr in range(SGU_GROUPS // 2):
        cols = slice(pr * LANES, (pr + 1) * LANES)
        blk = jnp.concatenate([vn[c * SGU_CHUNK:(c + 1) * SGU_CHUNK, cols] for c in range(n_chunks)],
                              axis=1)
        low = jnp.concatenate([low_half] * n_chunks, axis=1)
        mixed = (_dot(wm[2 * pr], jnp.where(low, blk, jnp.zeros_like(blk)))
                 + _dot(wm[2 * pr + 1], jnp.where(low, jnp.zeros_like(blk), blk)))
        for c in range(n_chunks):
            rows = slice(c * SGU_CHUNK, (c + 1) * SGU_CHUNK)
            m_c = mixed[:, c * LANES:(c + 1) * LANES] + sb_ref[:, cols]
            ysgu_ref[rows, cols] = (u_ref[rows, cols].astype(F32) * m_c).astype(BF16)

    z = z_ref[...].astype(F32)
    zprev = jnp.where(first, 0.0, zprev_ref[...].astype(F32))
    pos = (i % (SEQ // tr)) * tr + lax.broadcasted_iota(jnp.int32, (tr, POOL_GROUP_DIM), 0)
    for gi, w in enumerate(POOL_WINDOWS):
        cols = slice(gi * POOL_GROUP_DIM, (gi + 1) * POOL_GROUP_DIM)
        zg = z[:, cols]
        cur = jnp.concatenate([zprev[:, cols], zg], axis=0)
        d = 1
        while d < w:
            cur = cur[d:] + cur[:-d]
            d *= 2
        win_sum = cur[cur.shape[0] - tr:]
        count = jnp.minimum(pos + 1, w).astype(F32)
        pooled = win_sum / count - zg
        y = _dot(pooled.astype(BF16), pw_ref[gi].astype(BF16)) * ps_ref[:, cols]
        ypool_ref[:, cols] = y.astype(BF16)


def _branches(uvg, qkvz, sgu_norm_g, sgu_w, sgu_bias_full, pool_w, pool_scale):
    tr = BRANCH_TR
    halo_blocks = tr // POOL_HALO
    return pl.pallas_call(
        _branch_kernel,
        grid=(M_ROWS // tr,),
        in_specs=[
            pl.BlockSpec((tr, BRANCH_WIDTH), lambda i: (i, 0)),
            pl.BlockSpec((tr, BRANCH_WIDTH), lambda i: (i, 1)),
            pl.BlockSpec((tr, BRANCH_WIDTH), lambda i: (i, 1)),
            pl.BlockSpec((POOL_HALO, BRANCH_WIDTH),
                         lambda i: (jnp.maximum(i * halo_blocks - 1, 0), 1)),
            pl.BlockSpec((1, BRANCH_WIDTH), lambda i: (0, 0)),
            pl.BlockSpec((SGU_GROUPS, SGU_CHUNK, SGU_CHUNK), lambda i: (0, 0, 0)),
            pl.BlockSpec((SGU_CHUNK, BRANCH_WIDTH), lambda i: (0, 0)),
            pl.BlockSpec((POOL_GROUPS, POOL_GROUP_DIM, POOL_GROUP_DIM), lambda i: (0, 0, 0)),
            pl.BlockSpec((1, BRANCH_WIDTH), lambda i: (0, 0)),
        ],
        out_specs=[
            pl.BlockSpec((tr, BRANCH_WIDTH), lambda i: (i, 0)),
            pl.BlockSpec((tr, BRANCH_WIDTH), lambda i: (i, 0)),
        ],
        out_shape=[
            jax.ShapeDtypeStruct((M_ROWS, BRANCH_WIDTH), BF16),
            jax.ShapeDtypeStruct((M_ROWS, BRANCH_WIDTH), BF16),
        ],
        compiler_params=_params("arbitrary"),
        name="sgu_pool",
    )(uvg, uvg, qkvz, qkvz, sgu_norm_g, sgu_w, sgu_bias_full, pool_w, pool_scale)


def _attn_kernel(q_ref, k_ref, vt_ref, km_ref, o_ref,
                 qs_ref, pick_ref, m_ref, acc_ref, s_ref, smax_ref):
    pair = pl.program_id(1)
    tq, blk, hd = ATTN_TQ, MOBA_BLOCK, MOBA_HEAD_DIM
    own = [2 * pair, 2 * pair + 1]
    own_f = [o.astype(F32) for o in own]
    tiles, heads = range(2), range(MOBA_HEADS)

    key_row = lax.broadcasted_iota(jnp.int32, (blk, tq), 0)
    causal = jnp.where(key_row > lax.broadcasted_iota(jnp.int32, (blk, tq), 1), NEG, 0.0)
    blk_id = lax.broadcasted_iota(jnp.int32, (N_KV_BLOCKS, tq), 0)
    blk_id_f = blk_id.astype(F32)
    ones_rows = jnp.ones((ONES_ROWS, blk), BF16)
    part_row = lax.broadcasted_iota(jnp.int32, (LANES - hd, tq), 0)
    q_alibi = jnp.zeros((LANES - hd, tq), F32)
    for t, part in enumerate(LOG2E_PARTS):
        q_alibi = jnp.where(part_row == t, part, q_alibi)

    def raw_scores(j, x, h):
        start = pl.multiple_of(j * blk, blk)
        return _dot(k_ref[pl.ds(start, blk), h * LANES:(h + 1) * LANES], qs_ref[x, h])

    def values(j, h):
        return jnp.concatenate([vt_ref[j, h * hd:(h + 1) * hd, :], ones_rows], axis=0)

    def scores(j, x, h, slot):
        s = raw_scores(j, x, h)
        s_ref[slot, x, h] = s
        smax_ref[slot, x, h] = jnp.max(s, axis=0, keepdims=True)

    def absorb(j, x, h, slot):
        jf = j.astype(F32)
        slope = 2.0 ** -(h + 1)
        picks = pick_ref[x, h]
        chosen = (picks[0] == jf) | (picks[1] == jf) | (picks[2] == jf)
        shift = jnp.where(chosen, (-slope * blk * LOG2E) * (own_f[x] - jf), NEG)
        m = m_ref[x, h]
        m_new = jnp.maximum(m, smax_ref[slot, x, h] + shift)
        p = jnp.exp2(s_ref[slot, x, h] - (m_new - shift))
        m_ref[x, h] = m_new
        acc_ref[x, h] = (jnp.exp2(m - m_new) * acc_ref[x, h]
                         + _dot(values(j, h), p.astype(BF16)))

    gate_scores = {}
    for x in tiles:
        q_t = jnp.transpose(q_ref[x * tq:(x + 1) * tq, :].astype(F32))
        past = blk_id < own[x]
        for h in heads:
            qh = q_t[h * hd:(h + 1) * hd]
            qs_ref[x, h] = jnp.concatenate([qh * (hd ** -0.5 * LOG2E), q_alibi],
                                           axis=0).astype(BF16)
            km = km_ref[0, :, h * hd:(h + 1) * hd]
            km_hi = km.astype(BF16)
            km_lo = (km - km_hi.astype(F32)).astype(BF16)
            qg = qh.astype(BF16)
            sc = _dot(km_hi, qg) + _dot(km_lo, qg)
            gate_scores[x, h] = jnp.where(past, sc, -jnp.inf)

    for x in tiles:
        for h in heads:
            s = raw_scores(own[x], x, h) + causal
            s_ref[1, x, h] = s
            m_ref[x, h] = jnp.max(s, axis=0, keepdims=True)

    for r in range(MOBA_TOPK):
        for x in tiles:
            for h in heads:
                sc = gate_scores[x, h]
                best = jnp.max(sc, axis=0, keepdims=True)
                idx = jnp.min(jnp.where(sc == best, blk_id_f, 1e9), axis=0, keepdims=True)
                pick_ref[x, h, r] = jnp.where(best > -jnp.inf, idx, -1.0)
                gate_scores[x, h] = jnp.where(blk_id_f == idx, -jnp.inf, sc)

    for x in tiles:
        for h in heads:
            acc_ref[x, h] = _dot(values(own[x], h),
                                 jnp.exp2(s_ref[1, x, h] - m_ref[x, h]).astype(BF16))

    for h in heads:
        scores(own[0], 1, h, 0)
    for h in heads:
        absorb(own[0], 1, h, 0)

    for x in tiles:
        for h in heads:
            scores(0, x, h, 0)

    def body(t, carry):
        a = 2 * t
        for x in tiles:
            for h in heads:
                scores(a + 1, x, h, 1)
                absorb(a, x, h, 0)
        for x in tiles:
            for h in heads:
                scores(a + 2, x, h, 0)
                absorb(a + 1, x, h, 1)
        return carry

    lax.fori_loop(0, pair, body, 0)

    for x in tiles:
        o_t = jnp.concatenate([acc_ref[x, h, :hd] / acc_ref[x, h, hd:hd + 1] for h in heads],
                              axis=0)
        o_ref[x * tq:(x + 1) * tq, :] = jnp.transpose(o_t).astype(BF16)


def _attention(qkvz, k_aug, kmean, v_t):
    tq = 2 * ATTN_TQ
    q_tiles = SEQ // tq
    return pl.pallas_call(
        _attn_kernel,
        grid=(BATCH, q_tiles),
        in_specs=[
            pl.BlockSpec((tq, BRANCH_WIDTH), lambda b, i: (b * q_tiles + i, 0)),
            pl.BlockSpec((SEQ, MOBA_HEADS * LANES), lambda b, i: (b, 0),
                         pipeline_mode=pl.Buffered(1)),
            pl.BlockSpec((N_KV_BLOCKS, BRANCH_WIDTH, MOBA_BLOCK), lambda b, i: (b, 0, 0)),
            pl.BlockSpec((1, N_KV_BLOCKS, BRANCH_WIDTH), lambda b, i: (b, 0, 0)),
        ],
        out_specs=pl.BlockSpec((tq, BRANCH_WIDTH), lambda b, i: (b * q_tiles + i, 0)),
        out_shape=jax.ShapeDtypeStruct((M_ROWS, BRANCH_WIDTH), BF16),
        scratch_shapes=[
            pltpu.VMEM((2, MOBA_HEADS, LANES, ATTN_TQ), BF16),
            pltpu.VMEM((2, MOBA_HEADS, MOBA_TOPK, 1, ATTN_TQ), F32),
            pltpu.VMEM((2, MOBA_HEADS, 1, ATTN_TQ), F32),
            pltpu.VMEM((2, MOBA_HEADS, MOBA_HEAD_DIM + ONES_ROWS, ATTN_TQ), F32),
            pltpu.VMEM((2, 2, MOBA_HEADS, MOBA_BLOCK, ATTN_TQ), F32),
            pltpu.VMEM((2, 2, MOBA_HEADS, 1, ATTN_TQ), F32),
        ],
        compiler_params=_params("arbitrary", "arbitrary"),
        name="moba_attention",
    )(qkvz, k_aug, v_t, kmean)


def _merge_kernel(ys_ref, ym_ref, yp_ref, gs0, gs1, gm0, gm1, gp0, gp1,
                  ws_ref, wm_ref, wp_ref, wo_ref, x_ref, g_ref, gt_ref, g2_ref, sc2_ref, sh2_ref,
                  o_ref, h_ref):
    half_w = D_MODEL // 2
    for r0 in range(0, MERGE_TM, MERGE_ROWS):
        rows = slice(r0, r0 + MERGE_ROWS)
        ys, ym, yp = ys_ref[rows, :], ym_ref[rows, :], yp_ref[rows, :]
        halves = []
        for hf, (gs, gm, gp) in enumerate(((gs0, gm0, gp0), (gs1, gm1, gp1))):
            cols = slice(hf * half_w, (hf + 1) * half_w)
            merged = (gs[rows, :].astype(F32) * _dot(ys, ws_ref[:, cols])
                      + gm[rows, :].astype(F32) * _dot(ym, wm_ref[:, cols])
                      + gp[rows, :].astype(F32) * _dot(yp, wp_ref[:, cols]))
            halves.append(merged.astype(BF16))
        y = _dot(jnp.concatenate(halves, axis=1), wo_ref[...])
        x_new = x_ref[rows, :] + _rms_norm(y, g_ref[...] * gt_ref[0])
        o_ref[rows, :] = x_new
        h_ref[rows, :] = _modulated_norm(x_new, g2_ref[...], sc2_ref[0], sh2_ref[0])


def _merge(ysgu, ymoba, ypool, uvg, w_sgu_out, w_moba_out, w_pool_out, w_out, x, g, gate,
           g_next, scale_next, shift_next):
    tm = MERGE_TM
    tiles_per_seq = SEQ // tm
    half_w = D_MODEL // 2
    gate_col0 = 2 * BRANCH_WIDTH // half_w
    resident = functools.partial(pl.BlockSpec, pipeline_mode=pl.Buffered(1))
    branch = pl.BlockSpec((tm, BRANCH_WIDTH), lambda i: (i, 0))
    gate_specs = [pl.BlockSpec((tm, half_w), functools.partial(lambda i, c: (i, c), c=gate_col0 + c))
                  for c in range(6)]
    w_branch = resident((BRANCH_WIDTH, D_MODEL), lambda i: (0, 0))
    rows = pl.BlockSpec((tm, D_MODEL), lambda i: (i, 0))
    vec = pl.BlockSpec((1, D_MODEL), lambda i: (0, 0))
    per_batch = pl.BlockSpec((1, 1, D_MODEL), lambda i: (i // tiles_per_seq, 0, 0))
    return pl.pallas_call(
        _merge_kernel,
        grid=(M_ROWS // tm,),
        in_specs=[branch, branch, branch, *gate_specs, w_branch, w_branch, w_branch,
                  resident((D_MODEL, D_MODEL), lambda i: (0, 0)),
                  rows, vec, per_batch, vec, per_batch, per_batch],
        out_specs=[rows, rows],
        out_shape=[jax.ShapeDtypeStruct((M_ROWS, D_MODEL), F32),
                   jax.ShapeDtypeStruct((M_ROWS, D_MODEL), BF16)],
        compiler_params=_params("arbitrary"),
        name="merge_outproj",
    )(ysgu, ymoba, ypool, uvg, uvg, uvg, uvg, uvg, uvg, w_sgu_out, w_moba_out, w_pool_out, w_out,
      x, g, gate, g_next, scale_next, shift_next)


def _up_kernel(h_ref, wg_ref, wv_ref, cg_ref, cv_ref, bg_ref, bv_ref, a_ref,
               zg_ref, zv_ref, tail_g_ref, tail_v_ref):
    tm, sub = UP_TM, SUBLANES
    j = pl.program_id(1)

    @pl.when((pl.program_id(0) % (SEQ // tm)) == 0)
    def _():
        tail_g_ref[j] = jnp.zeros(tail_g_ref.shape[1:], F32)
        tail_v_ref[j] = jnp.zeros(tail_v_ref.shape[1:], F32)

    h = h_ref[...]

    def conv(w_ref, z_ref, tail_ref, cw_ref, cb_ref):
        z = _dot(h, w_ref[...])
        taps = []
        for s in range(UP_TN // LANES):
            cols = slice(s * LANES, (s + 1) * LANES)
            z_ref[s, :sub, :] = tail_ref[j, s]
            z_ref[s, sub:, :] = z[:, cols]
            tail_ref[j, s] = z[tm - sub:, cols]
            taps.append(cw_ref[0:1, cols] * z_ref[s, sub - 2:sub - 2 + tm, :]
                        + cw_ref[1:2, cols] * z_ref[s, sub - 1:sub - 1 + tm, :]
                        + cw_ref[2:3, cols] * z[:, cols] + cb_ref[:, cols])
        return jnp.concatenate(taps, axis=1)

    gate = conv(wg_ref, zg_ref, tail_g_ref, cg_ref, bg_ref)
    val = conv(wv_ref, zv_ref, tail_v_ref, cv_ref, bv_ref)
    a_ref[...] = (_gelu_tanh(gate) * val).astype(BF16)


def _ffn_up(h, w_up, w_conv, b_conv):
    tm, tn = UP_TM, UP_TN
    nj = D_FF // tn
    slabs = tn // LANES
    return pl.pallas_call(
        _up_kernel,
        grid=(M_ROWS // tm, nj),
        in_specs=[
            pl.BlockSpec((tm, D_MODEL), lambda i, j: (i, 0)),
            pl.BlockSpec((D_MODEL, tn), lambda i, j: (0, j)),
            pl.BlockSpec((D_MODEL, tn), lambda i, j: (0, nj + j)),
            pl.BlockSpec((3, tn), lambda i, j: (0, j)),
            pl.BlockSpec((3, tn), lambda i, j: (0, nj + j)),
            pl.BlockSpec((1, tn), lambda i, j: (0, j)),
            pl.BlockSpec((1, tn), lambda i, j: (0, nj + j)),
        ],
        out_specs=pl.BlockSpec((tm, tn), lambda i, j: (i, j)),
        out_shape=jax.ShapeDtypeStruct((M_ROWS, D_FF), BF16),
        scratch_shapes=([pltpu.VMEM((slabs, tm + SUBLANES, LANES), F32) for _ in range(2)]
                        + [pltpu.VMEM((nj, slabs, SUBLANES, LANES), F32) for _ in range(2)]),
        compiler_params=_params("arbitrary", "arbitrary"),
        name="ffn_up",
    )(h, w_up, w_up, w_conv, w_conv, b_conv, b_conv)


def _down_kernel(emit_h, a_ref, w_ref, x_ref, g_ref, gt_ref, *rest):
    if emit_h:
        g2_ref, sc2_ref, sh2_ref, o_ref, h_ref = rest
    else:
        (o_ref,) = rest
    for r0 in range(0, DOWN_TM, DOWN_ROWS):
        rows = slice(r0, r0 + DOWN_ROWS)
        y = _dot(a_ref[rows, :], w_ref[...])
        x_new = x_ref[rows, :] + _rms_norm(y, g_ref[...] * gt_ref[0])
        o_ref[rows, :] = x_new
        if emit_h:
            h_ref[rows, :] = _modulated_norm(x_new, g2_ref[...], sc2_ref[0], sh2_ref[0])


def _ffn_down(a, w_down, x, g, gate, next_norm=None):
    tm = DOWN_TM
    tiles_per_seq = SEQ // tm
    rows = pl.BlockSpec((tm, D_MODEL), lambda i: (i, 0))
    vec = pl.BlockSpec((1, D_MODEL), lambda i: (0, 0))
    per_batch = pl.BlockSpec((1, 1, D_MODEL), lambda i: (i // tiles_per_seq, 0, 0))
    emit_h = next_norm is not None
    in_specs = [pl.BlockSpec((tm, D_FF), lambda i: (i, 0)),
                pl.BlockSpec((D_FF, D_MODEL), lambda i: (0, 0), pipeline_mode=pl.Buffered(1)),
                rows, vec, per_batch]
    out_specs = [rows]
    out_shape = [jax.ShapeDtypeStruct((M_ROWS, D_MODEL), F32)]
    args = [a, w_down, x, g, gate]
    if emit_h:
        in_specs += [vec, per_batch, per_batch]
        out_specs.append(rows)
        out_shape.append(jax.ShapeDtypeStruct((M_ROWS, D_MODEL), BF16))
        args += list(next_norm)
    return pl.pallas_call(
        functools.partial(_down_kernel, emit_h),
        grid=(M_ROWS // tm,),
        in_specs=in_specs,
        out_specs=out_specs,
        out_shape=out_shape,
        compiler_params=_params("arbitrary"),
        name="ffn_down",
    )(*args)


def kernel(x, c, g_pre_mix, g_post_mix, g_pre_ffn, g_post_ffn, w_ada, b_ada, w_in, b_in,
           sgu_norm_g, sgu_w, sgu_b, pool_w, pool_scale, w_sgu_out, w_moba_out, w_pool_out,
           w_out, w_up, w_conv, b_conv, w_down):
    c_pad = jnp.pad(c, ((0, SUBLANES - BATCH), (0, 0)))
    mod = _modulation(c_pad, w_ada, b_ada)[:, :BATCH]
    mod = mod.reshape(DEPTH, BATCH, N_MOD, 1, D_MODEL)
    shift1, scale1, gate1, shift2, scale2, gate2 = [mod[:, :, n] for n in range(N_MOD)]
    row = lambda v, l: v[l].reshape(1, -1)
    qkv_cols = slice(2 * BRANCH_WIDTH, 6 * BRANCH_WIDTH)

    xf = x.reshape(M_ROWS, D_MODEL)
    h = _prenorm(xf, row(g_pre_mix, 0), scale1[0], shift1[0])
    for l in range(DEPTH):
        w_in_l = w_in[l].astype(BF16)
        uvg = _uvg_proj(h, w_in_l, row(b_in, l))
        qkvz, kmean, v_t, k_aug = _qkv_proj(h, w_in_l[:, qkv_cols],
                                            b_in[l, qkv_cols].reshape(1, -1))
        kmean = kmean.reshape(BATCH, N_KV_BLOCKS, BRANCH_WIDTH)

        sgu_bias_full = jnp.repeat(sgu_b[l].T, SGU_GROUP_DIM, axis=1)
        ysgu, ypool = _branches(uvg, qkvz, row(sgu_norm_g, l), sgu_w[l], sgu_bias_full, pool_w[l],
                                row(pool_scale, l))
        ymoba = _attention(qkvz, k_aug, kmean, v_t)
        xf, h = _merge(ysgu, ymoba, ypool, uvg, w_sgu_out[l].astype(BF16),
                       w_moba_out[l].astype(BF16), w_pool_out[l].astype(BF16),
                       w_out[l].astype(BF16), xf, row(g_post_mix, l), gate1[l],
                       row(g_pre_ffn, l), scale2[l], shift2[l])

        a = _ffn_up(h, w_up[l].astype(BF16), w_conv[l], row(b_conv, l))
        if l + 1 < DEPTH:
            xf, h = _ffn_down(a, w_down[l].astype(BF16), xf, row(g_post_ffn, l), gate2[l],
                              (row(g_pre_mix, l + 1), scale1[l + 1], shift1[l + 1]))
        else:
            (xf,) = _ffn_down(a, w_down[l].astype(BF16), xf, row(g_post_ffn, l), gate2[l])
    return xf.reshape(BATCH, SEQ, D_MODEL)
```
